```python
import math
import jax, jax.numpy as jnp
from jax import lax
import numpy as np

D_MODEL = 1024
BATCH = 4
SEQ = 4096
DEPTH = 4
DEC_BATCH = 32
DEC_SEQ = 8
PAST_LEN = 8192
PAGE_SIZE = 128

HEAD_DIM = 64
POOL_WINDOWS = (2, 4, 8, 16)
POOL_GROUP = D_MODEL // 16
POOL_WIDTH = POOL_GROUP * len(POOL_WINDOWS)
POOL_BUF = max(POOL_WINDOWS) - 1
ATT_WIDTH = 3 * D_MODEL // 8
ATT_HEADS = ATT_WIDTH // HEAD_DIM
DILATED_GROUPS = ((128, 1), (512, 4), (2048, 16))
ATT_BUF = max(w for w, _ in DILATED_GROUPS)
ROPE_DIM = HEAD_DIM // 4
ROPE_THETA = 500000.0
NEG_INF = -1e30
RWKV_WIDTH = 3 * D_MODEL // 8
RWKV_HEADS = RWKV_WIDTH // HEAD_DIM
DECAY_LORA = 64
ICLR_LORA = 64
GATE_LORA = 128
RWKV_PROJ = 3 * RWKV_WIDTH + DECAY_LORA + ICLR_LORA + GATE_LORA
GN_EPS = HEAD_DIM * 1e-5
MIX_WIDTH = POOL_WIDTH + ATT_WIDTH + RWKV_WIDTH
IN_WIDTH = POOL_WIDTH + 3 * ATT_WIDTH + RWKV_PROJ
N_KEYS = 128
N_EXPERTS = N_KEYS * N_KEYS
PEER_HEADS = 8
PEER_TOPK = 16
PEER_QDIM = 256
PEER_BLOCK = 128
RMS_EPS = 1e-6

kernel_name = "hybrid_pool_dilattn_rwkv7_peer_step"


def rms_norm(x, g):
    xf = x.astype(jnp.float32)
    y = xf * lax.rsqrt(jnp.mean(xf * xf, axis=-1, keepdims=True) + RMS_EPS)
    return (y * g.astype(jnp.float32)).astype(x.dtype)


def rope(x, pos):
    half = ROPE_DIM // 2
    inv = ROPE_THETA ** (-jnp.arange(half, dtype=jnp.float32) * 2.0 / ROPE_DIM)
    ang = pos.astype(jnp.float32)[:, None] * inv[None, :]
    cos = jnp.cos(ang)[None, :, None, :]
    sin = jnp.sin(ang)[None, :, None, :]
    xr = x[..., :ROPE_DIM].astype(jnp.float32)
    x1, x2 = xr[..., :half], xr[..., half:]
    rot = jnp.concatenate([x1 * cos - x2 * sin, x2 * cos + x1 * sin], axis=-1).astype(x.dtype)
    return jnp.concatenate([rot, x[..., ROPE_DIM:]], axis=-1)


def pool_mixer(x_new, buf, pos0, w, scale):
    B, L, _ = x_new.shape
    G = len(POOL_WINDOWS)
    xfull = jnp.concatenate([buf.astype(x_new.dtype), x_new], axis=1)
    xg = xfull.astype(jnp.float32).reshape(B, POOL_BUF + L, G, POOL_GROUP)
    csum = jnp.concatenate([jnp.zeros((B, 1, G, POOL_GROUP), jnp.float32), jnp.cumsum(xg, axis=1)], axis=1)
    win = jnp.array(POOL_WINDOWS, jnp.int32)
    t = jnp.arange(L, dtype=jnp.int32)
    hi = t + POOL_BUF + 1
    lo = hi[:, None] - win[None, :]
    gi = jnp.arange(G)[None, :]
    sums = csum[:, hi] - csum[:, lo, gi]
    cnt = jnp.minimum(win[None, :], pos0 + t[:, None] + 1).astype(jnp.float32)
    diff = sums / cnt[None, :, :, None] - xg[:, POOL_BUF:]
    y = jnp.einsum('blgc,gcd->blgd', diff, w.astype(jnp.float32)).reshape(B, L, POOL_WIDTH)
    y = y * scale.astype(jnp.float32)
    return y.astype(x_new.dtype), xfull[:, -POOL_BUF:].astype(buf.dtype)


def _combine_branches(outs, lses):
    wts = jax.nn.softmax(jnp.stack(lses, 0), axis=0)
    return jnp.sum(wts[..., None] * jnp.stack(outs, 0), axis=0)


def dilated_attn_prompt(q, k, v):
    B, S, H, Dh = q.shape
    scale = HEAD_DIM ** -0.5
    outs, lses = [], []
    for window, dil in DILATED_GROUPS:
        nb = window // dil
        blk = nb
        span = dil * blk
        s_pad = -(-S // span) * span
        M = s_pad // dil
        nblk = M // blk

        def to_res(a):
            a = jnp.pad(a, ((0, 0), (0, s_pad - S), (0, 0), (0, 0)))
            a = a.reshape(B, M, dil, H, Dh).transpose(0, 2, 1, 3, 4)
            return a.reshape(B, dil, nblk, blk, H, Dh)

        def with_prev(a):
            prev = jnp.pad(a, ((0, 0), (0, 0), (1, 0), (0, 0), (0, 0), (0, 0)))[:, :, :-1]
            return jnp.concatenate([prev, a], axis=3)

        qr = to_res(q)
        kb = with_prev(to_res(k))
        vb = with_prev(to_res(v))
        s = jnp.einsum('brnqhd,brnkhd->brnhqk', qr, kb, preferred_element_type=jnp.float32) * scale
        iq = jnp.arange(blk)[:, None]
        ik = jnp.arange(2 * blk)[None, :]
        rel = blk + iq - ik
        mk = jnp.arange(nblk)[:, None, None] * blk + ik[None] - blk
        valid = (rel >= 0)[None] & (rel <= nb)[None] & (mk >= 0)
        s = jnp.where(valid[None, None, :, None], s, NEG_INF)
        m = jnp.max(s, axis=-1, keepdims=True)
        p = jnp.exp(s - m)
        den = jnp.sum(p, axis=-1, keepdims=True)
        o = jnp.einsum('brnhqk,brnkhd->brnqhd', p / den, vb.astype(jnp.float32))
        lse = (m + jnp.log(den))[..., 0]
        o = o.reshape(B, dil, M, H, Dh).transpose(0, 2, 1, 3, 4).reshape(B, s_pad, H, Dh)[:, :S]
        lse = lse.transpose(0, 1, 2, 4, 3).reshape(B, dil, M, H).transpose(0, 2, 1, 3).reshape(B, s_pad, H)[:, :S]
        outs.append(o)
        lses.append(lse)
    return _combine_branches(outs, lses).astype(q.dtype)


def dilated_attn_sample(q, k_all, v_all, n_past):
    B, T, H, Dh = q.shape
    scale = HEAD_DIM ** -0.5
    outs, lses = [], []
    for window, dil in DILATED_GROUPS:
        nb = window // dil
        idx = n_past + jnp.arange(T)[:, None] - dil * jnp.arange(nb + 1)[None, :]
        valid = idx >= 0
        idc = jnp.maximum(idx, 0)
        kg = k_all[:, idc]
        vg = v_all[:, idc]
        s = jnp.einsum('bthd,btkhd->bthk', q, kg, preferred_element_type=jnp.float32) * scale
        s = jnp.where(valid[None, :, None, :], s, NEG_INF)
        m = jnp.max(s, axis=-1, keepdims=True)
        p = jnp.exp(s - m)
        den = jnp.sum(p, axis=-1, keepdims=True)
        o = jnp.einsum('bthk,btkhd->bthd', p / den, vg.astype(jnp.float32))
        outs.append(o)
        lses.append((m + jnp.log(den))[..., 0])
    return _combine_branches(outs, lses).astype(q.dtype)


def rwkv_mixer(p, prev, wkv0, mu, w0, w_up, a0, a_up, g_up, k_k, k_a, r_k, ln_w, ln_b):
    B, L, _ = p.shape
    f32 = jnp.float32
    pf = p.astype(f32)
    p_prev = jnp.concatenate([prev.astype(f32)[:, None, :], pf[:, :-1]], axis=1)
    ps = pf + (p_prev - pf) * mu.astype(f32)
    c1, c2, c3 = RWKV_WIDTH, 2 * RWKV_WIDTH, 3 * RWKV_WIDTH
    c4 = c3 + DECAY_LORA
    c5 = c4 + ICLR_LORA
    r, k, v = ps[..., :c1], ps[..., c1:c2], ps[..., c2:c3]
    wd, ad, gd = ps[..., c3:c4], ps[..., c4:c5], ps[..., c5:]
    w_log = -jax.nn.softplus(-(w0.astype(f32) + jnp.tanh(wd) @ w_up.astype(f32))) - 0.5
    decay = jnp.exp(-jnp.exp(w_log))
    a = jax.nn.sigmoid(a0.astype(f32) + ad @ a_up.astype(f32))
    g = jax.nn.sigmoid(gd) @ g_up.astype(f32)
    hd = lambda t: t.reshape(B, L, RWKV_HEADS, HEAD_DIM)
    kk = hd(k * k_k.astype(f32))
    kk = kk / jnp.maximum(jnp.sqrt(jnp.sum(kk * kk, axis=-1, keepdims=True)), 1e-12)
    k = k * (1.0 + (a - 1.0) * k_a.astype(f32))
    r_h, k_h, v_h, w_h, a_h = hd(r), hd(k), hd(v), hd(decay), hd(a)

    def step(S, inp):
        r_t, w_t, k_t, v_t, kk_t, a_t = inp
        sa = jnp.einsum('bhij,bhj->bhi', S, -kk_t)
        S = (S * w_t[:, :, None, :] + sa[..., None] * (kk_t * a_t)[:, :, None, :]
             + v_t[..., None] * k_t[:, :, None, :])
        return S, jnp.einsum('bhij,bhj->bhi', S, r_t)

    xs = tuple(jnp.moveaxis(t, 1, 0) for t in (r_h, w_h, k_h, v_h, kk, a_h))
    S_T, o = lax.scan(step, wkv0.astype(f32), xs)
    o = jnp.moveaxis(o, 0, 1)
    mean = jnp.mean(o, axis=-1, keepdims=True)
    var = jnp.mean(jnp.square(o - mean), axis=-1, keepdims=True)
    o = ((o - mean) * lax.rsqrt(var + GN_EPS)).reshape(B, L, RWKV_WIDTH)
    o = o * ln_w.astype(f32) + ln_b.astype(f32)
    bonus = jnp.sum(r_h * k_h * r_k.astype(f32), axis=-1, keepdims=True) * v_h
    o = (o + bonus.reshape(B, L, RWKV_WIDTH)) * g
    return o.astype(p.dtype), p[:, -1].astype(prev.dtype), S_T.astype(wkv0.dtype)


def peer_ffn(x, wq, subkeys, u, v):
    B, L, D = x.shape
    T = B * L
    xt = x.reshape(T, D)
    q = (xt @ wq).reshape(T, PEER_HEADS, 2, PEER_QDIM // 2)
    sc = jnp.einsum('thcd,hcnd->thcn', q, subkeys, preferred_element_type=jnp.float32)
    s_half, i_half = lax.top_k(sc, PEER_TOPK)
    cand = (s_half[:, :, 0, :, None] + s_half[:, :, 1, None, :]).reshape(T, PEER_HEADS, PEER_TOPK * PEER_TOPK)
    cidx = (i_half[:, :, 0, :, None] * N_KEYS + i_half[:, :, 1, None, :]).reshape(T, PEER_HEADS, PEER_TOPK * PEER_TOPK)
    s_top, j = lax.top_k(cand, PEER_TOPK)
    eidx = jnp.take_along_axis(cidx, j, axis=-1)
    gate = jax.nn.softmax(s_top, axis=-1)
    pad = (-T) % PEER_BLOCK
    nblk = (T + pad) // PEER_BLOCK
    xb = jnp.pad(xt, ((0, pad), (0, 0))).reshape(nblk, PEER_BLOCK, D)
    eb = jnp.pad(eidx, ((0, pad), (0, 0), (0, 0))).reshape(nblk, PEER_BLOCK, PEER_HEADS, PEER_TOPK)
    gb = jnp.pad(gate, ((0, pad), (0, 0), (0, 0))).reshape(nblk, PEER_BLOCK, PEER_HEADS, PEER_TOPK)

    def expert_block(args):
        xs, es, gs = args
        h = jax.nn.gelu(jnp.einsum('td,thkd->thk', xs, u[es], preferred_element_type=jnp.float32), approximate=False)
        return jnp.einsum('thk,thkd->td', gs * h, v[es].astype(jnp.float32)).astype(x.dtype)

    out = lax.map(expert_block, (xb, eb, gb))
    return out.reshape(nblk * PEER_BLOCK, D)[:T].reshape(B, L, D)


def trunk_layer(h, lp, pos0, pool_buf, k_buf, v_buf, shift_prev, wkv0):
    B, L, _ = h.shape
    xn = rms_norm(h, lp['norm1_g'])
    proj = xn @ lp['w_in']
    o1 = POOL_WIDTH
    o2 = o1 + ATT_WIDTH
    o3 = o2 + ATT_WIDTH
    o4 = o3 + ATT_WIDTH
    heads = lambda t: t.reshape(B, L, ATT_HEADS, HEAD_DIM)
    pos = pos0 + jnp.arange(L, dtype=jnp.int32)
    q = rope(heads(proj[..., o1:o2]), pos)
    k = rope(heads(proj[..., o2:o3]), pos)
    v = heads(proj[..., o3:o4])
    pool_out, pool_new = pool_mixer(proj[..., :o1], pool_buf, pos0, lp['pool_w'], lp['pool_scale'])
    if k_buf is None:
        att = dilated_attn_prompt(q, k, v)
        keep = min(ATT_BUF, L)
        k_new, v_new = k[:, -keep:], v[:, -keep:]
    else:
        n_past = k_buf.shape[1]
        k_all = jnp.concatenate([k_buf.astype(k.dtype), k], axis=1)
        v_all = jnp.concatenate([v_buf.astype(v.dtype), v], axis=1)
        att = dilated_attn_sample(q, k_all, v_all, n_past)
        k_new, v_new = k_all[:, -n_past:], v_all[:, -n_past:]
    rw_out, shift_new, wkv_new = rwkv_mixer(
        proj[..., o4:], shift_prev, wkv0, lp['rwkv_mu'], lp['rwkv_w0'], lp['rwkv_w_up'], lp['rwkv_a0'],
        lp['rwkv_a_up'], lp['rwkv_g_up'], lp['rwkv_k_k'], lp['rwkv_k_a'], lp['rwkv_r_k'],
        lp['rwkv_ln_w'], lp['rwkv_ln_b'])
    mixed = jnp.concatenate([pool_out, att.reshape(B, L, ATT_WIDTH), rw_out], axis=-1)
    h = h + mixed @ lp['w_out']
    h = h + peer_ffn(rms_norm(h, lp['norm2_g']), lp['peer_wq'], lp['peer_subkeys'], lp['peer_u'], lp['peer_v'])
    return h, pool_new, k_new, v_new, shift_new, wkv_new


def setup_inputs(seed: int = 0) -> dict:
    key = jax.random.key(seed)
    ks = iter(jax.random.split(key, 40))
    nrm = lambda shape, s: jax.random.normal(next(ks), shape, jnp.float32) * s
    uni = lambda shape, lo, hi: jax.random.uniform(next(ks), shape, jnp.float32, lo, hi)
    n_att = min(ATT_BUF, PAST_LEN)
    return {
        'x_prompt': nrm((BATCH, SEQ, D_MODEL), 1.0),
        'x_sample': nrm((DEC_BATCH, DEC_SEQ, D_MODEL), 1.0),
        'state_pool': nrm((DEPTH, DEC_BATCH, POOL_BUF, POOL_WIDTH), 1.0),
        'cache_k': nrm((DEPTH, DEC_BATCH, n_att, ATT_HEADS, HEAD_DIM), 1.0),
        'cache_v': nrm((DEPTH, DEC_BATCH, n_att, ATT_HEADS, HEAD_DIM), 1.0),
        'state_shift': nrm((DEPTH, DEC_BATCH, RWKV_PROJ), 1.0),
        'state_wkv': nrm((DEPTH, DEC_BATCH, RWKV_HEADS, HEAD_DIM, HEAD_DIM), 0.1),
        'norm1_g': 1.0 + nrm((DEPTH, D_MODEL), 0.02),
        'norm2_g': 1.0 + nrm((DEPTH, D_MODEL), 0.02),
        'final_g': 1.0 + nrm((D_MODEL,), 0.02),
        'w_in': nrm((DEPTH, D_MODEL, IN_WIDTH), D_MODEL ** -0.5),
        'w_out': nrm((DEPTH, MIX_WIDTH, D_MODEL), MIX_WIDTH ** -0.5),
        'pool_w': nrm((DEPTH, len(POOL_WINDOWS), POOL_GROUP, POOL_GROUP), POOL_GROUP ** -0.5),
        'pool_scale': 1.0 + nrm((DEPTH, POOL_WIDTH), 0.1),
        'rwkv_mu': uni((DEPTH, RWKV_PROJ), 0.0, 1.0),
        'rwkv_w0': uni((DEPTH, RWKV_WIDTH), -6.0, -1.0),
        'rwkv_w_up': nrm((DEPTH, DECAY_LORA, RWKV_WIDTH), 0.1),
        'rwkv_a0': nrm((DEPTH, RWKV_WIDTH), 0.1),
        'rwkv_a_up': nrm((DEPTH, ICLR_LORA, RWKV_WIDTH), ICLR_LORA ** -0.5),
        'rwkv_g_up': nrm((DEPTH, GATE_LORA, RWKV_WIDTH), GATE_LORA ** -0.5),
        'rwkv_k_k': 0.85 + nrm((DEPTH, RWKV_WIDTH), 0.02),
        'rwkv_k_a': 1.0 + nrm((DEPTH, RWKV_WIDTH), 0.02),
        'rwkv_r_k': nrm((DEPTH, RWKV_HEADS, HEAD_DIM), 0.1),
        'rwkv_ln_w': 1.0 + nrm((DEPTH, RWKV_WIDTH), 0.02),
        'rwkv_ln_b': nrm((DEPTH, RWKV_WIDTH), 0.02),
        'peer_wq': nrm((DEPTH, D_MODEL, PEER_HEADS * PEER_QDIM), D_MODEL ** -0.5),
        'peer_subkeys': nrm((DEPTH, PEER_HEADS, 2, N_KEYS, PEER_QDIM // 2), (PEER_QDIM // 2) ** -0.5),
        'peer_u': nrm((DEPTH, N_EXPERTS, D_MODEL), D_MODEL ** -0.5),
        'peer_v': nrm((DEPTH, N_EXPERTS, D_MODEL), (PEER_HEADS * PEER_TOPK) ** -0.5),
    }


def reference(x_prompt, x_sample, state_pool, cache_k, cache_v, state_shift, state_wkv,
              norm1_g, norm2_g, final_g, w_in, w_out, pool_w, pool_scale,
              rwkv_mu, rwkv_w0, rwkv_w_up, rwkv_a0, rwkv_a_up, rwkv_g_up, rwkv_k_k, rwkv_k_a,
              rwkv_r_k, rwkv_ln_w, rwkv_ln_b, peer_wq, peer_subkeys, peer_u, peer_v):
    bp = x_prompt.shape[0]
    hp, hs = x_prompt, x_sample
    pp, pk, pv, psh, pw = [], [], [], [], []
    sp, sk, sv, ssh, sw = [], [], [], [], []
    for l in range(DEPTH):
        lp = {
            'norm1_g': norm1_g[l], 'norm2_g': norm2_g[l], 'w_in': w_in[l], 'w_out': w_out[l],
            'pool_w': pool_w[l], 'pool_scale': pool_scale[l],
            'rwkv_mu': rwkv_mu[l], 'rwkv_w0': rwkv_w0[l], 'rwkv_w_up': rwkv_w_up[l],
            'rwkv_a0': rwkv_a0[l], 'rwkv_a_up': rwkv_a_up[l], 'rwkv_g_up': rwkv_g_up[l],
            'rwkv_k_k': rwkv_k_k[l], 'rwkv_k_a': rwkv_k_a[l], 'rwkv_r_k': rwkv_r_k[l],
            'rwkv_ln_w': rwkv_ln_w[l], 'rwkv_ln_b': rwkv_ln_b[l],
            'peer_wq': peer_wq[l], 'peer_subkeys': peer_subkeys[l], 'peer_u': peer_u[l], 'peer_v': peer_v[l],
        }
        hp, a0_, a1_, a2_, a3_, a4_ = trunk_layer(
            hp, lp, 0,
            jnp.zeros((bp, POOL_BUF, POOL_WIDTH), x_prompt.dtype), None, None,
            jnp.zeros((bp, RWKV_PROJ), x_prompt.dtype),
            jnp.zeros((bp, RWKV_HEADS, HEAD_DIM, HEAD_DIM), x_prompt.dtype))
        pp.append(a0_); pk.append(a1_); pv.append(a2_); psh.append(a3_); pw.append(a4_)
        hs, b0_, b1_, b2_, b3_, b4_ = trunk_layer(
            hs, lp, PAST_LEN, state_pool[l], cache_k[l], cache_v[l], state_shift[l], state_wkv[l])
        sp.append(b0_); sk.append(b1_); sv.append(b2_); ssh.append(b3_); sw.append(b4_)
    y_prompt = rms_norm(hp, final_g)
    y_sample = rms_norm(hs, final_g)
    return (y_prompt, y_sample,
            jnp.stack(pp), jnp.stack(pk), jnp.stack(pv), jnp.stack(psh), jnp.stack(pw),
            jnp.stack(sp), jnp.stack(sk), jnp.stack(sv), jnp.stack(ssh), jnp.stack(sw))
```

```python
import functools
import math

import jax
import jax.numpy as jnp
from jax import lax
from jax.experimental import pallas as pl
from jax.experimental.pallas import tpu as pltpu

F32 = jnp.float32
BF16 = jnp.bfloat16

D_MODEL = 1024
HEAD_DIM = 64
POOL_WINDOWS = (2, 4, 8, 16)
POOL_GROUP = 64
POOL_WIDTH = 256
POOL_BUF = 15
ATT_WIDTH = 384
ATT_HEADS = 6
DILATED_GROUPS = ((128, 1), (512, 4), (2048, 16))
ATT_BUF = 2048
ROPE_DIM = 16
ROPE_THETA = 500000.0
NEG_INF = -1e30
RWKV_WIDTH = 384
RWKV_HEADS = 6
DECAY_LORA = 64
ICLR_LORA = 64
GATE_LORA = 128
RWKV_PROJ = 1408
GN_EPS = HEAD_DIM * 1e-5
N_KEYS = 128
N_EXPERTS = N_KEYS * N_KEYS
PEER_HEADS = 8
PEER_TOPK = 16
RMS_EPS = 1e-6

TOKEN_BLOCK = 256
EXPERT_CHUNK = 1024
G_PITCH = 136
VMEM_LIMIT = 56 * 1024 * 1024


def _cparams(sem):
    return pltpu.CompilerParams(dimension_semantics=sem, vmem_limit_bytes=VMEM_LIMIT)


def _linear_kernel(*refs, norm, resid, passes, emit_xn):
    it = iter(refs)
    x_ref = next(it)
    g_ref = next(it) if norm else None
    w_ref = next(it)
    wlo_ref = next(it) if passes == 3 else None
    r_ref = next(it) if resid else None
    o_ref = next(it)
    xn_ref = next(it) if emit_xn else None

    x = x_ref[...].astype(F32)
    if norm:
        x = x * lax.rsqrt(jnp.mean(x * x, axis=-1, keepdims=True) + RMS_EPS) * g_ref[...]
    if emit_xn:
        xn_ref[...] = x.astype(xn_ref.dtype)
    x_hi = x.astype(BF16)
    acc = jnp.dot(x_hi, w_ref[...], preferred_element_type=F32)
    if passes == 3:
        x_lo = (x - x_hi.astype(F32)).astype(BF16)
        acc = acc + jnp.dot(x_lo, w_ref[...], preferred_element_type=F32)
        acc = acc + jnp.dot(x_hi, wlo_ref[...], preferred_element_type=F32)
    if resid:
        acc = acc + r_ref[...]
    o_ref[...] = acc


def _linear(x, w_hi, *, g=None, w_lo=None, resid=None, emit_xn=False, tb=TOKEN_BLOCK, name="linear"):
    t, k = x.shape
    n = w_hi.shape[1]
    assert t % tb == 0
    norm = g is not None
    passes = 3 if w_lo is not None else 1
    ins = [x]
    specs = [pl.BlockSpec((tb, k), lambda i: (i, 0))]
    if norm:
        ins.append(g.reshape(1, k).astype(F32))
        specs.append(pl.BlockSpec((1, k), lambda i: (0, 0)))
    ins.append(w_hi)
    specs.append(pl.BlockSpec((k, n), lambda i: (0, 0)))
    if passes == 3:
        ins.append(w_lo)
        specs.append(pl.BlockSpec((k, n), lambda i: (0, 0)))
    if resid is not None:
        ins.append(resid)
        specs.append(pl.BlockSpec((tb, n), lambda i: (i, 0)))
    out_shape = [jax.ShapeDtypeStruct((t, n), F32)]
    out_specs = [pl.BlockSpec((tb, n), lambda i: (i, 0))]
    if emit_xn:
        out_shape.append(jax.ShapeDtypeStruct((t, k), BF16))
        out_specs.append(pl.BlockSpec((tb, k), lambda i: (i, 0)))
    outs = pl.pallas_call(
        functools.partial(_linear_kernel, norm=norm, resid=resid is not None, passes=passes, emit_xn=emit_xn),
        grid=(t // tb,),
        in_specs=specs,
        out_specs=out_specs,
        out_shape=out_shape,
        compiler_params=_cparams(("parallel",)),
        name=name,
    )(*ins)
    return outs if emit_xn else outs[0]


def _split_bf16(w):
    hi = w.astype(BF16)
    lo = (w - hi.astype(F32)).astype(BF16)
    return hi, lo


def _rmsnorm_kernel(x_ref, g_ref, o_ref):
    x = x_ref[...]
    o_ref[...] = x * lax.rsqrt(jnp.mean(x * x, axis=-1, keepdims=True) + RMS_EPS) * g_ref[...]


def _rmsnorm(x, g, tb=TOKEN_BLOCK):
    t, d = x.shape
    return pl.pallas_call(
        _rmsnorm_kernel,
        grid=(t // tb,),
        in_specs=[pl.BlockSpec((tb, d), lambda i: (i, 0)), pl.BlockSpec((1, d), lambda i: (0, 0))],
        out_specs=pl.BlockSpec((tb, d), lambda i: (i, 0)),
        out_shape=jax.ShapeDtypeStruct((t, d), F32),
        compiler_params=_cparams(("parallel",)),
        name="final_norm",
    )(x, g.reshape(1, d))


def _dot_nt(a, b):
    return lax.dot_general(a, b, (((1,), (1,)), ((), ())), preferred_element_type=F32)


def _route_kernel(q_ref, sk_ref, i1_ref, i2_ref, gate_ref, tv_ref, ti_ref, e1_ref, e2_ref, gt_ref):
    tb = q_ref.shape[0]
    neg = jnp.float32(-jnp.inf)
    key_iota = lax.broadcasted_iota(jnp.int32, (N_KEYS, tb), 0).astype(F32)

    def half_topk(hc, carry):
        off = pl.multiple_of(hc * N_KEYS, N_KEYS)
        qh = q_ref[:, pl.ds(off, N_KEYS)]
        sk = sk_ref[hc]
        q_hi = qh.astype(BF16)
        q_lo = (qh - q_hi.astype(F32)).astype(BF16)
        s_hi = sk.astype(BF16)
        s_lo = (sk - s_hi.astype(F32)).astype(BF16)
        sc = _dot_nt(s_hi, q_hi) + _dot_nt(s_lo, q_hi) + _dot_nt(s_hi, q_lo)
        vals, idxs = [], []
        for _ in range(PEER_TOPK):
            m = jnp.max(sc, axis=0, keepdims=True)
            idx = jnp.min(jnp.where(sc == m, key_iota, float(N_KEYS)), axis=0, keepdims=True)
            sc = jnp.where(key_iota == idx, neg, sc)
            vals.append(m)
            idxs.append(idx)
        tv_ref[hc] = jnp.concatenate(vals, axis=0)
        ti_ref[hc] = jnp.concatenate(idxs, axis=0)
        return carry

    lax.fori_loop(0, 2 * PEER_HEADS, half_topk, 0)

    n_rows = 16 + 8 * 8
    row8 = lax.broadcasted_iota(jnp.int32, (8, tb), 0)
    cand_iota = lax.broadcasted_iota(jnp.int32, (n_rows, tb), 0).astype(F32)

    def pair_topk(h, carry):
        a = tv_ref[2 * h]
        b = tv_ref[2 * h + 1]
        ia = ti_ref[2 * h]
        ib = ti_ref[2 * h + 1]
        vs = [a[0:1] + b]
        p1 = [jnp.broadcast_to(ia[0:1], (16, tb))]
        p2 = [ib]
        for i in range(1, 8):
            lim = PEER_TOPK // (i + 1)
            v = a[i:i + 1] + b[0:8]
            vs.append(jnp.where(row8 < lim, v, neg) if lim < 8 else v)
            p1.append(jnp.broadcast_to(ia[i:i + 1], (8, tb)))
            p2.append(ib[0:8])
        vs.append(a[8:16] + b[0:1])
        p1.append(ia[8:16])
        p2.append(jnp.broadcast_to(ib[0:1], (8, tb)))
        cand = jnp.concatenate(vs, axis=0)
        c1 = jnp.concatenate(p1, axis=0)
        c2 = jnp.concatenate(p2, axis=0)
        tops, s1, s2 = [], [], []
        for _ in range(PEER_TOPK):
            m = jnp.max(cand, axis=0, keepdims=True)
            pos = jnp.min(jnp.where(cand == m, cand_iota, float(n_rows)), axis=0, keepdims=True)
            hit = cand_iota == pos
            s1.append(jnp.sum(jnp.where(hit, c1, 0.0), axis=0, keepdims=True))
            s2.append(jnp.sum(jnp.where(hit, c2, 0.0), axis=0, keepdims=True))
            cand = jnp.where(hit, neg, cand)
            tops.append(m)
        s = jnp.concatenate(tops, axis=0)
        e = jnp.exp(s - s[0:1])
        gate = e / jnp.sum(e, axis=0, keepdims=True)
        row = pl.multiple_of(h * PEER_TOPK, PEER_TOPK)
        gt_ref[pl.ds(row, PEER_TOPK), :] = gate
        e1_ref[pl.ds(row, PEER_TOPK), :] = jnp.concatenate(s1, axis=0)
        e2_ref[pl.ds(row, PEER_TOPK), :] = jnp.concatenate(s2, axis=0)
        return carry

    lax.fori_loop(0, PEER_HEADS, pair_topk, 0)
    i1_ref[...] = e1_ref[...].T
    i2_ref[...] = e2_ref[...].T
    gate_ref[...] = gt_ref[...].T


def _route(q, subkeys, tb=TOKEN_BLOCK):
    t = q.shape[0]
    nsel = PEER_HEADS * PEER_TOPK
    sk = subkeys.reshape(2 * PEER_HEADS, N_KEYS, N_KEYS)
    out = jax.ShapeDtypeStruct((t, nsel), F32)
    spec = pl.BlockSpec((tb, nsel), lambda i: (i, 0))
    return pl.pallas_call(
        _route_kernel,
        grid=(t // tb,),
        in_specs=[pl.BlockSpec((tb, q.shape[1]), lambda i: (i, 0)),
                  pl.BlockSpec(sk.shape, lambda i: (0, 0, 0))],
        out_specs=[spec, spec, spec],
        out_shape=[out, out, out],
        scratch_shapes=[pltpu.VMEM((2 * PEER_HEADS, PEER_TOPK, tb), F32),
                        pltpu.VMEM((2 * PEER_HEADS, PEER_TOPK, tb), F32),
                        pltpu.VMEM((nsel, tb), F32),
                        pltpu.VMEM((nsel, tb), F32),
                        pltpu.VMEM((nsel, tb), F32)],
        compiler_params=_cparams(("parallel",)),
        name="peer_route",
    )(q, sk)


_ERF_P = 0.3275911
_ERF_A = (0.254829592, -0.284496736, 1.421413741, -1.453152027, 1.061405429)


def _gelu_exact(x):
    return 0.5 * x * (1.0 + lax.erf(x * (1.0 / math.sqrt(2.0))))


def _peer_kernel(xn_ref, i1_ref, i2_ref, gate_ref, h_ref, ut_ref, v_ref, o_ref, gs_ref, w_ref, acc_ref):
    c = pl.program_id(1)
    tb = xn_ref.shape[0]
    ec = ut_ref.shape[1]
    a_per_chunk = ec // N_KEYS

    @pl.when(c == 0)
    def _build_gates():
        key_iota = lax.broadcasted_iota(jnp.int32, (N_KEYS, N_KEYS), 0).astype(F32)

        def one_token(t, carry):
            r1 = i1_ref[pl.ds(t, 1), :]
            r2 = i2_ref[pl.ds(t, 1), :]
            gr = gate_ref[pl.ds(t, 1), :]
            a_t = jnp.where(key_iota == r1, gr, 0.0).astype(BF16)
            b_t = jnp.where(key_iota == r2, 1.0, 0.0).astype(BF16)
            row = pl.multiple_of(t * G_PITCH, 8)
            gs_ref[pl.ds(row, N_KEYS), :] = _dot_nt(a_t, b_t)
            return carry

        lax.fori_loop(0, tb, one_token, 0)
        acc_ref[...] = jnp.zeros_like(acc_ref)

    hid = jnp.dot(xn_ref[...], ut_ref[...], preferred_element_type=F32)
    for al in range(a_per_chunk):
        g_a = gs_ref[pl.ds(c * a_per_chunk + al, tb, stride=G_PITCH), :]
        act = _gelu_exact(hid[:, al * N_KEYS:(al + 1) * N_KEYS])
        w_ref[:, al * N_KEYS:(al + 1) * N_KEYS] = (g_a * act).astype(BF16)
    acc_ref[...] += jnp.dot(w_ref[...], v_ref[...], preferred_element_type=F32)

    @pl.when(c == pl.num_programs(1) - 1)
    def _finish():
        o_ref[...] = h_ref[...] + acc_ref[...]


def _peer_experts(xn, i1, i2, gate, h, ut, v, tb=TOKEN_BLOCK, ec=EXPERT_CHUNK):
    t, d = h.shape
    n_exp = v.shape[0]
    nsel = i1.shape[1]
    tok = lambda i, c: (i, 0)
    return pl.pallas_call(
        _peer_kernel,
        grid=(t // tb, n_exp // ec),
        in_specs=[pl.BlockSpec((tb, d), tok),
                  pl.BlockSpec((tb, nsel), tok),
                  pl.BlockSpec((tb, nsel), tok),
                  pl.BlockSpec((tb, nsel), tok),
                  pl.BlockSpec((tb, d), tok),
                  pl.BlockSpec((d, ec), lambda i, c: (0, c)),
                  pl.BlockSpec((ec, d), lambda i, c: (c, 0))],
        out_specs=pl.BlockSpec((tb, d), tok),
        out_shape=jax.ShapeDtypeStruct((t, d), F32),
        scratch_shapes=[pltpu.VMEM((tb * G_PITCH, N_KEYS), F32),
                        pltpu.VMEM((tb, ec), BF16),
                        pltpu.VMEM((tb, d), F32)],
        compiler_params=_cparams(("parallel", "arbitrary")),
        name="peer_experts",
    )(xn, i1, i2, gate, h, ut, v)


def _peer_ffn(h, g2, wq, subkeys, u, v):
    wq_hi, wq_lo = _split_bf16(wq)
    q, xn = _linear(h, wq_hi, g=g2, w_lo=wq_lo, emit_xn=True, name="peer_query")
    i1, i2, gate = _route(q, subkeys)
    ut = u.astype(BF16).T
    return _peer_experts(xn, i1, i2, gate, h, ut, v.astype(BF16))


def _rope(x, pos):
    half = ROPE_DIM // 2
    inv = ROPE_THETA ** (-jnp.arange(half, dtype=F32) * 2.0 / ROPE_DIM)
    ang = pos.astype(F32)[:, None] * inv[None, :]
    cos = jnp.cos(ang)[None, :, None, :]
    sin = jnp.sin(ang)[None, :, None, :]
    xr = x[..., :ROPE_DIM]
    x1, x2 = xr[..., :half], xr[..., half:]
    rot = jnp.concatenate([x1 * cos - x2 * sin, x2 * cos + x1 * sin], axis=-1)
    return jnp.concatenate([rot, x[..., ROPE_DIM:]], axis=-1)


def _pool_mixer(x_new, buf, pos0, w, scale):
    B, L, _ = x_new.shape
    G = len(POOL_WINDOWS)
    xfull = jnp.concatenate([buf, x_new], axis=1)
    xg = xfull.reshape(B, POOL_BUF + L, G, POOL_GROUP)
    csum = jnp.concatenate([jnp.zeros((B, 1, G, POOL_GROUP), F32), jnp.cumsum(xg, axis=1)], axis=1)
    win = jnp.array(POOL_WINDOWS, jnp.int32)
    t = jnp.arange(L, dtype=jnp.int32)
    hi = t + POOL_BUF + 1
    lo = hi[:, None] - win[None, :]
    gi = jnp.arange(G)[None, :]
    sums = csum[:, hi] - csum[:, lo, gi]
    cnt = jnp.minimum(win[None, :], pos0 + t[:, None] + 1).astype(F32)
    diff = sums / cnt[None, :, :, None] - xg[:, POOL_BUF:]
    y = jnp.einsum('blgc,gcd->blgd', diff, w).reshape(B, L, POOL_WIDTH)
    return y * scale, xfull[:, -POOL_BUF:]


def _combine_branches(outs, lses):
    wts = jax.nn.softmax(jnp.stack(lses, 0), axis=0)
    return jnp.sum(wts[..., None] * jnp.stack(outs, 0), axis=0)


def _dilated_attn_prompt(q, k, v):
    B, S, H, Dh = q.shape
    scale = HEAD_DIM ** -0.5
    outs, lses = [], []
    for window, dil in DILATED_GROUPS:
        nb = window // dil
        blk = nb
        span = dil * blk
        s_pad = -(-S // span) * span
        M = s_pad // dil
        nblk = M // blk

        def to_res(a):
            a = jnp.pad(a, ((0, 0), (0, s_pad - S), (0, 0), (0, 0)))
            a = a.reshape(B, M, dil, H, Dh).transpose(0, 2, 1, 3, 4)
            return a.reshape(B, dil, nblk, blk, H, Dh)

        def with_prev(a):
            prev = jnp.pad(a, ((0, 0), (0, 0), (1, 0), (0, 0), (0, 0), (0, 0)))[:, :, :-1]
            return jnp.concatenate([prev, a], axis=3)

        qr = to_res(q)
        kb = with_prev(to_res(k))
        vb = with_prev(to_res(v))
        s = jnp.einsum('brnqhd,brnkhd->brnhqk', qr, kb, preferred_element_type=F32) * scale
        iq = jnp.arange(blk)[:, None]
        ik = jnp.arange(2 * blk)[None, :]
        rel = blk + iq - ik
        mk = jnp.arange(nblk)[:, None, None] * blk + ik[None] - blk
        valid = (rel >= 0)[None] & (rel <= nb)[None] & (mk >= 0)
        s = jnp.where(valid[None, None, :, None], s, NEG_INF)
        m = jnp.max(s, axis=-1, keepdims=True)
        p = jnp.exp(s - m)
        den = jnp.sum(p, axis=-1, keepdims=True)
        o = jnp.einsum('brnhqk,brnkhd->brnqhd', p / den, vb)
        lse = (m + jnp.log(den))[..., 0]
        o = o.reshape(B, dil, M, H, Dh).transpose(0, 2, 1, 3, 4).reshape(B, s_pad, H, Dh)[:, :S]
        lse = lse.transpose(0, 1, 2, 4, 3).reshape(B, dil, M, H).transpose(0, 2, 1, 3).reshape(B, s_pad, H)[:, :S]
        outs.append(o)
        lses.append(lse)
    return _combine_branches(outs, lses)


def _dilated_attn_sample(q, k_all, v_all, n_past):
    B, T, H, Dh = q.shape
    scale = HEAD_DIM ** -0.5
    outs, lses = [], []
    for window, dil in DILATED_GROUPS:
        nb = window // dil
        idx = n_past + jnp.arange(T)[:, None] - dil * jnp.arange(nb + 1)[None, :]
        valid = idx >= 0
        idc = jnp.maximum(idx, 0)
        kg = k_all[:, idc]
        vg = v_all[:, idc]
        s = jnp.einsum('bthd,btkhd->bthk', q, kg, preferred_element_type=F32) * scale
        s = jnp.where(valid[None, :, None, :], s, NEG_INF)
        m = jnp.max(s, axis=-1, keepdims=True)
        p = jnp.exp(s - m)
        den = jnp.sum(p, axis=-1, keepdims=True)
        o = jnp.einsum('bthk,btkhd->bthd', p / den, vg)
        outs.append(o)
        lses.append((m + jnp.log(den))[..., 0])
    return _combine_branches(outs, lses)


def _rwkv_mixer(p, prev, wkv0, lp):
    B, L, _ = p.shape
    p_prev = jnp.concatenate([prev[:, None, :], p[:, :-1]], axis=1)
    ps = p + (p_prev - p) * lp['rwkv_mu']
    c1, c2, c3 = RWKV_WIDTH, 2 * RWKV_WIDTH, 3 * RWKV_WIDTH
    c4 = c3 + DECAY_LORA
    c5 = c4 + ICLR_LORA
    r, k, v = ps[..., :c1], ps[..., c1:c2], ps[..., c2:c3]
    wd, ad, gd = ps[..., c3:c4], ps[..., c4:c5], ps[..., c5:]
    w_log = -jax.nn.softplus(-(lp['rwkv_w0'] + jnp.tanh(wd) @ lp['rwkv_w_up'])) - 0.5
    decay = jnp.exp(-jnp.exp(w_log))
    a = jax.nn.sigmoid(lp['rwkv_a0'] + ad @ lp['rwkv_a_up'])
    g = jax.nn.sigmoid(gd) @ lp['rwkv_g_up']
    hd = lambda t: t.reshape(B, L, RWKV_HEADS, HEAD_DIM)
    kk = hd(k * lp['rwkv_k_k'])
    kk = kk / jnp.maximum(jnp.sqrt(jnp.sum(kk * kk, axis=-1, keepdims=True)), 1e-12)
    k = k * (1.0 + (a - 1.0) * lp['rwkv_k_a'])
    r_h, k_h, v_h, w_h, a_h = hd(r), hd(k), hd(v), hd(decay), hd(a)

    def step(S, inp):
        r_t, w_t, k_t, v_t, kk_t, a_t = inp
        sa = jnp.einsum('bhij,bhj->bhi', S, -kk_t)
        S = (S * w_t[:, :, None, :] + sa[..., None] * (kk_t * a_t)[:, :, None, :]
             + v_t[..., None] * k_t[:, :, None, :])
        return S, jnp.einsum('bhij,bhj->bhi', S, r_t)

    xs = tuple(jnp.moveaxis(t, 1, 0) for t in (r_h, w_h, k_h, v_h, kk, a_h))
    S_T, o = lax.scan(step, wkv0, xs)
    o = jnp.moveaxis(o, 0, 1)
    mean = jnp.mean(o, axis=-1, keepdims=True)
    var = jnp.mean(jnp.square(o - mean), axis=-1, keepdims=True)
    o = ((o - mean) * lax.rsqrt(var + GN_EPS)).reshape(B, L, RWKV_WIDTH)
    o = o * lp['rwkv_ln_w'] + lp['rwkv_ln_b']
    bonus = jnp.sum(r_h * k_h * lp['rwkv_r_k'], axis=-1, keepdims=True) * v_h
    o = (o + bonus.reshape(B, L, RWKV_WIDTH)) * g
    return o, p[:, -1], S_T


def _mixers(proj, lp, pos0, pool_buf, k_buf, v_buf, shift_prev, wkv0):
    B, L, _ = proj.shape
    o1 = POOL_WIDTH
    o2 = o1 + ATT_WIDTH
    o3 = o2 + ATT_WIDTH
    o4 = o3 + ATT_WIDTH
    heads = lambda t: t.reshape(B, L, ATT_HEADS, HEAD_DIM)
    pos = pos0 + jnp.arange(L, dtype=jnp.int32)
    q = _rope(heads(proj[..., o1:o2]), pos)
    k = _rope(heads(proj[..., o2:o3]), pos)
    v = heads(proj[..., o3:o4])
    pool_out, pool_new = _pool_mixer(proj[..., :o1], pool_buf, pos0, lp['pool_w'], lp['pool_scale'])
    if k_buf is None:
        att = _dilated_attn_prompt(q, k, v)
        keep = min(ATT_BUF, L)
        k_new, v_new = k[:, -keep:], v[:, -keep:]
    else:
        n_past = k_buf.shape[1]
        k_all = jnp.concatenate([k_buf, k], axis=1)
        v_all = jnp.concatenate([v_buf, v], axis=1)
        att = _dilated_attn_sample(q, k_all, v_all, n_past)
        k_new, v_new = k_all[:, -n_past:], v_all[:, -n_past:]
    rw_out, shift_new, wkv_new = _rwkv_mixer(proj[..., o4:], shift_prev, wkv0, lp)
    mixed = jnp.concatenate([pool_out, att.reshape(B, L, ATT_WIDTH), rw_out], axis=-1)
    return mixed, pool_new, k_new, v_new, shift_new, wkv_new


def kernel(x_prompt, x_sample, state_pool, cache_k, cache_v, state_shift, state_wkv, norm1_g, norm2_g, final_g, w_in, w_out, pool_w, pool_scale, rwkv_mu, rwkv_w0, rwkv_w_up, rwkv_a0, rwkv_a_up, rwkv_g_up, rwkv_k_k, rwkv_k_a, rwkv_r_k, rwkv_ln_w, rwkv_ln_b, peer_wq, peer_subkeys, peer_u, peer_v):
    bp, sp_len, d = x_prompt.shape
    bs, ss_len, _ = x_sample.shape
    tp = bp * sp_len
    depth = w_in.shape[0]
    past_len = 8192
    h = jnp.concatenate([x_prompt.reshape(tp, d), x_sample.reshape(bs * ss_len, d)], axis=0)
    outs_p = [[] for _ in range(5)]
    outs_s = [[] for _ in range(5)]
    for l in range(depth):
        lp = {'pool_w': pool_w[l], 'pool_scale': pool_scale[l], 'rwkv_mu': rwkv_mu[l], 'rwkv_w0': rwkv_w0[l],
              'rwkv_w_up': rwkv_w_up[l], 'rwkv_a0': rwkv_a0[l], 'rwkv_a_up': rwkv_a_up[l],
              'rwkv_g_up': rwkv_g_up[l], 'rwkv_k_k': rwkv_k_k[l], 'rwkv_k_a': rwkv_k_a[l],
              'rwkv_r_k': rwkv_r_k[l], 'rwkv_ln_w': rwkv_ln_w[l], 'rwkv_ln_b': rwkv_ln_b[l]}
        proj = _linear(h, w_in[l].astype(BF16), g=norm1_g[l], name="in_proj")
        mixed_p, *st_p = _mixers(
            proj[:tp].reshape(bp, sp_len, -1), lp, 0,
            jnp.zeros((bp, POOL_BUF, POOL_WIDTH), F32), None, None,
            jnp.zeros((bp, RWKV_PROJ), F32), jnp.zeros((bp, RWKV_HEADS, HEAD_DIM, HEAD_DIM), F32))
        mixed_s, *st_s = _mixers(
            proj[tp:].reshape(bs, ss_len, -1), lp, past_len,
            state_pool[l], cache_k[l], cache_v[l], state_shift[l], state_wkv[l])
        for acc, val in zip(outs_p, st_p):
            acc.append(val)
        for acc, val in zip(outs_s, st_s):
            acc.append(val)
        mixed = jnp.concatenate([mixed_p.reshape(tp, -1), mixed_s.reshape(bs * ss_len, -1)], axis=0)
        h = _linear(mixed, w_out[l].astype(BF16), resid=h, name="out_proj")
        h = _peer_ffn(h, norm2_g[l], peer_wq[l], peer_subkeys[l], peer_u[l], peer_v[l])
    y = _rmsnorm(h, final_g)
    y_prompt = y[:tp].reshape(bp, sp_len, d)
    y_sample = y[tp:].reshape(bs, ss_len, d)
    return (y_prompt, y_sample, *[jnp.stack(a) for a in outs_p], *[jnp.stack(a) for a in outs_s])
```

```python
import functools
import math

import jax
import jax.numpy as jnp
from jax import lax
from jax.experimental import pallas as pl
from jax.experimental.pallas import tpu as pltpu

F32 = jnp.float32
BF16 = jnp.bfloat16

D_MODEL = 1024
HEAD_DIM = 64
POOL_WINDOWS = (2, 4, 8, 16)
POOL_GROUP = 64
POOL_WIDTH = 256
POOL_BUF = 15
ATT_WIDTH = 384
ATT_HEADS = 6
DILATED_GROUPS = ((128, 1), (512, 4), (2048, 16))
ATT_BUF = 2048
ROPE_DIM = 16
ROPE_THETA = 500000.0
NEG_INF = -1e30
RWKV_WIDTH = 384
RWKV_HEADS = 6
DECAY_LORA = 64
ICLR_LORA = 64
GATE_LORA = 128
RWKV_PROJ = 1408
GN_EPS = HEAD_DIM * 1e-5
N_KEYS = 128
N_EXPERTS = N_KEYS * N_KEYS
PEER_HEADS = 8
PEER_TOPK = 16
RMS_EPS = 1e-6

TOKEN_BLOCK = 256
EXPERT_CHUNK = 1024
G_PITCH = 136
RWKV_CHUNK = 64
VMEM_LIMIT = 56 * 1024 * 1024


def _cparams(sem):
    return pltpu.CompilerParams(dimension_semantics=sem, vmem_limit_bytes=VMEM_LIMIT)


def _linear_kernel(*refs, norm, resid, passes, emit_xn):
    it = iter(refs)
    x_ref = next(it)
    g_ref = next(it) if norm else None
    w_ref = next(it)
    wlo_ref = next(it) if passes == 3 else None
    r_ref = next(it) if resid else None
    o_ref = next(it)
    xn_ref = next(it) if emit_xn else None

    x = x_ref[...].astype(F32)
    if norm:
        x = x * lax.rsqrt(jnp.mean(x * x, axis=-1, keepdims=True) + RMS_EPS) * g_ref[...]
    if emit_xn:
        xn_ref[...] = x.astype(xn_ref.dtype)
    x_hi = x.astype(BF16)
    acc = jnp.dot(x_hi, w_ref[...], preferred_element_type=F32)
    if passes == 3:
        x_lo = (x - x_hi.astype(F32)).astype(BF16)
        acc = acc + jnp.dot(x_lo, w_ref[...], preferred_element_type=F32)
        acc = acc + jnp.dot(x_hi, wlo_ref[...], preferred_element_type=F32)
    if resid:
        acc = acc + r_ref[...]
    o_ref[...] = acc


def _linear(x, w_hi, *, g=None, w_lo=None, resid=None, emit_xn=False, tb=TOKEN_BLOCK, name="linear"):
    t, k = x.shape
    n = w_hi.shape[1]
    assert t % tb == 0
    norm = g is not None
    passes = 3 if w_lo is not None else 1
    ins = [x]
    specs = [pl.BlockSpec((tb, k), lambda i: (i, 0))]
    if norm:
        ins.append(g.reshape(1, k).astype(F32))
        specs.append(pl.BlockSpec((1, k), lambda i: (0, 0)))
    ins.append(w_hi)
    specs.append(pl.BlockSpec((k, n), lambda i: (0, 0)))
    if passes == 3:
        ins.append(w_lo)
        specs.append(pl.BlockSpec((k, n), lambda i: (0, 0)))
    if resid is not None:
        ins.append(resid)
        specs.append(pl.BlockSpec((tb, n), lambda i: (i, 0)))
    out_shape = [jax.ShapeDtypeStruct((t, n), F32)]
    out_specs = [pl.BlockSpec((tb, n), lambda i: (i, 0))]
    if emit_xn:
        out_shape.append(jax.ShapeDtypeStruct((t, k), BF16))
        out_specs.append(pl.BlockSpec((tb, k), lambda i: (i, 0)))
    outs = pl.pallas_call(
        functools.partial(_linear_kernel, norm=norm, resid=resid is not None, passes=passes, emit_xn=emit_xn),
        grid=(t // tb,),
        in_specs=specs,
        out_specs=out_specs,
        out_shape=out_shape,
        compiler_params=_cparams(("parallel",)),
        name=name,
    )(*ins)
    return outs if emit_xn else outs[0]


def _split_bf16(w):
    hi = w.astype(BF16)
    lo = (w - hi.astype(F32)).astype(BF16)
    return hi, lo


def _rmsnorm_kernel(x_ref, g_ref, o_ref):
    x = x_ref[...]
    o_ref[...] = x * lax.rsqrt(jnp.mean(x * x, axis=-1, keepdims=True) + RMS_EPS) * g_ref[...]


def _rmsnorm(x, g, tb=TOKEN_BLOCK):
    t, d = x.shape
    return pl.pallas_call(
        _rmsnorm_kernel,
        grid=(t // tb,),
        in_specs=[pl.BlockSpec((tb, d), lambda i: (i, 0)), pl.BlockSpec((1, d), lambda i: (0, 0))],
        out_specs=pl.BlockSpec((tb, d), lambda i: (i, 0)),
        out_shape=jax.ShapeDtypeStruct((t, d), F32),
        compiler_params=_cparams(("parallel",)),
        name="final_norm",
    )(x, g.reshape(1, d))


def _dot_nt(a, b):
    return lax.dot_general(a, b, (((1,), (1,)), ((), ())), preferred_element_type=F32)


def _route_kernel(q_ref, sk_ref, i1_ref, i2_ref, gate_ref, tv_ref, ti_ref, e1_ref, e2_ref, gt_ref):
    tb = q_ref.shape[0]
    neg = jnp.float32(-jnp.inf)
    key_iota = lax.broadcasted_iota(jnp.int32, (N_KEYS, tb), 0).astype(F32)

    def half_topk(hc, carry):
        off = pl.multiple_of(hc * N_KEYS, N_KEYS)
        qh = q_ref[:, pl.ds(off, N_KEYS)]
        sk = sk_ref[hc]
        q_hi = qh.astype(BF16)
        q_lo = (qh - q_hi.astype(F32)).astype(BF16)
        s_hi = sk.astype(BF16)
        s_lo = (sk - s_hi.astype(F32)).astype(BF16)
        sc = _dot_nt(s_hi, q_hi) + _dot_nt(s_lo, q_hi) + _dot_nt(s_hi, q_lo)
        vals, idxs = [], []
        for _ in range(PEER_TOPK):
            m = jnp.max(sc, axis=0, keepdims=True)
            idx = jnp.min(jnp.where(sc == m, key_iota, float(N_KEYS)), axis=0, keepdims=True)
            sc = jnp.where(key_iota == idx, neg, sc)
            vals.append(m)
            idxs.append(idx)
        tv_ref[hc] = jnp.concatenate(vals, axis=0)
        ti_ref[hc] = jnp.concatenate(idxs, axis=0)
        return carry

    lax.fori_loop(0, 2 * PEER_HEADS, half_topk, 0)

    n_rows = 16 + 8 * 8
    row8 = lax.broadcasted_iota(jnp.int32, (8, tb), 0)
    cand_iota = lax.broadcasted_iota(jnp.int32, (n_rows, tb), 0).astype(F32)

    def pair_topk(h, carry):
        a = tv_ref[2 * h]
        b = tv_ref[2 * h + 1]
        ia = ti_ref[2 * h]
        ib = ti_ref[2 * h + 1]
        vs = [a[0:1] + b]
        p1 = [jnp.broadcast_to(ia[0:1], (16, tb))]
        p2 = [ib]
        for i in range(1, 8):
            lim = PEER_TOPK // (i + 1)
            v = a[i:i + 1] + b[0:8]
            vs.append(jnp.where(row8 < lim, v, neg) if lim < 8 else v)
            p1.append(jnp.broadcast_to(ia[i:i + 1], (8, tb)))
            p2.append(ib[0:8])
        vs.append(a[8:16] + b[0:1])
        p1.append(ia[8:16])
        p2.append(jnp.broadcast_to(ib[0:1], (8, tb)))
        cand = jnp.concatenate(vs, axis=0)
        c1 = jnp.concatenate(p1, axis=0)
        c2 = jnp.concatenate(p2, axis=0)
        tops, s1, s2 = [], [], []
        for _ in range(PEER_TOPK):
            m = jnp.max(cand, axis=0, keepdims=True)
            pos = jnp.min(jnp.where(cand == m, cand_iota, float(n_rows)), axis=0, keepdims=True)
            hit = cand_iota == pos
            s1.append(jnp.sum(jnp.where(hit, c1, 0.0), axis=0, keepdims=True))
            s2.append(jnp.sum(jnp.where(hit, c2, 0.0), axis=0, keepdims=True))
            cand = jnp.where(hit, neg, cand)
            tops.append(m)
        s = jnp.concatenate(tops, axis=0)
        e = jnp.exp(s - s[0:1])
        gate = e / jnp.sum(e, axis=0, keepdims=True)
        row = pl.multiple_of(h * PEER_TOPK, PEER_TOPK)
        gt_ref[pl.ds(row, PEER_TOPK), :] = gate
        e1_ref[pl.ds(row, PEER_TOPK), :] = jnp.concatenate(s1, axis=0)
        e2_ref[pl.ds(row, PEER_TOPK), :] = jnp.concatenate(s2, axis=0)
        return carry

    lax.fori_loop(0, PEER_HEADS, pair_topk, 0)
    i1_ref[...] = e1_ref[...].T
    i2_ref[...] = e2_ref[...].T
    gate_ref[...] = gt_ref[...].T


def _route(q, subkeys, tb=TOKEN_BLOCK):
    t = q.shape[0]
    nsel = PEER_HEADS * PEER_TOPK
    sk = subkeys.reshape(2 * PEER_HEADS, N_KEYS, N_KEYS)
    out = jax.ShapeDtypeStruct((t, nsel), F32)
    spec = pl.BlockSpec((tb, nsel), lambda i: (i, 0))
    return pl.pallas_call(
        _route_kernel,
        grid=(t // tb,),
        in_specs=[pl.BlockSpec((tb, q.shape[1]), lambda i: (i, 0)),
                  pl.BlockSpec(sk.shape, lambda i: (0, 0, 0))],
        out_specs=[spec, spec, spec],
        out_shape=[out, out, out],
        scratch_shapes=[pltpu.VMEM((2 * PEER_HEADS, PEER_TOPK, tb), F32),
                        pltpu.VMEM((2 * PEER_HEADS, PEER_TOPK, tb), F32),
                        pltpu.VMEM((nsel, tb), F32),
                        pltpu.VMEM((nsel, tb), F32),
                        pltpu.VMEM((nsel, tb), F32)],
        compiler_params=_cparams(("parallel",)),
        name="peer_route",
    )(q, sk)


_ERF_P = 0.3275911
_ERF_A = (0.254829592, -0.284496736, 1.421413741, -1.453152027, 1.061405429)


def _gelu_exact(x):
    return 0.5 * x * (1.0 + lax.erf(x * (1.0 / math.sqrt(2.0))))


def _peer_kernel(xn_ref, i1_ref, i2_ref, gate_ref, h_ref, ut_ref, v_ref, o_ref, gs_ref, w_ref, acc_ref):
    c = pl.program_id(1)
    tb = xn_ref.shape[0]
    ec = ut_ref.shape[1]
    a_per_chunk = ec // N_KEYS

    @pl.when(c == 0)
    def _build_gates():
        key_iota = lax.broadcasted_iota(jnp.int32, (N_KEYS, N_KEYS), 0).astype(F32)

        def one_token(t, carry):
            r1 = i1_ref[pl.ds(t, 1), :]
            r2 = i2_ref[pl.ds(t, 1), :]
            gr = gate_ref[pl.ds(t, 1), :]
            a_t = jnp.where(key_iota == r1, gr, 0.0).astype(BF16)
            b_t = jnp.where(key_iota == r2, 1.0, 0.0).astype(BF16)
            row = pl.multiple_of(t * G_PITCH, 8)
            gs_ref[pl.ds(row, N_KEYS), :] = _dot_nt(a_t, b_t)
            return carry

        lax.fori_loop(0, tb, one_token, 0, unroll=8)
        acc_ref[...] = jnp.zeros_like(acc_ref)

    hid = jnp.dot(xn_ref[...], ut_ref[...], preferred_element_type=F32)
    for al in range(a_per_chunk):
        g_a = gs_ref[pl.ds(c * a_per_chunk + al, tb, stride=G_PITCH), :]
        act = _gelu_exact(hid[:, al * N_KEYS:(al + 1) * N_KEYS])
        w_ref[:, al * N_KEYS:(al + 1) * N_KEYS] = (g_a * act).astype(BF16)
    acc_ref[...] += jnp.dot(w_ref[...], v_ref[...], preferred_element_type=F32)

    @pl.when(c == pl.num_programs(1) - 1)
    def _finish():
        o_ref[...] = h_ref[...] + acc_ref[...]


def _peer_experts(xn, i1, i2, gate, h, ut, v, tb=TOKEN_BLOCK, ec=EXPERT_CHUNK):
    t, d = h.shape
    n_exp = v.shape[0]
    nsel = i1.shape[1]
    tok = lambda i, c: (i, 0)
    return pl.pallas_call(
        _peer_kernel,
        grid=(t // tb, n_exp // ec),
        in_specs=[pl.BlockSpec((tb, d), tok),
                  pl.BlockSpec((tb, nsel), tok),
                  pl.BlockSpec((tb, nsel), tok),
                  pl.BlockSpec((tb, nsel), tok),
                  pl.BlockSpec((tb, d), tok),
                  pl.BlockSpec((d, ec), lambda i, c: (0, c)),
                  pl.BlockSpec((ec, d), lambda i, c: (c, 0))],
        out_specs=pl.BlockSpec((tb, d), tok),
        out_shape=jax.ShapeDtypeStruct((t, d), F32),
        scratch_shapes=[pltpu.VMEM((tb * G_PITCH, N_KEYS), F32),
                        pltpu.VMEM((tb, ec), BF16),
                        pltpu.VMEM((tb, d), F32)],
        compiler_params=_cparams(("parallel", "arbitrary")),
        name="peer_experts",
    )(xn, i1, i2, gate, h, ut, v)


def _peer_ffn(h, g2, wq, subkeys, u, v):
    wq_hi, wq_lo = _split_bf16(wq)
    q, xn = _linear(h, wq_hi, g=g2, w_lo=wq_lo, emit_xn=True, name="peer_query")
    i1, i2, gate = _route(q, subkeys)
    ut = u.astype(BF16).T
    return _peer_experts(xn, i1, i2, gate, h, ut, v.astype(BF16))


def _rope(x, pos):
    half = ROPE_DIM // 2
    inv = ROPE_THETA ** (-jnp.arange(half, dtype=F32) * 2.0 / ROPE_DIM)
    ang = pos.astype(F32)[:, None] * inv[None, :]
    cos = jnp.cos(ang)[None, :, None, :]
    sin = jnp.sin(ang)[None, :, None, :]
    xr = x[..., :ROPE_DIM]
    x1, x2 = xr[..., :half], xr[..., half:]
    rot = jnp.concatenate([x1 * cos - x2 * sin, x2 * cos + x1 * sin], axis=-1)
    return jnp.concatenate([rot, x[..., ROPE_DIM:]], axis=-1)


def _pool_mixer(x_new, buf, pos0, w, scale):
    B, L, _ = x_new.shape
    G = len(POOL_WINDOWS)
    xfull = jnp.concatenate([buf, x_new], axis=1)
    xg = xfull.reshape(B, POOL_BUF + L, G, POOL_GROUP)
    csum = jnp.concatenate([jnp.zeros((B, 1, G, POOL_GROUP), F32), jnp.cumsum(xg, axis=1)], axis=1)
    win = jnp.array(POOL_WINDOWS, jnp.int32)
    t = jnp.arange(L, dtype=jnp.int32)
    hi = t + POOL_BUF + 1
    lo = hi[:, None] - win[None, :]
    gi = jnp.arange(G)[None, :]
    sums = csum[:, hi] - csum[:, lo, gi]
    cnt = jnp.minimum(win[None, :], pos0 + t[:, None] + 1).astype(F32)
    diff = sums / cnt[None, :, :, None] - xg[:, POOL_BUF:]
    y = jnp.einsum('blgc,gcd->blgd', diff, w).reshape(B, L, POOL_WIDTH)
    return y * scale, xfull[:, -POOL_BUF:]


def _combine_branches(outs, lses):
    wts = jax.nn.softmax(jnp.stack(lses, 0), axis=0)
    return jnp.sum(wts[..., None] * jnp.stack(outs, 0), axis=0)


def _dilated_attn_prompt(q, k, v):
    B, S, H, Dh = q.shape
    scale = HEAD_DIM ** -0.5
    outs, lses = [], []
    for window, dil in DILATED_GROUPS:
        nb = window // dil
        blk = nb
        span = dil * blk
        s_pad = -(-S // span) * span
        M = s_pad // dil
        nblk = M // blk

        def to_res(a):
            a = jnp.pad(a, ((0, 0), (0, s_pad - S), (0, 0), (0, 0)))
            a = a.reshape(B, M, dil, H, Dh).transpose(0, 2, 1, 3, 4)
            return a.reshape(B, dil, nblk, blk, H, Dh)

        def with_prev(a):
            prev = jnp.pad(a, ((0, 0), (0, 0), (1, 0), (0, 0), (0, 0), (0, 0)))[:, :, :-1]
            return jnp.concatenate([prev, a], axis=3)

        qr = to_res(q)
        kb = with_prev(to_res(k))
        vb = with_prev(to_res(v))
        s = jnp.einsum('brnqhd,brnkhd->brnhqk', qr, kb, preferred_element_type=F32) * scale
        iq = jnp.arange(blk)[:, None]
        ik = jnp.arange(2 * blk)[None, :]
        rel = blk + iq - ik
        mk = jnp.arange(nblk)[:, None, None] * blk + ik[None] - blk
        valid = (rel >= 0)[None] & (rel <= nb)[None] & (mk >= 0)
        s = jnp.where(valid[None, None, :, None], s, NEG_INF)
        m = jnp.max(s, axis=-1, keepdims=True)
        p = jnp.exp(s - m)
        den = jnp.sum(p, axis=-1, keepdims=True)
        o = jnp.einsum('brnhqk,brnkhd->brnqhd', p / den, vb)
        lse = (m + jnp.log(den))[..., 0]
        o = o.reshape(B, dil, M, H, Dh).transpose(0, 2, 1, 3, 4).reshape(B, s_pad, H, Dh)[:, :S]
        lse = lse.transpose(0, 1, 2, 4, 3).reshape(B, dil, M, H).transpose(0, 2, 1, 3).reshape(B, s_pad, H)[:, :S]
        outs.append(o)
        lses.append(lse)
    return _combine_branches(outs, lses)


def _dilated_attn_sample(q, k_all, v_all, n_past):
    B, T, H, Dh = q.shape
    scale = HEAD_DIM ** -0.5
    outs, lses = [], []
    for window, dil in DILATED_GROUPS:
        nb = window // dil
        idx = n_past + jnp.arange(T)[:, None] - dil * jnp.arange(nb + 1)[None, :]
        valid = idx >= 0
        idc = jnp.maximum(idx, 0)
        kg = k_all[:, idc]
        vg = v_all[:, idc]
        s = jnp.einsum('bthd,btkhd->bthk', q, kg, preferred_element_type=F32) * scale
        s = jnp.where(valid[None, :, None, :], s, NEG_INF)
        m = jnp.max(s, axis=-1, keepdims=True)
        p = jnp.exp(s - m)
        den = jnp.sum(p, axis=-1, keepdims=True)
        o = jnp.einsum('bthk,btkhd->bthd', p / den, vg)
        outs.append(o)
        lses.append((m + jnp.log(den))[..., 0])
    return _combine_branches(outs, lses)


def _split(x):
    hi = x.astype(BF16)
    return hi, (x - hi.astype(F32)).astype(BF16)


def _dg(a, b, dims):
    return lax.dot_general(a, b, (dims, ((), ())), preferred_element_type=F32)


_NN = ((1,), (0,))
_NT = ((1,), (1,))
_TN = ((0,), (0,))


def _mm3(a, b, dims=_NN):
    a_hi, a_lo = _split(a)
    b_hi, b_lo = _split(b)
    return _dg(a_hi, b_hi, dims) + (_dg(a_lo, b_hi, dims) + _dg(a_hi, b_lo, dims))


def _split3(x):
    x0 = x.astype(BF16)
    r1 = x - x0.astype(F32)
    x1 = r1.astype(BF16)
    return x0, x1, (r1 - x1.astype(F32)).astype(BF16)


def _mm_exact_rhs(a, b_bf16):
    a0, a1, a2 = _split3(a)
    return _dg(a0, b_bf16, _NN) + (_dg(a1, b_bf16, _NN) + _dg(a2, b_bf16, _NN))


def _mm_exact_lhs(a_bf16, b):
    b0, b1, b2 = _split3(b)
    return _dg(a_bf16, b0, _NN) + (_dg(a_bf16, b1, _NN) + _dg(a_bf16, b2, _NN))


def _rwkv_chunk(pblk, prev_row, s0, prm):
    (mu, w0, w_up, a0, a_up, g_up, k_k, k_a, r_k, ln_w, ln_b) = prm
    c = pblk.shape[0]
    shifted = jnp.concatenate([prev_row, pblk[:-1]], axis=0) if c > 1 else prev_row
    ps = pblk + (shifted - pblk) * mu
    c1, c2, c3 = RWKV_WIDTH, 2 * RWKV_WIDTH, 3 * RWKV_WIDTH
    c4 = c3 + DECAY_LORA
    c5 = c4 + ICLR_LORA
    r, k, v = ps[:, :c1], ps[:, c1:c2], ps[:, c2:c3]
    wd, ad, gd = ps[:, c3:c4], ps[:, c4:c5], ps[:, c5:]

    z = -(w0 + _mm3(jnp.tanh(wd), w_up))
    softplus = jnp.maximum(z, 0.0) + jnp.log(1.0 + jnp.exp(-jnp.abs(z)))
    lw = -jnp.exp(-softplus - 0.5)
    a = jax.nn.sigmoid(a0 + _mm3(ad, a_up))
    g = _mm3(jax.nn.sigmoid(gd), g_up)

    ch_r = lax.broadcasted_iota(jnp.int32, (RWKV_WIDTH, RWKV_WIDTH), 0) // HEAD_DIM
    ch_c = lax.broadcasted_iota(jnp.int32, (RWKV_WIDTH, RWKV_WIDTH), 1) // HEAD_DIM
    same_head = jnp.where(ch_r == ch_c, 1.0, 0.0).astype(BF16)

    kk = k * k_k
    kk = kk / jnp.maximum(jnp.sqrt(_mm_exact_rhs(kk * kk, same_head)), 1e-12)
    k2 = k * (1.0 + (a - 1.0) * k_a)
    bonus = _mm_exact_rhs(r * k2 * r_k, same_head) * v

    t_r = lax.broadcasted_iota(jnp.int32, (c, c), 0)
    t_c = lax.broadcasted_iota(jnp.int32, (c, c), 1)
    incl = t_c <= t_r
    strict = t_c < t_r
    cs = _mm_exact_lhs(jnp.where(incl, 1.0, 0.0).astype(BF16), lw)
    e_up = jnp.exp(cs)
    e_dn = jnp.exp(-cs)
    kap = kk * jnp.exp(cs - lw)
    bet = kk * a * e_dn
    kt = k2 * e_dn
    rt = r * e_up
    gam = e_up[c - 1:c, :]

    eye = jnp.where(t_r == t_c, 1.0, 0.0)
    outs, s_new = [], []
    for h in range(RWKV_HEADS):
        sl = slice(h * HEAD_DIM, (h + 1) * HEAD_DIM)
        kp_h, bt_h, kt_h, rt_h, v_h = kap[:, sl], bet[:, sl], kt[:, sl], rt[:, sl], v[:, sl]
        s_h = s0[h]
        a_b = jnp.where(strict, _mm3(kp_h, bt_h, _NT), 0.0)
        a_k = jnp.where(strict, _mm3(kp_h, kt_h, _NT), 0.0)
        m_b = jnp.where(incl, _mm3(rt_h, bt_h, _NT), 0.0)
        m_k = jnp.where(incl, _mm3(rt_h, kt_h, _NT), 0.0)
        inv = eye - jnp.where(t_r // 2 == t_c // 2, a_b, 0.0)
        b = 2
        while b < c:
            pair = (t_r // (2 * b) == t_c // (2 * b)) & ((t_r // b) % 2 == 1) & ((t_c // b) % 2 == 0)
            inv = inv - _mm3(inv, _mm3(jnp.where(pair, a_b, 0.0), inv))
            b *= 2
        p_h = _mm3(inv, kp_h)
        q_h = _mm3(inv, _mm3(a_k, v_h))
        u_h = _mm3(p_h, s_h, _NT) + q_h
        o_h = _mm3(rt_h, s_h, _NT) + _mm3(m_k, v_h) - _mm3(m_b, u_h)
        s_h = (s_h + _mm3(v_h, kt_h, _TN) - _mm3(u_h, bt_h, _TN)) * gam[:, sl]
        outs.append(o_h)
        s_new.append(s_h)
    o = jnp.concatenate(outs, axis=1)
    mean = _mm_exact_rhs(o, same_head) * (1.0 / HEAD_DIM)
    d = o - mean
    var = _mm_exact_rhs(d * d, same_head) * (1.0 / HEAD_DIM)
    o = d * lax.rsqrt(var + GN_EPS) * ln_w + ln_b
    return (o + bonus) * g, s_new


def _rwkv_kernel(p_ref, prev_ref, s0_ref, mu_ref, w0_ref, wup_ref, a0_ref, aup_ref, gup_ref, kk_ref, ka_ref,
                 rk_ref, lnw_ref, lnb_ref, o_ref, sT_ref, s_scr, last_scr):
    ci = pl.program_id(1)

    @pl.when(ci == 0)
    def _init():
        s_scr[...] = s0_ref[0]
        last_scr[...] = prev_ref[0]

    prm = (mu_ref[...], w0_ref[...], wup_ref[...], a0_ref[...], aup_ref[...], gup_ref[...], kk_ref[...],
           ka_ref[...], rk_ref[...], lnw_ref[...], lnb_ref[...])
    pblk = p_ref[0]
    out, s_new = _rwkv_chunk(pblk, last_scr[...], [s_scr[h] for h in range(RWKV_HEADS)], prm)
    o_ref[0] = out
    for h in range(RWKV_HEADS):
        s_scr[h] = s_new[h]
    c = pblk.shape[0]
    last_scr[...] = pblk[c - 1:c, :]

    @pl.when(ci == pl.num_programs(1) - 1)
    def _fin():
        sT_ref[0] = s_scr[...]


def _rwkv_mixer(p, prev, wkv0, lp, chunk):
    B, L, _ = p.shape
    assert L % chunk == 0
    row = lambda x: x.reshape(1, -1).astype(F32)
    full = lambda a: pl.BlockSpec(a.shape, lambda b, c: (0,) * a.ndim)
    weights = [row(lp['rwkv_mu']), row(lp['rwkv_w0']), lp['rwkv_w_up'], row(lp['rwkv_a0']), lp['rwkv_a_up'],
               lp['rwkv_g_up'], row(lp['rwkv_k_k']), row(lp['rwkv_k_a']), row(lp['rwkv_r_k']),
               row(lp['rwkv_ln_w']), row(lp['rwkv_ln_b'])]
    out, s_t = pl.pallas_call(
        _rwkv_kernel,
        grid=(B, L // chunk),
        in_specs=[pl.BlockSpec((1, chunk, RWKV_PROJ), lambda b, c: (b, c, 0)),
                  pl.BlockSpec((1, 1, RWKV_PROJ), lambda b, c: (b, 0, 0)),
                  pl.BlockSpec((1, RWKV_HEADS, HEAD_DIM, HEAD_DIM), lambda b, c: (b, 0, 0, 0))]
                 + [full(w) for w in weights],
        out_specs=[pl.BlockSpec((1, chunk, RWKV_WIDTH), lambda b, c: (b, c, 0)),
                   pl.BlockSpec((1, RWKV_HEADS, HEAD_DIM, HEAD_DIM), lambda b, c: (b, 0, 0, 0))],
        out_shape=[jax.ShapeDtypeStruct((B, L, RWKV_WIDTH), F32),
                   jax.ShapeDtypeStruct((B, RWKV_HEADS, HEAD_DIM, HEAD_DIM), F32)],
        scratch_shapes=[pltpu.VMEM((RWKV_HEADS, HEAD_DIM, HEAD_DIM), F32),
                        pltpu.VMEM((1, RWKV_PROJ), F32)],
        compiler_params=_cparams(("parallel", "arbitrary")),
        name="rwkv_mixer",
    )(p, prev.reshape(B, 1, RWKV_PROJ), wkv0, *weights)
    return out, p[:, -1], s_t


def _mixers(proj, lp, pos0, pool_buf, k_buf, v_buf, shift_prev, wkv0):
    B, L, _ = proj.shape
    o1 = POOL_WIDTH
    o2 = o1 + ATT_WIDTH
    o3 = o2 + ATT_WIDTH
    o4 = o3 + ATT_WIDTH
    heads = lambda t: t.reshape(B, L, ATT_HEADS, HEAD_DIM)
    pos = pos0 + jnp.arange(L, dtype=jnp.int32)
    q = _rope(heads(proj[..., o1:o2]), pos)
    k = _rope(heads(proj[..., o2:o3]), pos)
    v = heads(proj[..., o3:o4])
    pool_out, pool_new = _pool_mixer(proj[..., :o1], pool_buf, pos0, lp['pool_w'], lp['pool_scale'])
    if k_buf is None:
        att = _dilated_attn_prompt(q, k, v)
        keep = min(ATT_BUF, L)
        k_new, v_new = k[:, -keep:], v[:, -keep:]
    else:
        n_past = k_buf.shape[1]
        k_all = jnp.concatenate([k_buf, k], axis=1)
        v_all = jnp.concatenate([v_buf, v], axis=1)
        att = _dilated_attn_sample(q, k_all, v_all, n_past)
        k_new, v_new = k_all[:, -n_past:], v_all[:, -n_past:]
    rw_out, shift_new, wkv_new = _rwkv_mixer(proj[..., o4:], shift_prev, wkv0, lp, min(L, RWKV_CHUNK))
    mixed = jnp.concatenate([pool_out, att.reshape(B, L, ATT_WIDTH), rw_out], axis=-1)
    return mixed, pool_new, k_new, v_new, shift_new, wkv_new


def kernel(x_prompt, x_sample, state_pool, cache_k, cache_v, state_shift, state_wkv, norm1_g, norm2_g, final_g, w_in, w_out, pool_w, pool_scale, rwkv_mu, rwkv_w0, rwkv_w_up, rwkv_a0, rwkv_a_up, rwkv_g_up, rwkv_k_k, rwkv_k_a, rwkv_r_k, rwkv_ln_w, rwkv_ln_b, peer_wq, peer_subkeys, peer_u, peer_v):
    bp, sp_len, d = x_prompt.shape
    bs, ss_len, _ = x_sample.shape
    tp = bp * sp_len
    depth = w_in.shape[0]
    past_len = 8192
    h = jnp.concatenate([x_prompt.reshape(tp, d), x_sample.reshape(bs * ss_len, d)], axis=0)
    outs_p = [[] for _ in range(5)]
    outs_s = [[] for _ in range(5)]
    for l in range(depth):
        lp = {'pool_w': pool_w[l], 'pool_scale': pool_scale[l], 'rwkv_mu': rwkv_mu[l], 'rwkv_w0': rwkv_w0[l],
              'rwkv_w_up': rwkv_w_up[l], 'rwkv_a0': rwkv_a0[l], 'rwkv_a_up': rwkv_a_up[l],
              'rwkv_g_up': rwkv_g_up[l], 'rwkv_k_k': rwkv_k_k[l], 'rwkv_k_a': rwkv_k_a[l],
              'rwkv_r_k': rwkv_r_k[l], 'rwkv_ln_w': rwkv_ln_w[l], 'rwkv_ln_b': rwkv_ln_b[l]}
        proj = _linear(h, w_in[l].astype(BF16), g=norm1_g[l], name="in_proj")
        mixed_p, *st_p = _mixers(
            proj[:tp].reshape(bp, sp_len, -1), lp, 0,
            jnp.zeros((bp, POOL_BUF, POOL_WIDTH), F32), None, None,
            jnp.zeros((bp, RWKV_PROJ), F32), jnp.zeros((bp, RWKV_HEADS, HEAD_DIM, HEAD_DIM), F32))
        mixed_s, *st_s = _mixers(
            proj[tp:].reshape(bs, ss_len, -1), lp, past_len,
            state_pool[l], cache_k[l], cache_v[l], state_shift[l], state_wkv[l])
        for acc, val in zip(outs_p, st_p):
            acc.append(val)
        for acc, val in zip(outs_s, st_s):
            acc.append(val)
        mixed = jnp.concatenate([mixed_p.reshape(tp, -1), mixed_s.reshape(bs * ss_len, -1)], axis=0)
        h = _linear(mixed, w_out[l].astype(BF16), resid=h, name="out_proj")
        h = _peer_ffn(h, norm2_g[l], peer_wq[l], peer_subkeys[l], peer_u[l], peer_v[l])
    y = _rmsnorm(h, final_g)
    y_prompt = y[:tp].reshape(bp, sp_len, d)
    y_sample = y[tp:].reshape(bs, ss_len, d)
    return (y_prompt, y_sample, *[jnp.stack(a) for a in outs_p], *[jnp.stack(a) for a in outs_s])
```

```python
import functools
import math

import jax
import jax.numpy as jnp
import numpy as np
from jax import lax
from jax.experimental import pallas as pl
from jax.experimental.pallas import tpu as pltpu

F32 = jnp.float32
BF16 = jnp.bfloat16

D_MODEL = 1024
HEAD_DIM = 64
POOL_WINDOWS = (2, 4, 8, 16)
POOL_GROUP = 64
POOL_WIDTH = 256
POOL_BUF = 15
ATT_WIDTH = 384
ATT_HEADS = 6
DILATED_GROUPS = ((128, 1), (512, 4), (2048, 16))
ATT_BUF = 2048
ROPE_DIM = 16
ROPE_THETA = 500000.0
NEG_INF = -1e30
RWKV_WIDTH = 384
RWKV_HEADS = 6
DECAY_LORA = 64
ICLR_LORA = 64
GATE_LORA = 128
RWKV_PROJ = 1408
GN_EPS = HEAD_DIM * 1e-5
N_KEYS = 128
N_EXPERTS = N_KEYS * N_KEYS
PEER_HEADS = 8
PEER_TOPK = 16
RMS_EPS = 1e-6

TOKEN_BLOCK = 256
EXPERT_CHUNK = 1024
G_PITCH = 136
RWKV_CHUNK = 64
RWKV_ROWS = 2
LANES = 128
IN_WIDTH = POOL_WIDTH + 3 * ATT_WIDTH + RWKV_PROJ
ATT_QBLOCK = 128
ATT_WINDOW = ATT_BUF + ATT_QBLOCK
PAST_LEN = 8192
VMEM_LIMIT = 56 * 1024 * 1024


def _cparams(sem):
    return pltpu.CompilerParams(dimension_semantics=sem, vmem_limit_bytes=VMEM_LIMIT)


def _rope_tables(pos):
    half = ROPE_DIM // 2
    inv = ROPE_THETA ** (-jnp.arange(half, dtype=F32) * 2.0 / ROPE_DIM)
    ang = pos.astype(F32)[:, None] * inv[None, :]
    cos, sin = jnp.cos(ang), jnp.sin(ang)
    n = pos.shape[0]
    one = jnp.ones((n, HEAD_DIM - ROPE_DIM), F32)
    zero = jnp.zeros((n, HEAD_DIM - ROPE_DIM), F32)
    zh = jnp.zeros((n, half), F32)
    c = jnp.concatenate([cos, cos, one], axis=1)
    a = jnp.concatenate([-sin, zh, zero], axis=1)
    b = jnp.concatenate([zh, sin, zero], axis=1)
    tile = lambda x: jnp.concatenate([x, x], axis=1)
    return tile(c), tile(a), tile(b)


def _rope_lanes(x, c, a, b):
    half = ROPE_DIM // 2
    return x * c + pltpu.roll(x, LANES - half, 1) * a + pltpu.roll(x, half, 1) * b


def _linear_kernel(*refs, norm, resid, passes, emit_xn, rope):
    it = iter(refs)
    x_ref = next(it)
    g_ref = next(it) if norm else None
    w_ref = next(it)
    wlo_ref = next(it) if passes == 3 else None
    r_ref = next(it) if resid else None
    rope_refs = [next(it) for _ in range(3)] if rope else None
    o_ref = next(it)
    xn_ref = next(it) if emit_xn else None

    x = x_ref[...].astype(F32)
    if norm:
        x = x * lax.rsqrt(jnp.mean(x * x, axis=-1, keepdims=True) + RMS_EPS) * g_ref[...]
    if emit_xn:
        xn_ref[...] = x.astype(xn_ref.dtype)
    x_hi = x.astype(BF16)
    acc = jnp.dot(x_hi, w_ref[...], preferred_element_type=F32)
    if passes == 3:
        x_lo = (x - x_hi.astype(F32)).astype(BF16)
        acc = acc + jnp.dot(x_lo, w_ref[...], preferred_element_type=F32)
        acc = acc + jnp.dot(x_hi, wlo_ref[...], preferred_element_type=F32)
    if resid:
        acc = acc + r_ref[...]
    if rope:
        c, a, b = (r[...] for r in rope_refs)
        lo, hi = POOL_WIDTH, POOL_WIDTH + 2 * ATT_WIDTH
        o_ref[:, :lo] = acc[:, :lo]
        for j in range(lo, hi, LANES):
            o_ref[:, j:j + LANES] = _rope_lanes(acc[:, j:j + LANES], c, a, b)
        o_ref[:, hi:] = acc[:, hi:]
    else:
        o_ref[...] = acc


def _linear(x, w_hi, *, g=None, w_lo=None, resid=None, emit_xn=False, rope=None, tb=TOKEN_BLOCK, name="linear"):
    t, k = x.shape
    n = w_hi.shape[1]
    assert t % tb == 0
    norm = g is not None
    passes = 3 if w_lo is not None else 1
    ins = [x]
    specs = [pl.BlockSpec((tb, k), lambda i: (i, 0))]
    if norm:
        ins.append(g.reshape(1, k).astype(F32))
        specs.append(pl.BlockSpec((1, k), lambda i: (0, 0)))
    ins.append(w_hi)
    specs.append(pl.BlockSpec((k, n), lambda i: (0, 0)))
    if passes == 3:
        ins.append(w_lo)
        specs.append(pl.BlockSpec((k, n), lambda i: (0, 0)))
    if resid is not None:
        ins.append(resid)
        specs.append(pl.BlockSpec((tb, n), lambda i: (i, 0)))
    if rope is not None:
        ins.extend(rope)
        specs.extend([pl.BlockSpec((tb, LANES), lambda i: (i, 0))] * 3)
    out_shape = [jax.ShapeDtypeStruct((t, n), F32)]
    out_specs = [pl.BlockSpec((tb, n), lambda i: (i, 0))]
    if emit_xn:
        out_shape.append(jax.ShapeDtypeStruct((t, k), BF16))
        out_specs.append(pl.BlockSpec((tb, k), lambda i: (i, 0)))
    outs = pl.pallas_call(
        functools.partial(_linear_kernel, norm=norm, resid=resid is not None, passes=passes, emit_xn=emit_xn,
                          rope=rope is not None),
        grid=(t // tb,),
        in_specs=specs,
        out_specs=out_specs,
        out_shape=out_shape,
        compiler_params=_cparams(("parallel",)),
        name=name,
    )(*ins)
    return outs if emit_xn else outs[0]


def _split_bf16(w):
    hi = w.astype(BF16)
    lo = (w - hi.astype(F32)).astype(BF16)
    return hi, lo


def _rmsnorm_kernel(x_ref, g_ref, o_ref):
    x = x_ref[...]
    o_ref[...] = x * lax.rsqrt(jnp.mean(x * x, axis=-1, keepdims=True) + RMS_EPS) * g_ref[...]


def _rmsnorm(x, g, tb=TOKEN_BLOCK):
    t, d = x.shape
    return pl.pallas_call(
        _rmsnorm_kernel,
        grid=(t // tb,),
        in_specs=[pl.BlockSpec((tb, d), lambda i: (i, 0)), pl.BlockSpec((1, d), lambda i: (0, 0))],
        out_specs=pl.BlockSpec((tb, d), lambda i: (i, 0)),
        out_shape=jax.ShapeDtypeStruct((t, d), F32),
        compiler_params=_cparams(("parallel",)),
        name="final_norm",
    )(x, g.reshape(1, d))


def _dot_nt(a, b):
    return lax.dot_general(a, b, (((1,), (1,)), ((), ())), preferred_element_type=F32)


def _route_kernel(q_ref, sk_ref, i1_ref, i2_ref, gate_ref, tv_ref, ti_ref, e1_ref, e2_ref, gt_ref):
    tb = q_ref.shape[0]
    neg = jnp.float32(-jnp.inf)
    key_iota = lax.broadcasted_iota(jnp.int32, (N_KEYS, tb), 0).astype(F32)

    def half_topk(hc, carry):
        off = pl.multiple_of(hc * N_KEYS, N_KEYS)
        qh = q_ref[:, pl.ds(off, N_KEYS)]
        sk = sk_ref[hc]
        q_hi = qh.astype(BF16)
        q_lo = (qh - q_hi.astype(F32)).astype(BF16)
        s_hi = sk.astype(BF16)
        s_lo = (sk - s_hi.astype(F32)).astype(BF16)
        sc = _dot_nt(s_hi, q_hi) + _dot_nt(s_lo, q_hi) + _dot_nt(s_hi, q_lo)
        vals, idxs = [], []
        for _ in range(PEER_TOPK):
            m = jnp.max(sc, axis=0, keepdims=True)
            idx = jnp.min(jnp.where(sc == m, key_iota, float(N_KEYS)), axis=0, keepdims=True)
            sc = jnp.where(key_iota == idx, neg, sc)
            vals.append(m)
            idxs.append(idx)
        tv_ref[hc] = jnp.concatenate(vals, axis=0)
        ti_ref[hc] = jnp.concatenate(idxs, axis=0)
        return carry

    lax.fori_loop(0, 2 * PEER_HEADS, half_topk, 0)

    n_rows = 16 + 8 * 8
    row8 = lax.broadcasted_iota(jnp.int32, (8, tb), 0)
    cand_iota = lax.broadcasted_iota(jnp.int32, (n_rows, tb), 0).astype(F32)

    def pair_topk(h, carry):
        a = tv_ref[2 * h]
        b = tv_ref[2 * h + 1]
        ia = ti_ref[2 * h]
        ib = ti_ref[2 * h + 1]
        vs = [a[0:1] + b]
        p1 = [jnp.broadcast_to(ia[0:1], (16, tb))]
        p2 = [ib]
        for i in range(1, 8):
            lim = PEER_TOPK // (i + 1)
            v = a[i:i + 1] + b[0:8]
            vs.append(jnp.where(row8 < lim, v, neg) if lim < 8 else v)
            p1.append(jnp.broadcast_to(ia[i:i + 1], (8, tb)))
            p2.append(ib[0:8])
        vs.append(a[8:16] + b[0:1])
        p1.append(ia[8:16])
        p2.append(jnp.broadcast_to(ib[0:1], (8, tb)))
        cand = jnp.concatenate(vs, axis=0)
        c1 = jnp.concatenate(p1, axis=0)
        c2 = jnp.concatenate(p2, axis=0)
        tops, s1, s2 = [], [], []
        for _ in range(PEER_TOPK):
            m = jnp.max(cand, axis=0, keepdims=True)
            pos = jnp.min(jnp.where(cand == m, cand_iota, float(n_rows)), axis=0, keepdims=True)
            hit = cand_iota == pos
            s1.append(jnp.sum(jnp.where(hit, c1, 0.0), axis=0, keepdims=True))
            s2.append(jnp.sum(jnp.where(hit, c2, 0.0), axis=0, keepdims=True))
            cand = jnp.where(hit, neg, cand)
            tops.append(m)
        s = jnp.concatenate(tops, axis=0)
        e = jnp.exp(s - s[0:1])
        gate = e / jnp.sum(e, axis=0, keepdims=True)
        row = pl.multiple_of(h * PEER_TOPK, PEER_TOPK)
        gt_ref[pl.ds(row, PEER_TOPK), :] = gate
        e1_ref[pl.ds(row, PEER_TOPK), :] = jnp.concatenate(s1, axis=0)
        e2_ref[pl.ds(row, PEER_TOPK), :] = jnp.concatenate(s2, axis=0)
        return carry

    lax.fori_loop(0, PEER_HEADS, pair_topk, 0)
    i1_ref[...] = e1_ref[...].T
    i2_ref[...] = e2_ref[...].T
    gate_ref[...] = gt_ref[...].T


def _route(q, subkeys, tb=TOKEN_BLOCK):
    t = q.shape[0]
    nsel = PEER_HEADS * PEER_TOPK
    sk = subkeys.reshape(2 * PEER_HEADS, N_KEYS, N_KEYS)
    out = jax.ShapeDtypeStruct((t, nsel), F32)
    spec = pl.BlockSpec((tb, nsel), lambda i: (i, 0))
    return pl.pallas_call(
        _route_kernel,
        grid=(t // tb,),
        in_specs=[pl.BlockSpec((tb, q.shape[1]), lambda i: (i, 0)),
                  pl.BlockSpec(sk.shape, lambda i: (0, 0, 0))],
        out_specs=[spec, spec, spec],
        out_shape=[out, out, out],
        scratch_shapes=[pltpu.VMEM((2 * PEER_HEADS, PEER_TOPK, tb), F32),
                        pltpu.VMEM((2 * PEER_HEADS, PEER_TOPK, tb), F32),
                        pltpu.VMEM((nsel, tb), F32),
                        pltpu.VMEM((nsel, tb), F32),
                        pltpu.VMEM((nsel, tb), F32)],
        compiler_params=_cparams(("parallel",)),
        name="peer_route",
    )(q, sk)


def _gelu_exact(x):
    return 0.5 * x * (1.0 + lax.erf(x * (1.0 / math.sqrt(2.0))))


def _peer_kernel(xn_ref, i1_ref, i2_ref, gate_ref, h_ref, ut_ref, v_ref, o_ref, gs_ref, w_ref, acc_ref):
    c = pl.program_id(1)
    tb = xn_ref.shape[0]
    ec = ut_ref.shape[1]
    a_per_chunk = ec // N_KEYS

    @pl.when(c == 0)
    def _build_gates():
        key_iota = lax.broadcasted_iota(jnp.int32, (N_KEYS, N_KEYS), 0).astype(F32)

        def one_token(t, carry):
            r1 = i1_ref[pl.ds(t, 1), :]
            r2 = i2_ref[pl.ds(t, 1), :]
            gr = gate_ref[pl.ds(t, 1), :]
            a_t = jnp.where(key_iota == r1, gr, 0.0).astype(BF16)
            b_t = jnp.where(key_iota == r2, 1.0, 0.0).astype(BF16)
            row = pl.multiple_of(t * G_PITCH, 8)
            gs_ref[pl.ds(row, N_KEYS), :] = _dot_nt(a_t, b_t)
            return carry

        lax.fori_loop(0, tb, one_token, 0, unroll=8)
        acc_ref[...] = jnp.zeros_like(acc_ref)

    hid = jnp.dot(xn_ref[...], ut_ref[...], preferred_element_type=F32)
    for al in range(a_per_chunk):
        g_a = gs_ref[pl.ds(c * a_per_chunk + al, tb, stride=G_PITCH), :]
        act = _gelu_exact(hid[:, al * N_KEYS:(al + 1) * N_KEYS])
        w_ref[:, al * N_KEYS:(al + 1) * N_KEYS] = (g_a * act).astype(BF16)
    acc_ref[...] += jnp.dot(w_ref[...], v_ref[...], preferred_element_type=F32)

    @pl.when(c == pl.num_programs(1) - 1)
    def _finish():
        o_ref[...] = h_ref[...] + acc_ref[...]


def _peer_experts(xn, i1, i2, gate, h, ut, v, tb=TOKEN_BLOCK, ec=EXPERT_CHUNK):
    t, d = h.shape
    n_exp = v.shape[0]
    nsel = i1.shape[1]
    tok = lambda i, c: (i, 0)
    return pl.pallas_call(
        _peer_kernel,
        grid=(t // tb, n_exp // ec),
        in_specs=[pl.BlockSpec((tb, d), tok),
                  pl.BlockSpec((tb, nsel), tok),
                  pl.BlockSpec((tb, nsel), tok),
                  pl.BlockSpec((tb, nsel), tok),
                  pl.BlockSpec((tb, d), tok),
                  pl.BlockSpec((d, ec), lambda i, c: (0, c)),
                  pl.BlockSpec((ec, d), lambda i, c: (c, 0))],
        out_specs=pl.BlockSpec((tb, d), tok),
        out_shape=jax.ShapeDtypeStruct((t, d), F32),
        scratch_shapes=[pltpu.VMEM((tb * G_PITCH, N_KEYS), F32),
                        pltpu.VMEM((tb, ec), BF16),
                        pltpu.VMEM((tb, d), F32)],
        compiler_params=_cparams(("parallel", "arbitrary")),
        name="peer_experts",
    )(xn, i1, i2, gate, h, ut, v)


def _peer_ffn(h, g2, wq, subkeys, u, v):
    wq_hi, wq_lo = _split_bf16(wq)
    q, xn = _linear(h, wq_hi, g=g2, w_lo=wq_lo, emit_xn=True, name="peer_query")
    i1, i2, gate = _route(q, subkeys)
    ut = u.astype(BF16).T
    return _peer_experts(xn, i1, i2, gate, h, ut, v.astype(BF16))


def _band_bias(rows):
    r = np.arange(rows)[:, None]
    c = np.arange(ATT_WINDOW)[None, :]
    delta = ATT_BUF + r - c
    mult = np.zeros(delta.shape, np.float64)
    for window, dil in DILATED_GROUPS:
        mult += (delta >= 0) & (delta <= window) & (delta % dil == 0)
    return jnp.asarray(np.where(mult > 0, np.log(np.maximum(mult, 1.0)), NEG_INF), F32)


def _band_attend(q, kw, vw, bias):
    lane = lax.broadcasted_iota(jnp.int32, q.shape, 1)
    outs = []
    for half in range(2):
        in_half = (lane // HEAD_DIM) == half
        s = _dg(jnp.where(in_half, q, jnp.zeros_like(q)), kw, _NT) + bias
        m = jnp.max(s, axis=1, keepdims=True)
        p = jnp.exp(s - m)
        den = jnp.sum(p, axis=1, keepdims=True)
        outs.append(_dg(p.astype(BF16), vw, _NN) / den)
    return jnp.where((lane // HEAD_DIM) == 0, outs[0], outs[1])


def _attn_prompt_kernel(q_ref, k_ref, v_ref, bias_ref, o_ref, q_scr, k_scr, v_scr):
    s_len = q_ref.shape[0]
    zeros = jnp.zeros((ATT_BUF, LANES), BF16)
    k_scr[0:ATT_BUF, :] = zeros
    v_scr[0:ATT_BUF, :] = zeros
    k_scr[ATT_BUF:, :] = k_ref[...].astype(BF16)
    v_scr[ATT_BUF:, :] = v_ref[...].astype(BF16)
    q_scr[...] = (q_ref[...] * (HEAD_DIM ** -0.5)).astype(BF16)
    col = lax.broadcasted_iota(jnp.int32, (1, ATT_WINDOW), 1)

    def qblock(qi, carry):
        r0 = pl.multiple_of(qi * ATT_QBLOCK, ATT_QBLOCK)
        bias = jnp.where(col >= ATT_BUF - r0, bias_ref[...], NEG_INF)
        o_ref[pl.ds(r0, ATT_QBLOCK), :] = _band_attend(
            q_scr[pl.ds(r0, ATT_QBLOCK), :], k_scr[pl.ds(r0, ATT_WINDOW), :], v_scr[pl.ds(r0, ATT_WINDOW), :], bias)
        return carry

    lax.fori_loop(0, s_len // ATT_QBLOCK, qblock, 0)


_Q_BLOCK0 = POOL_WIDTH // LANES
_K_BLOCK0 = (POOL_WIDTH + ATT_WIDTH) // LANES
_V_BLOCK0 = (POOL_WIDTH + 2 * ATT_WIDTH) // LANES


def _attn_prompt(proj, n_seq, s_len):
    blk = lambda first: pl.BlockSpec((s_len, LANES), lambda b, hp: (b, first + hp))
    bias = _band_bias(ATT_QBLOCK)
    return pl.pallas_call(
        _attn_prompt_kernel,
        grid=(n_seq, ATT_WIDTH // LANES),
        in_specs=[blk(_Q_BLOCK0), blk(_K_BLOCK0), blk(_V_BLOCK0), pl.BlockSpec(bias.shape, lambda b, hp: (0, 0))],
        out_specs=pl.BlockSpec((s_len, LANES), lambda b, hp: (b, hp)),
        out_shape=jax.ShapeDtypeStruct((n_seq * s_len, ATT_WIDTH), F32),
        scratch_shapes=[pltpu.VMEM((s_len, LANES), BF16),
                        pltpu.VMEM((ATT_BUF + s_len, LANES), BF16),
                        pltpu.VMEM((ATT_BUF + s_len, LANES), BF16)],
        compiler_params=_cparams(("parallel", "parallel")),
        name="attn_prompt",
    )(proj, proj, proj, bias)


def _attn_sample_kernel(q_ref, k_ref, v_ref, ck_ref, cv_ref, bias_ref, o_ref, nk_ref, nv_ref, k_scr, v_scr):
    t = q_ref.shape[0]
    n_past = ck_ref.shape[0]
    pad = jnp.zeros((ATT_WINDOW - n_past - t, LANES), BF16)
    k_scr[0:n_past, :] = ck_ref[...].astype(BF16)
    v_scr[0:n_past, :] = cv_ref[...].astype(BF16)
    k_scr[n_past:n_past + t, :] = k_ref[...].astype(BF16)
    v_scr[n_past:n_past + t, :] = v_ref[...].astype(BF16)
    k_scr[n_past + t:, :] = pad
    v_scr[n_past + t:, :] = pad
    q = (q_ref[...] * (HEAD_DIM ** -0.5)).astype(BF16)
    o_ref[...] = _band_attend(q, k_scr[...], v_scr[...], bias_ref[...])
    nk_ref[0:n_past - t, :] = ck_ref[t:n_past, :]
    nv_ref[0:n_past - t, :] = cv_ref[t:n_past, :]
    nk_ref[n_past - t:n_past, :] = k_ref[...]
    nv_ref[n_past - t:n_past, :] = v_ref[...]


def _attn_sample(proj, row0, n_seq, t_len, cache_k, cache_v):
    assert row0 % t_len == 0 and t_len % 8 == 0 and t_len <= ATT_QBLOCK
    blk0 = row0 // t_len
    blk = lambda first: pl.BlockSpec((t_len, LANES), lambda b, hp: (blk0 + b, first + hp))
    cblk = pl.BlockSpec((ATT_BUF, LANES), lambda b, hp: (b, hp))
    bias = _band_bias(t_len)
    return pl.pallas_call(
        _attn_sample_kernel,
        grid=(n_seq, ATT_WIDTH // LANES),
        in_specs=[blk(_Q_BLOCK0), blk(_K_BLOCK0), blk(_V_BLOCK0), cblk, cblk,
                  pl.BlockSpec(bias.shape, lambda b, hp: (0, 0))],
        out_specs=[pl.BlockSpec((t_len, LANES), lambda b, hp: (b, hp)), cblk, cblk],
        out_shape=[jax.ShapeDtypeStruct((n_seq * t_len, ATT_WIDTH), F32),
                   jax.ShapeDtypeStruct(cache_k.shape, F32), jax.ShapeDtypeStruct(cache_v.shape, F32)],
        scratch_shapes=[pltpu.VMEM((ATT_WINDOW, LANES), BF16), pltpu.VMEM((ATT_WINDOW, LANES), BF16)],
        compiler_params=_cparams(("parallel", "parallel")),
        name="attn_sample",
    )(proj, proj, proj, cache_k, cache_v, bias)


def _pool_kernel(x_ref, buf_ref, w_ref, scale_ref, o_ref, *, pos0):
    x = x_ref[...]
    n = x.shape[0]
    xf = jnp.concatenate([buf_ref[...], x], axis=0)
    s2 = xf + pltpu.roll(xf, 1, 0)
    s4 = s2 + pltpu.roll(s2, 2, 0)
    s8 = s4 + pltpu.roll(s4, 4, 0)
    s16 = s8 + pltpu.roll(s8, 8, 0)
    group = lax.broadcasted_iota(jnp.int32, (1, POOL_WIDTH), 1) // POOL_GROUP
    sums = jnp.where(group == 0, s2, jnp.where(group == 1, s4, jnp.where(group == 2, s8, s16)))[POOL_BUF + 1:]
    win = jnp.where(group == 0, 2.0, jnp.where(group == 1, 4.0, jnp.where(group == 2, 8.0, 16.0)))
    seen = (lax.broadcasted_iota(jnp.int32, (n, 1), 0) + (pos0 + 1)).astype(F32)
    diff = sums / jnp.minimum(win, seen) - x
    o_ref[...] = _mm3(diff, w_ref[...]) * scale_ref[...]


def _pool(proj, row0, n_seq, s_len, buf, w, scale, pos0):
    assert POOL_WINDOWS == (2, 4, 8, 16) and row0 % s_len == 0
    blk0 = row0 // s_len
    wbd = jnp.zeros((POOL_WIDTH, POOL_WIDTH), F32)
    for i in range(len(POOL_WINDOWS)):
        wbd = wbd.at[i * POOL_GROUP:(i + 1) * POOL_GROUP, i * POOL_GROUP:(i + 1) * POOL_GROUP].set(w[i])
    buf16 = jnp.pad(buf, ((0, 0), (1, 0), (0, 0))).reshape(n_seq * (POOL_BUF + 1), POOL_WIDTH)
    return pl.pallas_call(
        functools.partial(_pool_kernel, pos0=pos0),
        grid=(n_seq,),
        in_specs=[pl.BlockSpec((s_len, POOL_WIDTH), lambda b: (blk0 + b, 0)),
                  pl.BlockSpec((POOL_BUF + 1, POOL_WIDTH), lambda b: (b, 0)),
                  pl.BlockSpec((POOL_WIDTH, POOL_WIDTH), lambda b: (0, 0)),
                  pl.BlockSpec((1, POOL_WIDTH), lambda b: (0, 0))],
        out_specs=pl.BlockSpec((s_len, POOL_WIDTH), lambda b: (b, 0)),
        out_shape=jax.ShapeDtypeStruct((n_seq * s_len, POOL_WIDTH), F32),
        compiler_params=_cparams(("parallel",)),
        name="pool_mixer",
    )(proj, buf16, wbd, scale.reshape(1, POOL_WIDTH))


def _split(x):
    hi = x.astype(BF16)
    return hi, (x - hi.astype(F32)).astype(BF16)


def _dg(a, b, dims):
    return lax.dot_general(a, b, (dims, ((), ())), preferred_element_type=F32)


_NN = ((1,), (0,))
_NT = ((1,), (1,))
_TN = ((0,), (0,))


def _mm3(a, b, dims=_NN):
    a_hi, a_lo = _split(a)
    b_hi, b_lo = _split(b)
    return _dg(a_hi, b_hi, dims) + (_dg(a_lo, b_hi, dims) + _dg(a_hi, b_lo, dims))


def _split3(x):
    x0 = x.astype(BF16)
    r1 = x - x0.astype(F32)
    x1 = r1.astype(BF16)
    return x0, x1, (r1 - x1.astype(F32)).astype(BF16)


def _mm_exact_rhs(a, b_bf16):
    a0, a1, a2 = _split3(a)
    return _dg(a0, b_bf16, _NN) + (_dg(a1, b_bf16, _NN) + _dg(a2, b_bf16, _NN))


def _mm_exact_lhs(a_bf16, b):
    b0, b1, b2 = _split3(b)
    return _dg(a_bf16, b0, _NN) + (_dg(a_bf16, b1, _NN) + _dg(a_bf16, b2, _NN))


def _rwkv_chunk(pblks, prev_rows, s0, prm):
    (mu, w0, w_up, a0, a_up, g_up, k_k, k_a, r_k, ln_w, ln_b) = prm
    n_seq = len(pblks)
    c = pblks[0].shape[0]
    n = n_seq * c
    pblk = jnp.concatenate(pblks, axis=0) if n_seq > 1 else pblks[0]
    shifted = [jnp.concatenate([prev_rows[i], pblks[i][:-1]], axis=0) if c > 1 else prev_rows[i]
               for i in range(n_seq)]
    shifted = jnp.concatenate(shifted, axis=0) if n_seq > 1 else shifted[0]
    ps = pblk + (shifted - pblk) * mu
    c1, c2, c3 = RWKV_WIDTH, 2 * RWKV_WIDTH, 3 * RWKV_WIDTH
    c4 = c3 + DECAY_LORA
    c5 = c4 + ICLR_LORA
    r, k, v = ps[:, :c1], ps[:, c1:c2], ps[:, c2:c3]
    wd, ad, gd = ps[:, c3:c4], ps[:, c4:c5], ps[:, c5:]

    z = -(w0 + _mm3(jnp.tanh(wd), w_up))
    softplus = jnp.maximum(z, 0.0) + jnp.log(1.0 + jnp.exp(-jnp.abs(z)))
    lw = -jnp.exp(-softplus - 0.5)
    a = jax.nn.sigmoid(a0 + _mm3(ad, a_up))
    g = _mm3(jax.nn.sigmoid(gd), g_up)

    ch_r = lax.broadcasted_iota(jnp.int32, (RWKV_WIDTH, RWKV_WIDTH), 0) // HEAD_DIM
    ch_c = lax.broadcasted_iota(jnp.int32, (RWKV_WIDTH, RWKV_WIDTH), 1) // HEAD_DIM
    same_head = jnp.where(ch_r == ch_c, 1.0, 0.0).astype(BF16)

    kk = k * k_k
    kk = kk / jnp.maximum(jnp.sqrt(_mm_exact_rhs(kk * kk, same_head)), 1e-12)
    k2 = k * (1.0 + (a - 1.0) * k_a)
    bonus = _mm_exact_rhs(r * k2 * r_k, same_head) * v

    n_r = lax.broadcasted_iota(jnp.int32, (n, n), 0)
    n_c = lax.broadcasted_iota(jnp.int32, (n, n), 1)
    run = (n_r // c == n_c // c) & (n_c <= n_r)
    cs = _mm_exact_lhs(jnp.where(run, 1.0, 0.0).astype(BF16), lw)
    e_up = jnp.exp(cs)
    e_dn = jnp.exp(-cs)
    kap = kk * jnp.exp(cs - lw)
    bet = kk * a * e_dn
    kt = k2 * e_dn
    rt = r * e_up

    t_r = lax.broadcasted_iota(jnp.int32, (c, c), 0)
    t_c = lax.broadcasted_iota(jnp.int32, (c, c), 1)
    incl = t_c <= t_r
    strict = t_c < t_r
    eye = jnp.where(t_r == t_c, 1.0, 0.0)
    chains = [(i, h) for i in range(n_seq) for h in range(RWKV_HEADS)]
    cut = lambda x, i, h: x[i * c:(i + 1) * c, h * HEAD_DIM:(h + 1) * HEAD_DIM]
    kp_x = [cut(kap, i, h) for i, h in chains]
    bt_x = [cut(bet, i, h) for i, h in chains]
    kt_x = [cut(kt, i, h) for i, h in chains]
    rt_x = [cut(rt, i, h) for i, h in chains]
    v_x = [cut(v, i, h) for i, h in chains]
    s_x = [s0[i][h] for i, h in chains]
    every = range(len(chains))
    a_b = [jnp.where(strict, _mm3(kp_x[x], bt_x[x], _NT), 0.0) for x in every]
    a_k = [jnp.where(strict, _mm3(kp_x[x], kt_x[x], _NT), 0.0) for x in every]
    m_b = [jnp.where(incl, _mm3(rt_x[x], bt_x[x], _NT), 0.0) for x in every]
    m_k = [jnp.where(incl, _mm3(rt_x[x], kt_x[x], _NT), 0.0) for x in every]
    akv = [_mm3(a_k[x], v_x[x]) for x in every]
    mkv = [_mm3(m_k[x], v_x[x]) for x in every]
    rs = [_mm3(rt_x[x], s_x[x], _NT) for x in every]
    vk = [_mm3(v_x[x], kt_x[x], _TN) for x in every]
    inv = [eye - jnp.where(t_r // 2 == t_c // 2, a_b[x], 0.0) for x in every]
    b = 2
    while b < c:
        pair = (t_r // (2 * b) == t_c // (2 * b)) & ((t_r // b) % 2 == 1) & ((t_c // b) % 2 == 0)
        low = [_mm3(jnp.where(pair, a_b[x], 0.0), inv[x]) for x in every]
        inv = [inv[x] - _mm3(inv[x], low[x]) for x in every]
        b *= 2
    p_x = [_mm3(inv[x], kp_x[x]) for x in every]
    q_x = [_mm3(inv[x], akv[x]) for x in every]
    u_x = [_mm3(p_x[x], s_x[x], _NT) + q_x[x] for x in every]
    o_x = [rs[x] + mkv[x] - _mm3(m_b[x], u_x[x]) for x in every]
    ub = [_mm3(u_x[x], bt_x[x], _TN) for x in every]
    s_new = [[None] * RWKV_HEADS for _ in range(n_seq)]
    for x, (i, h) in enumerate(chains):
        gam = e_up[(i + 1) * c - 1:(i + 1) * c, h * HEAD_DIM:(h + 1) * HEAD_DIM]
        s_new[i][h] = (s_x[x] + vk[x] - ub[x]) * gam
    o = [jnp.concatenate(o_x[i * RWKV_HEADS:(i + 1) * RWKV_HEADS], axis=1) for i in range(n_seq)]
    o = jnp.concatenate(o, axis=0) if n_seq > 1 else o[0]
    mean = _mm_exact_rhs(o, same_head) * (1.0 / HEAD_DIM)
    d = o - mean
    var = _mm_exact_rhs(d * d, same_head) * (1.0 / HEAD_DIM)
    o = (d * lax.rsqrt(var + GN_EPS) * ln_w + ln_b + bonus) * g
    return [o[i * c:(i + 1) * c] for i in range(n_seq)], s_new


def _rwkv_kernel(*refs, rows):
    p_refs = refs[:rows]
    (prev_ref, s0_ref, mu_ref, w0_ref, wup_ref, a0_ref, aup_ref, gup_ref, kk_ref, ka_ref, rk_ref, lnw_ref, lnb_ref,
     o_ref, sT_ref, s_scr, last_scr) = refs[rows:]
    ci = pl.program_id(1)

    @pl.when(ci == 0)
    def _init():
        s_scr[...] = s0_ref[...]
        last_scr[...] = prev_ref[...]

    prm = (mu_ref[...], w0_ref[...], wup_ref[...], a0_ref[...], aup_ref[...], gup_ref[...], kk_ref[...],
           ka_ref[...], rk_ref[...], lnw_ref[...], lnb_ref[...])
    pblks = [p_refs[i][...] for i in range(rows)]
    outs, s_new = _rwkv_chunk(pblks, [last_scr[i] for i in range(rows)],
                              [[s_scr[i, h] for h in range(RWKV_HEADS)] for i in range(rows)], prm)
    c = pblks[0].shape[0]
    for i in range(rows):
        o_ref[i] = outs[i]
        for h in range(RWKV_HEADS):
            s_scr[i, h] = s_new[i][h]
        last_scr[i] = pblks[i][c - 1:c, :]

    @pl.when(ci == pl.num_programs(1) - 1)
    def _fin():
        sT_ref[...] = s_scr[...]


def _rwkv_mixer(proj, row0, n_seq, s_len, prev, wkv0, lp, chunk, rows=RWKV_ROWS):
    assert s_len % chunk == 0 and n_seq % rows == 0 and row0 % chunk == 0
    assert proj.shape[1] == 2 * RWKV_PROJ
    n_chunks = s_len // chunk
    blk0 = row0 // chunk
    row = lambda x: x.reshape(1, -1).astype(F32)
    full = lambda a: pl.BlockSpec(a.shape, lambda b, c: (0,) * a.ndim)
    weights = [row(lp['rwkv_mu']), row(lp['rwkv_w0']), lp['rwkv_w_up'], row(lp['rwkv_a0']), lp['rwkv_a_up'],
               lp['rwkv_g_up'], row(lp['rwkv_k_k']), row(lp['rwkv_k_a']), row(lp['rwkv_r_k']),
               row(lp['rwkv_ln_w']), row(lp['rwkv_ln_b'])]
    p_spec = lambda i: pl.BlockSpec((chunk, RWKV_PROJ), lambda b, c: (blk0 + (b * rows + i) * n_chunks + c, 1))
    out, s_t = pl.pallas_call(
        functools.partial(_rwkv_kernel, rows=rows),
        grid=(n_seq // rows, n_chunks),
        in_specs=[p_spec(i) for i in range(rows)]
                 + [pl.BlockSpec((rows, 1, RWKV_PROJ), lambda b, c: (b, 0, 0)),
                    pl.BlockSpec((rows, RWKV_HEADS, HEAD_DIM, HEAD_DIM), lambda b, c: (b, 0, 0, 0))]
                 + [full(w) for w in weights],
        out_specs=[pl.BlockSpec((rows, chunk, RWKV_WIDTH), lambda b, c: (b, c, 0)),
                   pl.BlockSpec((rows, RWKV_HEADS, HEAD_DIM, HEAD_DIM), lambda b, c: (b, 0, 0, 0))],
        out_shape=[jax.ShapeDtypeStruct((n_seq, s_len, RWKV_WIDTH), F32),
                   jax.ShapeDtypeStruct((n_seq, RWKV_HEADS, HEAD_DIM, HEAD_DIM), F32)],
        scratch_shapes=[pltpu.VMEM((rows, RWKV_HEADS, HEAD_DIM, HEAD_DIM), F32),
                        pltpu.VMEM((rows, 1, RWKV_PROJ), F32)],
        compiler_params=_cparams(("parallel", "arbitrary")),
        name="rwkv_mixer",
    )(*([proj] * rows), prev.reshape(n_seq, 1, RWKV_PROJ), wkv0, *weights)
    return out.reshape(n_seq * s_len, RWKV_WIDTH), s_t


def kernel(x_prompt, x_sample, state_pool, cache_k, cache_v, state_shift, state_wkv, norm1_g, norm2_g, final_g, w_in, w_out, pool_w, pool_scale, rwkv_mu, rwkv_w0, rwkv_w_up, rwkv_a0, rwkv_a_up, rwkv_g_up, rwkv_k_k, rwkv_k_a, rwkv_r_k, rwkv_ln_w, rwkv_ln_b, peer_wq, peer_subkeys, peer_u, peer_v):
    bp, sp_len, d = x_prompt.shape
    bs, ss_len, _ = x_sample.shape
    tp = bp * sp_len
    ts = bs * ss_len
    depth = w_in.shape[0]
    keep = min(ATT_BUF, sp_len)
    h = jnp.concatenate([x_prompt.reshape(tp, d), x_sample.reshape(ts, d)], axis=0)
    pos = jnp.concatenate([jnp.tile(jnp.arange(sp_len, dtype=jnp.int32), bp),
                           jnp.tile(PAST_LEN + jnp.arange(ss_len, dtype=jnp.int32), bs)])
    rope_tabs = _rope_tables(pos)
    zero_buf = jnp.zeros((bp, POOL_BUF, POOL_WIDTH), F32)
    zero_shift = jnp.zeros((bp, RWKV_PROJ), F32)
    zero_wkv = jnp.zeros((bp, RWKV_HEADS, HEAD_DIM, HEAD_DIM), F32)
    k_lo, v_lo, r_lo = POOL_WIDTH + ATT_WIDTH, POOL_WIDTH + 2 * ATT_WIDTH, POOL_WIDTH + 3 * ATT_WIDTH
    heads = lambda x, n: x.reshape(n, -1, ATT_HEADS, HEAD_DIM)
    outs_p = [[] for _ in range(5)]
    outs_s = [[] for _ in range(5)]
    for l in range(depth):
        lp = {'rwkv_mu': rwkv_mu[l], 'rwkv_w0': rwkv_w0[l],
              'rwkv_w_up': rwkv_w_up[l], 'rwkv_a0': rwkv_a0[l], 'rwkv_a_up': rwkv_a_up[l],
              'rwkv_g_up': rwkv_g_up[l], 'rwkv_k_k': rwkv_k_k[l], 'rwkv_k_a': rwkv_k_a[l],
              'rwkv_r_k': rwkv_r_k[l], 'rwkv_ln_w': rwkv_ln_w[l], 'rwkv_ln_b': rwkv_ln_b[l]}
        proj = _linear(h, w_in[l].astype(BF16), g=norm1_g[l], rope=rope_tabs, name="in_proj")
        proj_p = proj[:tp].reshape(bp, sp_len, IN_WIDTH)
        proj_s = proj[tp:].reshape(bs, ss_len, IN_WIDTH)

        pool_p = _pool(proj, 0, bp, sp_len, zero_buf, pool_w[l], pool_scale[l], 0)
        pool_s = _pool(proj, tp, bs, ss_len, state_pool[l], pool_w[l], pool_scale[l], PAST_LEN)
        att_p = _attn_prompt(proj, bp, sp_len)
        att_s, ck_new, cv_new = _attn_sample(proj, tp, bs, ss_len, cache_k[l].reshape(bs * ATT_BUF, ATT_WIDTH),
                                             cache_v[l].reshape(bs * ATT_BUF, ATT_WIDTH))
        rw_p, wkv_p = _rwkv_mixer(proj, 0, bp, sp_len, zero_shift, zero_wkv, lp, min(sp_len, RWKV_CHUNK))
        rw_s, wkv_s = _rwkv_mixer(proj, tp, bs, ss_len, state_shift[l], state_wkv[l], lp, min(ss_len, RWKV_CHUNK))

        outs_p[0].append(proj_p[:, -POOL_BUF:, :POOL_WIDTH])
        outs_p[1].append(heads(proj_p[:, -keep:, k_lo:v_lo], bp))
        outs_p[2].append(heads(proj_p[:, -keep:, v_lo:r_lo], bp))
        outs_p[3].append(proj_p[:, -1, r_lo:])
        outs_p[4].append(wkv_p)
        outs_s[0].append(jnp.concatenate([state_pool[l], proj_s[:, :, :POOL_WIDTH]], axis=1)[:, -POOL_BUF:])
        outs_s[1].append(heads(ck_new, bs))
        outs_s[2].append(heads(cv_new, bs))
        outs_s[3].append(proj_s[:, -1, r_lo:])
        outs_s[4].append(wkv_s)

        mixed = jnp.concatenate([jnp.concatenate([pool_p, att_p, rw_p], axis=1),
                                 jnp.concatenate([pool_s, att_s, rw_s], axis=1)], axis=0)
        h = _linear(mixed, w_out[l].astype(BF16), resid=h, name="out_proj")
        h = _peer_ffn(h, norm2_g[l], peer_wq[l], peer_subkeys[l], peer_u[l], peer_v[l])
    y = _rmsnorm(h, final_g)
    y_prompt = y[:tp].reshape(bp, sp_len, d)
    y_sample = y[tp:].reshape(bs, ss_len, d)
    return (y_prompt, y_sample, *[jnp.stack(a) for a in outs_p], *[jnp.stack(a) for a in outs_s])
```

```python
import functools
import math

import jax
import jax.numpy as jnp
import numpy as np
from jax import lax
from jax.experimental import pallas as pl
from jax.experimental.pallas import tpu as pltpu

F32 = jnp.float32
BF16 = jnp.bfloat16

D_MODEL = 1024
HEAD_DIM = 64
POOL_WINDOWS = (2, 4, 8, 16)
POOL_GROUP = 64
POOL_WIDTH = 256
POOL_BUF = 15
ATT_WIDTH = 384
ATT_HEADS = 6
DILATED_GROUPS = ((128, 1), (512, 4), (2048, 16))
ATT_BUF = 2048
ROPE_DIM = 16
ROPE_THETA = 500000.0
NEG_INF = -1e30
RWKV_WIDTH = 384
RWKV_HEADS = 6
DECAY_LORA = 64
ICLR_LORA = 64
GATE_LORA = 128
RWKV_PROJ = 1408
GN_EPS = HEAD_DIM * 1e-5
N_KEYS = 128
N_EXPERTS = N_KEYS * N_KEYS
PEER_HEADS = 8
PEER_TOPK = 16
RMS_EPS = 1e-6

TOKEN_BLOCK = 256
EXPERT_CHUNK = 2048
G_PITCH = 136
RWKV_CHUNK = 64
RWKV_ROWS = 2
LANES = 128
IN_WIDTH = POOL_WIDTH + 3 * ATT_WIDTH + RWKV_PROJ
ATT_QBLOCK = 128
ATT_WINDOW = ATT_BUF + ATT_QBLOCK
PAST_LEN = 8192
VMEM_LIMIT = 56 * 1024 * 1024


def _cparams(sem):
    return pltpu.CompilerParams(dimension_semantics=sem, vmem_limit_bytes=VMEM_LIMIT)


def _rope_tables(pos):
    half = ROPE_DIM // 2
    inv = ROPE_THETA ** (-jnp.arange(half, dtype=F32) * 2.0 / ROPE_DIM)
    ang = pos.astype(F32)[:, None] * inv[None, :]
    cos, sin = jnp.cos(ang), jnp.sin(ang)
    n = pos.shape[0]
    one = jnp.ones((n, HEAD_DIM - ROPE_DIM), F32)
    zero = jnp.zeros((n, HEAD_DIM - ROPE_DIM), F32)
    zh = jnp.zeros((n, half), F32)
    c = jnp.concatenate([cos, cos, one], axis=1)
    a = jnp.concatenate([-sin, zh, zero], axis=1)
    b = jnp.concatenate([zh, sin, zero], axis=1)
    tile = lambda x: jnp.concatenate([x, x], axis=1)
    return tile(c), tile(a), tile(b)


def _rope_lanes(x, c, a, b):
    half = ROPE_DIM // 2
    return x * c + pltpu.roll(x, LANES - half, 1) * a + pltpu.roll(x, half, 1) * b


def _linear_kernel(*refs, norm, resid, passes, emit_xn, rope):
    it = iter(refs)
    x_ref = next(it)
    g_ref = next(it) if norm else None
    w_ref = next(it)
    wlo_ref = next(it) if passes == 3 else None
    r_ref = next(it) if resid else None
    rope_refs = [next(it) for _ in range(3)] if rope else None
    o_ref = next(it)
    xn_ref = next(it) if emit_xn else None

    x = x_ref[...].astype(F32)
    if norm:
        x = x * lax.rsqrt(jnp.mean(x * x, axis=-1, keepdims=True) + RMS_EPS) * g_ref[...]
    if emit_xn:
        xn_ref[...] = x.astype(xn_ref.dtype)
    x_hi = x.astype(BF16)
    acc = jnp.dot(x_hi, w_ref[...], preferred_element_type=F32)
    if passes == 3:
        x_lo = (x - x_hi.astype(F32)).astype(BF16)
        acc = acc + jnp.dot(x_lo, w_ref[...], preferred_element_type=F32)
        acc = acc + jnp.dot(x_hi, wlo_ref[...], preferred_element_type=F32)
    if resid:
        acc = acc + r_ref[...]
    if rope:
        c, a, b = (r[...] for r in rope_refs)
        lo, hi = POOL_WIDTH, POOL_WIDTH + 2 * ATT_WIDTH
        o_ref[:, :lo] = acc[:, :lo]
        for j in range(lo, hi, LANES):
            o_ref[:, j:j + LANES] = _rope_lanes(acc[:, j:j + LANES], c, a, b)
        o_ref[:, hi:] = acc[:, hi:]
    else:
        o_ref[...] = acc


def _linear(x, w_hi, *, g=None, w_lo=None, resid=None, emit_xn=False, rope=None, tb=TOKEN_BLOCK, name="linear"):
    t, k = x.shape
    n = w_hi.shape[1]
    assert t % tb == 0
    norm = g is not None
    passes = 3 if w_lo is not None else 1
    ins = [x]
    specs = [pl.BlockSpec((tb, k), lambda i: (i, 0))]
    if norm:
        ins.append(g.reshape(1, k).astype(F32))
        specs.append(pl.BlockSpec((1, k), lambda i: (0, 0)))
    ins.append(w_hi)
    specs.append(pl.BlockSpec((k, n), lambda i: (0, 0)))
    if passes == 3:
        ins.append(w_lo)
        specs.append(pl.BlockSpec((k, n), lambda i: (0, 0)))
    if resid is not None:
        ins.append(resid)
        specs.append(pl.BlockSpec((tb, n), lambda i: (i, 0)))
    if rope is not None:
        ins.extend(rope)
        specs.extend([pl.BlockSpec((tb, LANES), lambda i: (i, 0))] * 3)
    out_shape = [jax.ShapeDtypeStruct((t, n), F32)]
    out_specs = [pl.BlockSpec((tb, n), lambda i: (i, 0))]
    if emit_xn:
        out_shape.append(jax.ShapeDtypeStruct((t, k), BF16))
        out_specs.append(pl.BlockSpec((tb, k), lambda i: (i, 0)))
    outs = pl.pallas_call(
        functools.partial(_linear_kernel, norm=norm, resid=resid is not None, passes=passes, emit_xn=emit_xn,
                          rope=rope is not None),
        grid=(t // tb,),
        in_specs=specs,
        out_specs=out_specs,
        out_shape=out_shape,
        compiler_params=_cparams(("parallel",)),
        name=name,
    )(*ins)
    return outs if emit_xn else outs[0]


def _split_bf16(w):
    hi = w.astype(BF16)
    lo = (w - hi.astype(F32)).astype(BF16)
    return hi, lo


def _rmsnorm_kernel(x_ref, g_ref, o_ref):
    x = x_ref[...]
    o_ref[...] = x * lax.rsqrt(jnp.mean(x * x, axis=-1, keepdims=True) + RMS_EPS) * g_ref[...]


def _rmsnorm(x, g, tb=TOKEN_BLOCK):
    t, d = x.shape
    return pl.pallas_call(
        _rmsnorm_kernel,
        grid=(t // tb,),
        in_specs=[pl.BlockSpec((tb, d), lambda i: (i, 0)), pl.BlockSpec((1, d), lambda i: (0, 0))],
        out_specs=pl.BlockSpec((tb, d), lambda i: (i, 0)),
        out_shape=jax.ShapeDtypeStruct((t, d), F32),
        compiler_params=_cparams(("parallel",)),
        name="final_norm",
    )(x, g.reshape(1, d))


def _dot_nt(a, b):
    return lax.dot_general(a, b, (((1,), (1,)), ((), ())), preferred_element_type=F32)


def _oddeven_merge(lo, hi, r):
    step = r * 2
    if step < hi - lo:
        yield from _oddeven_merge(lo, hi, step)
        yield from _oddeven_merge(lo + r, hi, step)
        yield from [(i, i + r) for i in range(lo + r, hi - r, step)]
    else:
        yield (lo, lo + r)


def _oddeven_sort(lo, hi):
    if hi - lo >= 1:
        mid = lo + (hi - lo) // 2
        yield from _oddeven_sort(lo, mid)
        yield from _oddeven_sort(mid + 1, hi)
        yield from _oddeven_merge(lo, hi, 1)


_SORT16 = tuple(_oddeven_sort(0, PEER_TOPK - 1))
_BITONIC16 = tuple((i, i + d) for d in (8, 4, 2, 1) for i in range(PEER_TOPK) if not i & d)
SUBLANES = 8


def _exchange(v, p, i, j):
    keep = v[i] >= v[j]
    v[i], v[j] = jnp.where(keep, v[i], v[j]), jnp.where(keep, v[j], v[i])
    p[i], p[j] = jnp.where(keep, p[i], p[j]), jnp.where(keep, p[j], p[i])


def _top16(v, p, n_real):
    v, p = list(v), list(p)
    for i, j in _SORT16:
        if j < n_real:
            _exchange(v, p, i, j)
    for shift in (4, 2, 1):
        vb = [pltpu.roll(x, shift, 0) for x in v]
        pb = [pltpu.roll(x, shift, 0) for x in p]
        for i in range(PEER_TOPK):
            keep = v[i] >= vb[PEER_TOPK - 1 - i]
            v[i] = jnp.where(keep, v[i], vb[PEER_TOPK - 1 - i])
            p[i] = jnp.where(keep, p[i], pb[PEER_TOPK - 1 - i])
        for i, j in _BITONIC16:
            _exchange(v, p, i, j)
    return v, p


def _rows16(xs):
    sub = lax.broadcasted_iota(jnp.int32, xs[0].shape, 0)
    halves = []
    for base in (0, SUBLANES):
        acc = xs[base]
        for k in range(1, SUBLANES):
            acc = jnp.where(sub == k, xs[base + k], acc)
        halves.append(acc)
    return jnp.concatenate(halves, axis=0)


def _route_kernel(q_ref, sk_ref, i1_ref, i2_ref, gate_ref, tv_ref, ti_ref, e1_ref, e2_ref, gt_ref):
    tb = q_ref.shape[0]
    neg = jnp.float32(-jnp.inf)
    sub_iota = lax.broadcasted_iota(jnp.int32, (SUBLANES, tb), 0).astype(F32)

    def half_topk(hc, carry):
        off = pl.multiple_of(hc * N_KEYS, N_KEYS)
        qh = q_ref[:, pl.ds(off, N_KEYS)]
        sk = sk_ref[hc]
        q_hi = qh.astype(BF16)
        q_lo = (qh - q_hi.astype(F32)).astype(BF16)
        s_hi = sk.astype(BF16)
        s_lo = (sk - s_hi.astype(F32)).astype(BF16)
        sc = _dot_nt(s_hi, q_hi) + _dot_nt(s_lo, q_hi) + _dot_nt(s_hi, q_lo)
        n_slab = N_KEYS // SUBLANES
        vals = [sc[j * SUBLANES:(j + 1) * SUBLANES] for j in range(n_slab)]
        idxs = [sub_iota + float(j * SUBLANES) for j in range(n_slab)]
        vals, idxs = _top16(vals, idxs, n_slab)
        tv_ref[hc] = _rows16(vals)
        ti_ref[hc] = _rows16(idxs)
        return carry

    lax.fori_loop(0, 2 * PEER_HEADS, half_topk, 0)

    row8 = lax.broadcasted_iota(jnp.int32, (SUBLANES, tb), 0)

    def pair_topk(h, carry):
        a = tv_ref[2 * h]
        b = tv_ref[2 * h + 1]
        ia = ti_ref[2 * h] * float(N_KEYS)
        ib = ti_ref[2 * h + 1]
        vs = [a[0:1] + b[0:8], a[0:1] + b[8:16]]
        code = [ia[0:1] + ib[0:8], ia[0:1] + ib[8:16]]
        for i in range(1, 8):
            lim = PEER_TOPK // (i + 1)
            v = a[i:i + 1] + b[0:8]
            vs.append(jnp.where(row8 < lim, v, neg) if lim < 8 else v)
            code.append(ia[i:i + 1] + ib[0:8])
        vs.append(a[8:16] + b[0:1])
        code.append(ia[8:16] + ib[0:1])
        n_real = len(vs)
        pad_v = jnp.full((SUBLANES, tb), neg, F32)
        pad_c = jnp.zeros((SUBLANES, tb), F32)
        vs += [pad_v] * (PEER_TOPK - n_real)
        code += [pad_c] * (PEER_TOPK - n_real)
        tops, code = _top16(vs, code, n_real)
        e = [jnp.exp(t - tops[0]) for t in tops]
        total = e[0]
        for x in e[1:]:
            total = total + x
        first = [jnp.floor(cd * (1.0 / N_KEYS)) for cd in code]
        second = [cd - f * float(N_KEYS) for cd, f in zip(code, first)]
        row = pl.multiple_of(h * PEER_TOPK, PEER_TOPK)
        gt_ref[pl.ds(row, PEER_TOPK), :] = _rows16([x / total for x in e])
        e1_ref[pl.ds(row, PEER_TOPK), :] = _rows16(first)
        e2_ref[pl.ds(row, PEER_TOPK), :] = _rows16(second)
        return carry

    lax.fori_loop(0, PEER_HEADS, pair_topk, 0)
    i1_ref[...] = e1_ref[...].T
    i2_ref[...] = e2_ref[...].T
    gate_ref[...] = gt_ref[...].T


def _route(q, subkeys, tb=TOKEN_BLOCK):
    t = q.shape[0]
    nsel = PEER_HEADS * PEER_TOPK
    sk = subkeys.reshape(2 * PEER_HEADS, N_KEYS, N_KEYS)
    out = jax.ShapeDtypeStruct((t, nsel), F32)
    spec = pl.BlockSpec((tb, nsel), lambda i: (i, 0))
    return pl.pallas_call(
        _route_kernel,
        grid=(t // tb,),
        in_specs=[pl.BlockSpec((tb, q.shape[1]), lambda i: (i, 0)),
                  pl.BlockSpec(sk.shape, lambda i: (0, 0, 0))],
        out_specs=[spec, spec, spec],
        out_shape=[out, out, out],
        scratch_shapes=[pltpu.VMEM((2 * PEER_HEADS, PEER_TOPK, tb), F32),
                        pltpu.VMEM((2 * PEER_HEADS, PEER_TOPK, tb), F32),
                        pltpu.VMEM((nsel, tb), F32),
                        pltpu.VMEM((nsel, tb), F32),
                        pltpu.VMEM((nsel, tb), F32)],
        compiler_params=_cparams(("parallel",)),
        name="peer_route",
    )(q, sk)


def _gelu_exact(x):
    return 0.5 * x * (1.0 + lax.erf(x * (1.0 / math.sqrt(2.0))))


def _peer_kernel(xn_ref, i1_ref, i2_ref, gate_ref, h_ref, u_ref, v_ref, o_ref, gs_ref, w_ref, acc_ref):
    c = pl.program_id(1)
    tb = xn_ref.shape[0]
    ec = u_ref.shape[0]
    a_per_chunk = ec // N_KEYS

    @pl.when(c == 0)
    def _build_gates():
        key_iota = lax.broadcasted_iota(jnp.int32, (N_KEYS, N_KEYS), 0).astype(F32)

        def one_token(t, carry):
            r1 = i1_ref[pl.ds(t, 1), :]
            r2 = i2_ref[pl.ds(t, 1), :]
            gr = gate_ref[pl.ds(t, 1), :]
            a_t = jnp.where(key_iota == r1, gr, 0.0).astype(BF16)
            b_t = jnp.where(key_iota == r2, 1.0, 0.0).astype(BF16)
            row = pl.multiple_of(t * G_PITCH, 8)
            gs_ref[pl.ds(row, N_KEYS), :] = _dot_nt(a_t, b_t)
            return carry

        lax.fori_loop(0, tb, one_token, 0, unroll=16)
        acc_ref[...] = jnp.zeros_like(acc_ref)

    hid = _dot_nt(xn_ref[...], u_ref[...])
    for al in range(a_per_chunk):
        g_a = gs_ref[pl.ds(c * a_per_chunk + al, tb, stride=G_PITCH), :]
        act = _gelu_exact(hid[:, al * N_KEYS:(al + 1) * N_KEYS])
        w_ref[:, al * N_KEYS:(al + 1) * N_KEYS] = (g_a * act).astype(BF16)
    acc_ref[...] += jnp.dot(w_ref[...], v_ref[...], preferred_element_type=F32)

    @pl.when(c == pl.num_programs(1) - 1)
    def _finish():
        o_ref[...] = h_ref[...] + acc_ref[...]


def _peer_experts(xn, i1, i2, gate, h, u, v, tb=TOKEN_BLOCK, ec=EXPERT_CHUNK):
    t, d = h.shape
    n_exp = v.shape[0]
    nsel = i1.shape[1]
    tok = lambda i, c: (i, 0)
    return pl.pallas_call(
        _peer_kernel,
        grid=(t // tb, n_exp // ec),
        in_specs=[pl.BlockSpec((tb, d), tok),
                  pl.BlockSpec((tb, nsel), tok),
                  pl.BlockSpec((tb, nsel), tok),
                  pl.BlockSpec((tb, nsel), tok),
                  pl.BlockSpec((tb, d), tok),
                  pl.BlockSpec((ec, d), lambda i, c: (c, 0)),
                  pl.BlockSpec((ec, d), lambda i, c: (c, 0))],
        out_specs=pl.BlockSpec((tb, d), tok),
        out_shape=jax.ShapeDtypeStruct((t, d), F32),
        scratch_shapes=[pltpu.VMEM((tb * G_PITCH, N_KEYS), F32),
                        pltpu.VMEM((tb, ec), BF16),
                        pltpu.VMEM((tb, d), F32)],
        compiler_params=_cparams(("parallel", "arbitrary")),
        name="peer_experts",
    )(xn, i1, i2, gate, h, u, v)


def _peer_ffn(h, g2, wq, subkeys, u, v):
    wq_hi, wq_lo = _split_bf16(wq)
    q, xn = _linear(h, wq_hi, g=g2, w_lo=wq_lo, emit_xn=True, name="peer_query")
    i1, i2, gate = _route(q, subkeys)
    return _peer_experts(xn, i1, i2, gate, h, u.astype(BF16), v.astype(BF16))


def _band_bias(rows):
    r = np.arange(rows)[:, None]
    c = np.arange(ATT_WINDOW)[None, :]
    delta = ATT_BUF + r - c
    mult = np.zeros(delta.shape, np.float64)
    for window, dil in DILATED_GROUPS:
        mult += (delta >= 0) & (delta <= window) & (delta % dil == 0)
    return jnp.asarray(np.where(mult > 0, np.log(np.maximum(mult, 1.0)), NEG_INF), F32)


def _band_attend(q, kw, vw, bias):
    lane = lax.broadcasted_iota(jnp.int32, q.shape, 1)
    outs = []
    for half in range(2):
        in_half = (lane // HEAD_DIM) == half
        s = _dg(jnp.where(in_half, q, jnp.zeros_like(q)), kw, _NT) + bias
        m = jnp.max(s, axis=1, keepdims=True)
        p = jnp.exp(s - m)
        den = jnp.sum(p, axis=1, keepdims=True)
        outs.append(_dg(p.astype(BF16), vw, _NN) / den)
    return jnp.where((lane // HEAD_DIM) == 0, outs[0], outs[1])


def _attn_prompt_kernel(q_ref, k_ref, v_ref, bias_ref, o_ref, q_scr, k_scr, v_scr):
    s_len = q_ref.shape[0]
    zeros = jnp.zeros((ATT_BUF, LANES), BF16)
    k_scr[0:ATT_BUF, :] = zeros
    v_scr[0:ATT_BUF, :] = zeros
    k_scr[ATT_BUF:, :] = k_ref[...].astype(BF16)
    v_scr[ATT_BUF:, :] = v_ref[...].astype(BF16)
    q_scr[...] = (q_ref[...] * (HEAD_DIM ** -0.5)).astype(BF16)
    col = lax.broadcasted_iota(jnp.int32, (1, ATT_WINDOW), 1)

    def qblock(qi, carry):
        r0 = pl.multiple_of(qi * ATT_QBLOCK, ATT_QBLOCK)
        bias = jnp.where(col >= ATT_BUF - r0, bias_ref[...], NEG_INF)
        o_ref[pl.ds(r0, ATT_QBLOCK), :] = _band_attend(
            q_scr[pl.ds(r0, ATT_QBLOCK), :], k_scr[pl.ds(r0, ATT_WINDOW), :], v_scr[pl.ds(r0, ATT_WINDOW), :], bias)
        return carry

    lax.fori_loop(0, s_len // ATT_QBLOCK, qblock, 0)


_Q_BLOCK0 = POOL_WIDTH // LANES
_K_BLOCK0 = (POOL_WIDTH + ATT_WIDTH) // LANES
_V_BLOCK0 = (POOL_WIDTH + 2 * ATT_WIDTH) // LANES


def _attn_prompt(proj, n_seq, s_len):
    blk = lambda first: pl.BlockSpec((s_len, LANES), lambda b, hp: (b, first + hp))
    bias = _band_bias(ATT_QBLOCK)
    return pl.pallas_call(
        _attn_prompt_kernel,
        grid=(n_seq, ATT_WIDTH // LANES),
        in_specs=[blk(_Q_BLOCK0), blk(_K_BLOCK0), blk(_V_BLOCK0), pl.BlockSpec(bias.shape, lambda b, hp: (0, 0))],
        out_specs=pl.BlockSpec((s_len, LANES), lambda b, hp: (b, hp)),
        out_shape=jax.ShapeDtypeStruct((n_seq * s_len, ATT_WIDTH), F32),
        scratch_shapes=[pltpu.VMEM((s_len, LANES), BF16),
                        pltpu.VMEM((ATT_BUF + s_len, LANES), BF16),
                        pltpu.VMEM((ATT_BUF + s_len, LANES), BF16)],
        compiler_params=_cparams(("parallel", "parallel")),
        name="attn_prompt",
    )(proj, proj, proj, bias)


def _attn_sample_kernel(q_ref, k_ref, v_ref, ck_ref, cv_ref, bias_ref, o_ref, nk_ref, nv_ref, k_scr, v_scr):
    t = q_ref.shape[0]
    n_past = ck_ref.shape[0]
    pad = jnp.zeros((ATT_WINDOW - n_past - t, LANES), BF16)
    k_scr[0:n_past, :] = ck_ref[...].astype(BF16)
    v_scr[0:n_past, :] = cv_ref[...].astype(BF16)
    k_scr[n_past:n_past + t, :] = k_ref[...].astype(BF16)
    v_scr[n_past:n_past + t, :] = v_ref[...].astype(BF16)
    k_scr[n_past + t:, :] = pad
    v_scr[n_past + t:, :] = pad
    q = (q_ref[...] * (HEAD_DIM ** -0.5)).astype(BF16)
    o_ref[...] = _band_attend(q, k_scr[...], v_scr[...], bias_ref[...])
    nk_ref[0:n_past - t, :] = ck_ref[t:n_past, :]
    nv_ref[0:n_past - t, :] = cv_ref[t:n_past, :]
    nk_ref[n_past - t:n_past, :] = k_ref[...]
    nv_ref[n_past - t:n_past, :] = v_ref[...]


def _attn_sample(proj, row0, n_seq, t_len, cache_k, cache_v):
    assert row0 % t_len == 0 and t_len % 8 == 0 and t_len <= ATT_QBLOCK
    blk0 = row0 // t_len
    blk = lambda first: pl.BlockSpec((t_len, LANES), lambda b, hp: (blk0 + b, first + hp))
    cblk = pl.BlockSpec((ATT_BUF, LANES), lambda b, hp: (b, hp))
    bias = _band_bias(t_len)
    return pl.pallas_call(
        _attn_sample_kernel,
        grid=(n_seq, ATT_WIDTH // LANES),
        in_specs=[blk(_Q_BLOCK0), blk(_K_BLOCK0), blk(_V_BLOCK0), cblk, cblk,
                  pl.BlockSpec(bias.shape, lambda b, hp: (0, 0))],
        out_specs=[pl.BlockSpec((t_len, LANES), lambda b, hp: (b, hp)), cblk, cblk],
        out_shape=[jax.ShapeDtypeStruct((n_seq * t_len, ATT_WIDTH), F32),
                   jax.ShapeDtypeStruct(cache_k.shape, F32), jax.ShapeDtypeStruct(cache_v.shape, F32)],
        scratch_shapes=[pltpu.VMEM((ATT_WINDOW, LANES), BF16), pltpu.VMEM((ATT_WINDOW, LANES), BF16)],
        compiler_params=_cparams(("parallel", "parallel")),
        name="attn_sample",
    )(proj, proj, proj, cache_k, cache_v, bias)


def _pool_kernel(x_ref, buf_ref, w_ref, scale_ref, o_ref, *, pos0):
    x = x_ref[...]
    n = x.shape[0]
    xf = jnp.concatenate([buf_ref[...], x], axis=0)
    s2 = xf + pltpu.roll(xf, 1, 0)
    s4 = s2 + pltpu.roll(s2, 2, 0)
    s8 = s4 + pltpu.roll(s4, 4, 0)
    s16 = s8 + pltpu.roll(s8, 8, 0)
    group = lax.broadcasted_iota(jnp.int32, (1, POOL_WIDTH), 1) // POOL_GROUP
    sums = jnp.where(group == 0, s2, jnp.where(group == 1, s4, jnp.where(group == 2, s8, s16)))[POOL_BUF + 1:]
    win = jnp.where(group == 0, 2.0, jnp.where(group == 1, 4.0, jnp.where(group == 2, 8.0, 16.0)))
    seen = (lax.broadcasted_iota(jnp.int32, (n, 1), 0) + (pos0 + 1)).astype(F32)
    diff = sums / jnp.minimum(win, seen) - x
    o_ref[...] = _mm3(diff, w_ref[...]) * scale_ref[...]


def _pool(proj, row0, n_seq, s_len, buf, w, scale, pos0):
    assert POOL_WINDOWS == (2, 4, 8, 16) and row0 % s_len == 0
    blk0 = row0 // s_len
    wbd = jnp.zeros((POOL_WIDTH, POOL_WIDTH), F32)
    for i in range(len(POOL_WINDOWS)):
        wbd = wbd.at[i * POOL_GROUP:(i + 1) * POOL_GROUP, i * POOL_GROUP:(i + 1) * POOL_GROUP].set(w[i])
    buf16 = jnp.pad(buf, ((0, 0), (1, 0), (0, 0))).reshape(n_seq * (POOL_BUF + 1), POOL_WIDTH)
    return pl.pallas_call(
        functools.partial(_pool_kernel, pos0=pos0),
        grid=(n_seq,),
        in_specs=[pl.BlockSpec((s_len, POOL_WIDTH), lambda b: (blk0 + b, 0)),
                  pl.BlockSpec((POOL_BUF + 1, POOL_WIDTH), lambda b: (b, 0)),
                  pl.BlockSpec((POOL_WIDTH, POOL_WIDTH), lambda b: (0, 0)),
                  pl.BlockSpec((1, POOL_WIDTH), lambda b: (0, 0))],
        out_specs=pl.BlockSpec((s_len, POOL_WIDTH), lambda b: (b, 0)),
        out_shape=jax.ShapeDtypeStruct((n_seq * s_len, POOL_WIDTH), F32),
        compiler_params=_cparams(("parallel",)),
        name="pool_mixer",
    )(proj, buf16, wbd, scale.reshape(1, POOL_WIDTH))


def _split(x):
    hi = x.astype(BF16)
    return hi, (x - hi.astype(F32)).astype(BF16)


def _dg(a, b, dims):
    return lax.dot_general(a, b, (dims, ((), ())), preferred_element_type=F32)


_NN = ((1,), (0,))
_NT = ((1,), (1,))
_TN = ((0,), (0,))


def _mm3(a, b, dims=_NN):
    a_hi, a_lo = _split(a)
    b_hi, b_lo = _split(b)
    return _dg(a_hi, b_hi, dims) + (_dg(a_lo, b_hi, dims) + _dg(a_hi, b_lo, dims))


def _split3(x):
    x0 = x.astype(BF16)
    r1 = x - x0.astype(F32)
    x1 = r1.astype(BF16)
    return x0, x1, (r1 - x1.astype(F32)).astype(BF16)


def _mm_exact_rhs(a, b_bf16):
    a0, a1, a2 = _split3(a)
    return _dg(a0, b_bf16, _NN) + (_dg(a1, b_bf16, _NN) + _dg(a2, b_bf16, _NN))


def _mm_exact_lhs(a_bf16, b):
    b0, b1, b2 = _split3(b)
    return _dg(a_bf16, b0, _NN) + (_dg(a_bf16, b1, _NN) + _dg(a_bf16, b2, _NN))


def _rwkv_chunk(pblks, prev_rows, s0, prm):
    (mu, w0, w_up, a0, a_up, g_up, k_k, k_a, r_k, ln_w, ln_b) = prm
    n_seq = len(pblks)
    c = pblks[0].shape[0]
    n = n_seq * c
    pblk = jnp.concatenate(pblks, axis=0) if n_seq > 1 else pblks[0]
    shifted = [jnp.concatenate([prev_rows[i], pblks[i][:-1]], axis=0) if c > 1 else prev_rows[i]
               for i in range(n_seq)]
    shifted = jnp.concatenate(shifted, axis=0) if n_seq > 1 else shifted[0]
    ps = pblk + (shifted - pblk) * mu
    c1, c2, c3 = RWKV_WIDTH, 2 * RWKV_WIDTH, 3 * RWKV_WIDTH
    c4 = c3 + DECAY_LORA
    c5 = c4 + ICLR_LORA
    r, k, v = ps[:, :c1], ps[:, c1:c2], ps[:, c2:c3]
    wd, ad, gd = ps[:, c3:c4], ps[:, c4:c5], ps[:, c5:]

    z = -(w0 + _mm3(jnp.tanh(wd), w_up))
    softplus = jnp.maximum(z, 0.0) + jnp.log(1.0 + jnp.exp(-jnp.abs(z)))
    lw = -jnp.exp(-softplus - 0.5)
    a = jax.nn.sigmoid(a0 + _mm3(ad, a_up))
    g = _mm3(jax.nn.sigmoid(gd), g_up)

    ch_r = lax.broadcasted_iota(jnp.int32, (RWKV_WIDTH, RWKV_WIDTH), 0) // HEAD_DIM
    ch_c = lax.broadcasted_iota(jnp.int32, (RWKV_WIDTH, RWKV_WIDTH), 1) // HEAD_DIM
    same_head = jnp.where(ch_r == ch_c, 1.0, 0.0).astype(BF16)

    kk = k * k_k
    kk = kk / jnp.maximum(jnp.sqrt(_mm_exact_rhs(kk * kk, same_head)), 1e-12)
    k2 = k * (1.0 + (a - 1.0) * k_a)
    bonus = _mm_exact_rhs(r * k2 * r_k, same_head) * v

    n_r = lax.broadcasted_iota(jnp.int32, (n, n), 0)
    n_c = lax.broadcasted_iota(jnp.int32, (n, n), 1)
    run = (n_r // c == n_c // c) & (n_c <= n_r)
    cs = _mm_exact_lhs(jnp.where(run, 1.0, 0.0).astype(BF16), lw)
    e_up = jnp.exp(cs)
    e_dn = jnp.exp(-cs)
    kap = kk * jnp.exp(cs - lw)
    bet = kk * a * e_dn
    kt = k2 * e_dn
    rt = r * e_up

    t_r = lax.broadcasted_iota(jnp.int32, (c, c), 0)
    t_c = lax.broadcasted_iota(jnp.int32, (c, c), 1)
    incl = t_c <= t_r
    strict = t_c < t_r
    eye = jnp.where(t_r == t_c, 1.0, 0.0)
    chains = [(i, h) for i in range(n_seq) for h in range(RWKV_HEADS)]
    cut = lambda x, i, h: x[i * c:(i + 1) * c, h * HEAD_DIM:(h + 1) * HEAD_DIM]
    kp_x = [cut(kap, i, h) for i, h in chains]
    bt_x = [cut(bet, i, h) for i, h in chains]
    kt_x = [cut(kt, i, h) for i, h in chains]
    rt_x = [cut(rt, i, h) for i, h in chains]
    v_x = [cut(v, i, h) for i, h in chains]
    s_x = [s0[i][h] for i, h in chains]
    every = range(len(chains))
    a_b = [jnp.where(strict, _mm3(kp_x[x], bt_x[x], _NT), 0.0) for x in every]
    a_k = [jnp.where(strict, _mm3(kp_x[x], kt_x[x], _NT), 0.0) for x in every]
    m_b = [jnp.where(incl, _mm3(rt_x[x], bt_x[x], _NT), 0.0) for x in every]
    m_k = [jnp.where(incl, _mm3(rt_x[x], kt_x[x], _NT), 0.0) for x in every]
    akv = [_mm3(a_k[x], v_x[x]) for x in every]
    mkv = [_mm3(m_k[x], v_x[x]) for x in every]
    rs = [_mm3(rt_x[x], s_x[x], _NT) for x in every]
    vk = [_mm3(v_x[x], kt_x[x], _TN) for x in every]
    inv = [eye - jnp.where(t_r // 2 == t_c // 2, a_b[x], 0.0) for x in every]
    b = 2
    while b < c:
        pair = (t_r // (2 * b) == t_c // (2 * b)) & ((t_r // b) % 2 == 1) & ((t_c // b) % 2 == 0)
        low = [_mm3(jnp.where(pair, a_b[x], 0.0), inv[x]) for x in every]
        inv = [inv[x] - _mm3(inv[x], low[x]) for x in every]
        b *= 2
    p_x = [_mm3(inv[x], kp_x[x]) for x in every]
    q_x = [_mm3(inv[x], akv[x]) for x in every]
    u_x = [_mm3(p_x[x], s_x[x], _NT) + q_x[x] for x in every]
    o_x = [rs[x] + mkv[x] - _mm3(m_b[x], u_x[x]) for x in every]
    ub = [_mm3(u_x[x], bt_x[x], _TN) for x in every]
    s_new = [[None] * RWKV_HEADS for _ in range(n_seq)]
    for x, (i, h) in enumerate(chains):
        gam = e_up[(i + 1) * c - 1:(i + 1) * c, h * HEAD_DIM:(h + 1) * HEAD_DIM]
        s_new[i][h] = (s_x[x] + vk[x] - ub[x]) * gam
    o = [jnp.concatenate(o_x[i * RWKV_HEADS:(i + 1) * RWKV_HEADS], axis=1) for i in range(n_seq)]
    o = jnp.concatenate(o, axis=0) if n_seq > 1 else o[0]
    mean = _mm_exact_rhs(o, same_head) * (1.0 / HEAD_DIM)
    d = o - mean
    var = _mm_exact_rhs(d * d, same_head) * (1.0 / HEAD_DIM)
    o = (d * lax.rsqrt(var + GN_EPS) * ln_w + ln_b + bonus) * g
    return [o[i * c:(i + 1) * c] for i in range(n_seq)], s_new


def _rwkv_kernel(*refs, rows):
    p_refs = refs[:rows]
    (prev_ref, s0_ref, mu_ref, w0_ref, wup_ref, a0_ref, aup_ref, gup_ref, kk_ref, ka_ref, rk_ref, lnw_ref, lnb_ref,
     o_ref, sT_ref, s_scr, last_scr) = refs[rows:]
    ci = pl.program_id(1)

    @pl.when(ci == 0)
    def _init():
        s_scr[...] = s0_ref[...]
        last_scr[...] = prev_ref[...]

    prm = (mu_ref[...], w0_ref[...], wup_ref[...], a0_ref[...], aup_ref[...], gup_ref[...], kk_ref[...],
           ka_ref[...], rk_ref[...], lnw_ref[...], lnb_ref[...])
    pblks = [p_refs[i][...] for i in range(rows)]
    outs, s_new = _rwkv_chunk(pblks, [last_scr[i] for i in range(rows)],
                              [[s_scr[i, h] for h in range(RWKV_HEADS)] for i in range(rows)], prm)
    c = pblks[0].shape[0]
    for i in range(rows):
        o_ref[i] = outs[i]
        for h in range(RWKV_HEADS):
            s_scr[i, h] = s_new[i][h]
        last_scr[i] = pblks[i][c - 1:c, :]

    @pl.when(ci == pl.num_programs(1) - 1)
    def _fin():
        sT_ref[...] = s_scr[...]


def _rwkv_mixer(proj, row0, n_seq, s_len, prev, wkv0, lp, chunk, rows=RWKV_ROWS):
    assert s_len % chunk == 0 and n_seq % rows == 0 and row0 % chunk == 0
    assert proj.shape[1] == 2 * RWKV_PROJ
    n_chunks = s_len // chunk
    blk0 = row0 // chunk
    row = lambda x: x.reshape(1, -1).astype(F32)
    full = lambda a: pl.BlockSpec(a.shape, lambda b, c: (0,) * a.ndim)
    weights = [row(lp['rwkv_mu']), row(lp['rwkv_w0']), lp['rwkv_w_up'], row(lp['rwkv_a0']), lp['rwkv_a_up'],
               lp['rwkv_g_up'], row(lp['rwkv_k_k']), row(lp['rwkv_k_a']), row(lp['rwkv_r_k']),
               row(lp['rwkv_ln_w']), row(lp['rwkv_ln_b'])]
    p_spec = lambda i: pl.BlockSpec((chunk, RWKV_PROJ), lambda b, c: (blk0 + (b * rows + i) * n_chunks + c, 1))
    out, s_t = pl.pallas_call(
        functools.partial(_rwkv_kernel, rows=rows),
        grid=(n_seq // rows, n_chunks),
        in_specs=[p_spec(i) for i in range(rows)]
                 + [pl.BlockSpec((rows, 1, RWKV_PROJ), lambda b, c: (b, 0, 0)),
                    pl.BlockSpec((rows, RWKV_HEADS, HEAD_DIM, HEAD_DIM), lambda b, c: (b, 0, 0, 0))]
                 + [full(w) for w in weights],
        out_specs=[pl.BlockSpec((rows, chunk, RWKV_WIDTH), lambda b, c: (b, c, 0)),
                   pl.BlockSpec((rows, RWKV_HEADS, HEAD_DIM, HEAD_DIM), lambda b, c: (b, 0, 0, 0))],
        out_shape=[jax.ShapeDtypeStruct((n_seq, s_len, RWKV_WIDTH), F32),
                   jax.ShapeDtypeStruct((n_seq, RWKV_HEADS, HEAD_DIM, HEAD_DIM), F32)],
        scratch_shapes=[pltpu.VMEM((rows, RWKV_HEADS, HEAD_DIM, HEAD_DIM), F32),
                        pltpu.VMEM((rows, 1, RWKV_PROJ), F32)],
        compiler_params=_cparams(("parallel", "arbitrary")),
        name="rwkv_mixer",
    )(*([proj] * rows), prev.reshape(n_seq, 1, RWKV_PROJ), wkv0, *weights)
    return out.reshape(n_seq * s_len, RWKV_WIDTH), s_t


def kernel(x_prompt, x_sample, state_pool, cache_k, cache_v, state_shift, state_wkv, norm1_g, norm2_g, final_g, w_in, w_out, pool_w, pool_scale, rwkv_mu, rwkv_w0, rwkv_w_up, rwkv_a0, rwkv_a_up, rwkv_g_up, rwkv_k_k, rwkv_k_a, rwkv_r_k, rwkv_ln_w, rwkv_ln_b, peer_wq, peer_subkeys, peer_u, peer_v):
    bp, sp_len, d = x_prompt.shape
    bs, ss_len, _ = x_sample.shape
    tp = bp * sp_len
    ts = bs * ss_len
    depth = w_in.shape[0]
    keep = min(ATT_BUF, sp_len)
    h = jnp.concatenate([x_prompt.reshape(tp, d), x_sample.reshape(ts, d)], axis=0)
    pos = jnp.concatenate([jnp.tile(jnp.arange(sp_len, dtype=jnp.int32), bp),
                           jnp.tile(PAST_LEN + jnp.arange(ss_len, dtype=jnp.int32), bs)])
    rope_tabs = _rope_tables(pos)
    zero_buf = jnp.zeros((bp, POOL_BUF, POOL_WIDTH), F32)
    zero_shift = jnp.zeros((bp, RWKV_PROJ), F32)
    zero_wkv = jnp.zeros((bp, RWKV_HEADS, HEAD_DIM, HEAD_DIM), F32)
    k_lo, v_lo, r_lo = POOL_WIDTH + ATT_WIDTH, POOL_WIDTH + 2 * ATT_WIDTH, POOL_WIDTH + 3 * ATT_WIDTH
    heads = lambda x, n: x.reshape(n, -1, ATT_HEADS, HEAD_DIM)
    outs_p = [[] for _ in range(5)]
    outs_s = [[] for _ in range(5)]
    for l in range(depth):
        lp = {'rwkv_mu': rwkv_mu[l], 'rwkv_w0': rwkv_w0[l],
              'rwkv_w_up': rwkv_w_up[l], 'rwkv_a0': rwkv_a0[l], 'rwkv_a_up': rwkv_a_up[l],
              'rwkv_g_up': rwkv_g_up[l], 'rwkv_k_k': rwkv_k_k[l], 'rwkv_k_a': rwkv_k_a[l],
              'rwkv_r_k': rwkv_r_k[l], 'rwkv_ln_w': rwkv_ln_w[l], 'rwkv_ln_b': rwkv_ln_b[l]}
        proj = _linear(h, w_in[l].astype(BF16), g=norm1_g[l], rope=rope_tabs, name="in_proj")
        proj_s = proj[tp:].reshape(bs, ss_len, IN_WIDTH)

        pool_p = _pool(proj, 0, bp, sp_len, zero_buf, pool_w[l], pool_scale[l], 0)
        pool_s = _pool(proj, tp, bs, ss_len, state_pool[l], pool_w[l], pool_scale[l], PAST_LEN)
        att_p = _attn_prompt(proj, bp, sp_len)
        att_s, ck_new, cv_new = _attn_sample(proj, tp, bs, ss_len, cache_k[l].reshape(bs * ATT_BUF, ATT_WIDTH),
                                             cache_v[l].reshape(bs * ATT_BUF, ATT_WIDTH))
        rw_p, wkv_p = _rwkv_mixer(proj, 0, bp, sp_len, zero_shift, zero_wkv, lp, min(sp_len, RWKV_CHUNK))
        rw_s, wkv_s = _rwkv_mixer(proj, tp, bs, ss_len, state_shift[l], state_wkv[l], lp, min(ss_len, RWKV_CHUNK))

        tail = lambda n, lo, hi: jnp.stack([proj[(b + 1) * sp_len - n:(b + 1) * sp_len, lo:hi] for b in range(bp)])
        outs_p[0].append(tail(POOL_BUF, 0, POOL_WIDTH))
        outs_p[1].append(heads(tail(keep, k_lo, v_lo), bp))
        outs_p[2].append(heads(tail(keep, v_lo, r_lo), bp))
        outs_p[3].append(tail(1, r_lo, IN_WIDTH)[:, 0])
        outs_p[4].append(wkv_p)
        outs_s[0].append(jnp.concatenate([state_pool[l], proj_s[:, :, :POOL_WIDTH]], axis=1)[:, -POOL_BUF:])
        outs_s[1].append(heads(ck_new, bs))
        outs_s[2].append(heads(cv_new, bs))
        outs_s[3].append(proj_s[:, -1, r_lo:])
        outs_s[4].append(wkv_s)

        mixed = jnp.concatenate([jnp.concatenate([pool_p, att_p, rw_p], axis=1),
                                 jnp.concatenate([pool_s, att_s, rw_s], axis=1)], axis=0)
        h = _linear(mixed, w_out[l].astype(BF16), resid=h, name="out_proj")
        h = _peer_ffn(h, norm2_g[l], peer_wq[l], peer_subkeys[l], peer_u[l], peer_v[l])
    y = _rmsnorm(h, final_g)
    y_prompt = y[:tp].reshape(bp, sp_len, d)
    y_sample = y[tp:].reshape(bs, ss_len, d)
    return (y_prompt, y_sample, *[jnp.stack(a) for a in outs_p], *[jnp.stack(a) for a in outs_s])
```

```python
import functools
import math

import jax
import jax.numpy as jnp
import numpy as np
from jax import lax
from jax.experimental import pallas as pl
from jax.experimental.pallas import tpu as pltpu

F32 = jnp.float32
BF16 = jnp.bfloat16

D_MODEL = 1024
HEAD_DIM = 64
POOL_WINDOWS = (2, 4, 8, 16)
POOL_GROUP = 64
POOL_WIDTH = 256
POOL_BUF = 15
ATT_WIDTH = 384
ATT_HEADS = 6
DILATED_GROUPS = ((128, 1), (512, 4), (2048, 16))
ATT_BUF = 2048
ROPE_DIM = 16
ROPE_THETA = 500000.0
NEG_INF = -1e30
RWKV_WIDTH = 384
RWKV_HEADS = 6
DECAY_LORA = 64
ICLR_LORA = 64
GATE_LORA = 128
RWKV_PROJ = 1408
GN_EPS = HEAD_DIM * 1e-5
N_KEYS = 128
N_EXPERTS = N_KEYS * N_KEYS
PEER_HEADS = 8
PEER_TOPK = 16
RMS_EPS = 1e-6

TOKEN_BLOCK = 256
EXPERT_CHUNK = 2048
G_PITCH = 136
RWKV_CHUNK = 64
RWKV_ROWS = 2
LANES = 128
IN_WIDTH = POOL_WIDTH + 3 * ATT_WIDTH + RWKV_PROJ
ATT_QBLOCK = 128
ATT_WINDOW = ATT_BUF + ATT_QBLOCK
PAST_LEN = 8192
VMEM_LIMIT = 56 * 1024 * 1024


def _cparams(sem):
    return pltpu.CompilerParams(dimension_semantics=sem, vmem_limit_bytes=VMEM_LIMIT)


def _rope_tables(pos):
    half = ROPE_DIM // 2
    inv = ROPE_THETA ** (-jnp.arange(half, dtype=F32) * 2.0 / ROPE_DIM)
    ang = pos.astype(F32)[:, None] * inv[None, :]
    cos, sin = jnp.cos(ang), jnp.sin(ang)
    n = pos.shape[0]
    one = jnp.ones((n, HEAD_DIM - ROPE_DIM), F32)
    zero = jnp.zeros((n, HEAD_DIM - ROPE_DIM), F32)
    zh = jnp.zeros((n, half), F32)
    c = jnp.concatenate([cos, cos, one], axis=1)
    a = jnp.concatenate([-sin, zh, zero], axis=1)
    b = jnp.concatenate([zh, sin, zero], axis=1)
    tile = lambda x: jnp.concatenate([x, x], axis=1)
    return tile(c), tile(a), tile(b)


def _rope_lanes(x, c, a, b):
    half = ROPE_DIM // 2
    return x * c + pltpu.roll(x, LANES - half, 1) * a + pltpu.roll(x, half, 1) * b


def _linear_kernel(*refs, norm, resid, passes, emit_xn, rope):
    it = iter(refs)
    x_ref = next(it)
    g_ref = next(it) if norm else None
    w_ref = next(it)
    wlo_ref = next(it) if passes == 3 else None
    r_ref = next(it) if resid else None
    rope_refs = [next(it) for _ in range(3)] if rope else None
    o_ref = next(it)
    xn_ref = next(it) if emit_xn else None

    x = x_ref[...].astype(F32)
    if norm:
        x = x * lax.rsqrt(jnp.mean(x * x, axis=-1, keepdims=True) + RMS_EPS) * g_ref[...]
    if emit_xn:
        xn_ref[...] = x.astype(xn_ref.dtype)
    x_hi = x.astype(BF16)
    acc = jnp.dot(x_hi, w_ref[...], preferred_element_type=F32)
    if passes == 3:
        x_lo = (x - x_hi.astype(F32)).astype(BF16)
        acc = acc + jnp.dot(x_lo, w_ref[...], preferred_element_type=F32)
        acc = acc + jnp.dot(x_hi, wlo_ref[...], preferred_element_type=F32)
    if resid:
        acc = acc + r_ref[...]
    if rope:
        c, a, b = (r[...] for r in rope_refs)
        lo, hi = POOL_WIDTH, POOL_WIDTH + 2 * ATT_WIDTH
        o_ref[:, :lo] = acc[:, :lo]
        for j in range(lo, hi, LANES):
            o_ref[:, j:j + LANES] = _rope_lanes(acc[:, j:j + LANES], c, a, b)
        o_ref[:, hi:] = acc[:, hi:]
    else:
        o_ref[...] = acc


def _linear(x, w_hi, *, g=None, w_lo=None, resid=None, emit_xn=False, rope=None, tb=TOKEN_BLOCK, name="linear"):
    t, k = x.shape
    n = w_hi.shape[1]
    assert t % tb == 0
    norm = g is not None
    passes = 3 if w_lo is not None else 1
    ins = [x]
    specs = [pl.BlockSpec((tb, k), lambda i: (i, 0))]
    if norm:
        ins.append(g.reshape(1, k).astype(F32))
        specs.append(pl.BlockSpec((1, k), lambda i: (0, 0)))
    ins.append(w_hi)
    specs.append(pl.BlockSpec((k, n), lambda i: (0, 0)))
    if passes == 3:
        ins.append(w_lo)
        specs.append(pl.BlockSpec((k, n), lambda i: (0, 0)))
    if resid is not None:
        ins.append(resid)
        specs.append(pl.BlockSpec((tb, n), lambda i: (i, 0)))
    if rope is not None:
        ins.extend(rope)
        specs.extend([pl.BlockSpec((tb, LANES), lambda i: (i, 0))] * 3)
    out_shape = [jax.ShapeDtypeStruct((t, n), F32)]
    out_specs = [pl.BlockSpec((tb, n), lambda i: (i, 0))]
    if emit_xn:
        out_shape.append(jax.ShapeDtypeStruct((t, k), BF16))
        out_specs.append(pl.BlockSpec((tb, k), lambda i: (i, 0)))
    outs = pl.pallas_call(
        functools.partial(_linear_kernel, norm=norm, resid=resid is not None, passes=passes, emit_xn=emit_xn,
                          rope=rope is not None),
        grid=(t // tb,),
        in_specs=specs,
        out_specs=out_specs,
        out_shape=out_shape,
        compiler_params=_cparams(("parallel",)),
        name=name,
    )(*ins)
    return outs if emit_xn else outs[0]


def _split_bf16(w):
    hi = w.astype(BF16)
    lo = (w - hi.astype(F32)).astype(BF16)
    return hi, lo


def _dot_nt(a, b):
    return lax.dot_general(a, b, (((1,), (1,)), ((), ())), preferred_element_type=F32)


def _oddeven_merge(lo, hi, r):
    step = r * 2
    if step < hi - lo:
        yield from _oddeven_merge(lo, hi, step)
        yield from _oddeven_merge(lo + r, hi, step)
        yield from [(i, i + r) for i in range(lo + r, hi - r, step)]
    else:
        yield (lo, lo + r)


def _oddeven_sort(lo, hi):
    if hi - lo >= 1:
        mid = lo + (hi - lo) // 2
        yield from _oddeven_sort(lo, mid)
        yield from _oddeven_sort(mid + 1, hi)
        yield from _oddeven_merge(lo, hi, 1)


_SORT16 = tuple(_oddeven_sort(0, PEER_TOPK - 1))
_BITONIC16 = tuple((i, i + d) for d in (8, 4, 2, 1) for i in range(PEER_TOPK) if not i & d)
SUBLANES = 8


def _exchange(v, p, i, j):
    keep = v[i] >= v[j]
    v[i], v[j] = jnp.where(keep, v[i], v[j]), jnp.where(keep, v[j], v[i])
    p[i], p[j] = jnp.where(keep, p[i], p[j]), jnp.where(keep, p[j], p[i])


def _top16(v, p, n_real):
    v, p = list(v), list(p)
    for i, j in _SORT16:
        if j < n_real:
            _exchange(v, p, i, j)
    for shift in (4, 2, 1):
        vb = [pltpu.roll(x, shift, 0) for x in v]
        pb = [pltpu.roll(x, shift, 0) for x in p]
        for i in range(PEER_TOPK):
            keep = v[i] >= vb[PEER_TOPK - 1 - i]
            v[i] = jnp.where(keep, v[i], vb[PEER_TOPK - 1 - i])
            p[i] = jnp.where(keep, p[i], pb[PEER_TOPK - 1 - i])
        for i, j in _BITONIC16:
            _exchange(v, p, i, j)
    return v, p


def _rows16(xs):
    sub = lax.broadcasted_iota(jnp.int32, xs[0].shape, 0)
    halves = []
    for base in (0, SUBLANES):
        acc = xs[base]
        for k in range(1, SUBLANES):
            acc = jnp.where(sub == k, xs[base + k], acc)
        halves.append(acc)
    return jnp.concatenate(halves, axis=0)


def _route_kernel(q_ref, sk_ref, i1_ref, i2_ref, gate_ref, tv_ref, ti_ref, e1_ref, e2_ref, gt_ref):
    tb = q_ref.shape[0]
    neg = jnp.float32(-jnp.inf)
    sub_iota = lax.broadcasted_iota(jnp.int32, (SUBLANES, tb), 0).astype(F32)

    def half_topk(hc, carry):
        off = pl.multiple_of(hc * N_KEYS, N_KEYS)
        qh = q_ref[:, pl.ds(off, N_KEYS)]
        sk = sk_ref[hc]
        q_hi = qh.astype(BF16)
        q_lo = (qh - q_hi.astype(F32)).astype(BF16)
        s_hi = sk.astype(BF16)
        s_lo = (sk - s_hi.astype(F32)).astype(BF16)
        sc = _dot_nt(s_hi, q_hi) + _dot_nt(s_lo, q_hi) + _dot_nt(s_hi, q_lo)
        n_slab = N_KEYS // SUBLANES
        vals = [sc[j * SUBLANES:(j + 1) * SUBLANES] for j in range(n_slab)]
        idxs = [sub_iota + float(j * SUBLANES) for j in range(n_slab)]
        vals, idxs = _top16(vals, idxs, n_slab)
        tv_ref[hc] = _rows16(vals)
        ti_ref[hc] = _rows16(idxs)
        return carry

    lax.fori_loop(0, 2 * PEER_HEADS, half_topk, 0)

    row8 = lax.broadcasted_iota(jnp.int32, (SUBLANES, tb), 0)

    def pair_topk(h, carry):
        a = tv_ref[2 * h]
        b = tv_ref[2 * h + 1]
        ia = ti_ref[2 * h] * float(N_KEYS)
        ib = ti_ref[2 * h + 1]
        vs = [a[0:1] + b[0:8], a[0:1] + b[8:16]]
        code = [ia[0:1] + ib[0:8], ia[0:1] + ib[8:16]]
        for i in range(1, 8):
            lim = PEER_TOPK // (i + 1)
            v = a[i:i + 1] + b[0:8]
            vs.append(jnp.where(row8 < lim, v, neg) if lim < 8 else v)
            code.append(ia[i:i + 1] + ib[0:8])
        vs.append(a[8:16] + b[0:1])
        code.append(ia[8:16] + ib[0:1])
        n_real = len(vs)
        pad_v = jnp.full((SUBLANES, tb), neg, F32)
        pad_c = jnp.zeros((SUBLANES, tb), F32)
        vs += [pad_v] * (PEER_TOPK - n_real)
        code += [pad_c] * (PEER_TOPK - n_real)
        tops, code = _top16(vs, code, n_real)
        e = [jnp.exp(t - tops[0]) for t in tops]
        total = e[0]
        for x in e[1:]:
            total = total + x
        first = [jnp.floor(cd * (1.0 / N_KEYS)) for cd in code]
        second = [cd - f * float(N_KEYS) for cd, f in zip(code, first)]
        row = pl.multiple_of(h * PEER_TOPK, PEER_TOPK)
        gt_ref[pl.ds(row, PEER_TOPK), :] = _rows16([x / total for x in e])
        e1_ref[pl.ds(row, PEER_TOPK), :] = _rows16(first)
        e2_ref[pl.ds(row, PEER_TOPK), :] = _rows16(second)
        return carry

    lax.fori_loop(0, PEER_HEADS, pair_topk, 0)
    i1_ref[...] = e1_ref[...].T
    i2_ref[...] = e2_ref[...].T
    gate_ref[...] = gt_ref[...].T


def _route(q, subkeys, tb=TOKEN_BLOCK):
    t = q.shape[0]
    nsel = PEER_HEADS * PEER_TOPK
    sk = subkeys.reshape(2 * PEER_HEADS, N_KEYS, N_KEYS)
    out = jax.ShapeDtypeStruct((t, nsel), F32)
    spec = pl.BlockSpec((tb, nsel), lambda i: (i, 0))
    return pl.pallas_call(
        _route_kernel,
        grid=(t // tb,),
        in_specs=[pl.BlockSpec((tb, q.shape[1]), lambda i: (i, 0)),
                  pl.BlockSpec(sk.shape, lambda i: (0, 0, 0))],
        out_specs=[spec, spec, spec],
        out_shape=[out, out, out],
        scratch_shapes=[pltpu.VMEM((2 * PEER_HEADS, PEER_TOPK, tb), F32),
                        pltpu.VMEM((2 * PEER_HEADS, PEER_TOPK, tb), F32),
                        pltpu.VMEM((nsel, tb), F32),
                        pltpu.VMEM((nsel, tb), F32),
                        pltpu.VMEM((nsel, tb), F32)],
        compiler_params=_cparams(("parallel",)),
        name="peer_route",
    )(q, sk)


def _gelu_exact(x):
    return 0.5 * x * (1.0 + lax.erf(x * (1.0 / math.sqrt(2.0))))


def _peer_kernel(xn_ref, i1_ref, i2_ref, gate_ref, h_ref, u_ref, v_ref, *rest):
    norm_ref = rest[0] if len(rest) == 5 else None
    o_ref, gs_ref, w_ref, acc_ref = rest[-4:]
    c = pl.program_id(1)
    tb = xn_ref.shape[0]
    ec = u_ref.shape[0]
    a_per_chunk = ec // N_KEYS

    @pl.when(c == 0)
    def _build_gates():
        key_iota = lax.broadcasted_iota(jnp.int32, (N_KEYS, N_KEYS), 0).astype(F32)

        def one_token(t, carry):
            r1 = i1_ref[pl.ds(t, 1), :]
            r2 = i2_ref[pl.ds(t, 1), :]
            gr = gate_ref[pl.ds(t, 1), :]
            a_t = jnp.where(key_iota == r1, gr, 0.0).astype(BF16)
            b_t = jnp.where(key_iota == r2, 1.0, 0.0).astype(BF16)
            row = pl.multiple_of(t * G_PITCH, 8)
            gs_ref[pl.ds(row, N_KEYS), :] = _dot_nt(a_t, b_t)
            return carry

        lax.fori_loop(0, tb, one_token, 0, unroll=32)
        acc_ref[...] = jnp.zeros_like(acc_ref)

    hid = _dot_nt(xn_ref[...], u_ref[...])
    for al in range(a_per_chunk):
        g_a = gs_ref[pl.ds(c * a_per_chunk + al, tb, stride=G_PITCH), :]
        act = _gelu_exact(hid[:, al * N_KEYS:(al + 1) * N_KEYS])
        w_ref[:, al * N_KEYS:(al + 1) * N_KEYS] = (g_a * act).astype(BF16)
    acc_ref[...] += jnp.dot(w_ref[...], v_ref[...], preferred_element_type=F32)

    @pl.when(c == pl.num_programs(1) - 1)
    def _finish():
        y = h_ref[...] + acc_ref[...]
        if norm_ref is not None:
            y = y * lax.rsqrt(jnp.mean(y * y, axis=-1, keepdims=True) + RMS_EPS) * norm_ref[...]
        o_ref[...] = y


def _peer_experts(xn, i1, i2, gate, h, u, v, out_norm_g=None, tb=TOKEN_BLOCK, ec=EXPERT_CHUNK):
    t, d = h.shape
    n_exp = v.shape[0]
    nsel = i1.shape[1]
    tok = lambda i, c: (i, 0)
    extra = [] if out_norm_g is None else [out_norm_g.reshape(1, d)]
    return pl.pallas_call(
        _peer_kernel,
        grid=(t // tb, n_exp // ec),
        in_specs=[pl.BlockSpec((tb, d), tok),
                  pl.BlockSpec((tb, nsel), tok),
                  pl.BlockSpec((tb, nsel), tok),
                  pl.BlockSpec((tb, nsel), tok),
                  pl.BlockSpec((tb, d), tok),
                  pl.BlockSpec((ec, d), lambda i, c: (c, 0)),
                  pl.BlockSpec((ec, d), lambda i, c: (c, 0))]
                 + [pl.BlockSpec((1, d), lambda i, c: (0, 0))] * len(extra),
        out_specs=pl.BlockSpec((tb, d), tok),
        out_shape=jax.ShapeDtypeStruct((t, d), F32),
        scratch_shapes=[pltpu.VMEM((tb * G_PITCH, N_KEYS), F32),
                        pltpu.VMEM((tb, ec), BF16),
                        pltpu.VMEM((tb, d), F32)],
        compiler_params=_cparams(("parallel", "arbitrary")),
        name="peer_experts",
    )(xn, i1, i2, gate, h, u, v, *extra)


def _peer_ffn(h, g2, wq, subkeys, u, v, out_norm_g=None):
    wq_hi, wq_lo = _split_bf16(wq)
    q, xn = _linear(h, wq_hi, g=g2, w_lo=wq_lo, emit_xn=True, name="peer_query")
    i1, i2, gate = _route(q, subkeys)
    return _peer_experts(xn, i1, i2, gate, h, u.astype(BF16), v.astype(BF16), out_norm_g)


def _band_bias(rows):
    r = np.arange(rows)[:, None]
    c = np.arange(ATT_WINDOW)[None, :]
    delta = ATT_BUF + r - c
    mult = np.zeros(delta.shape, np.float64)
    for window, dil in DILATED_GROUPS:
        mult += (delta >= 0) & (delta <= window) & (delta % dil == 0)
    return jnp.asarray(np.where(mult > 0, np.log(np.maximum(mult, 1.0)), NEG_INF), F32)


def _band_attend(q, kw, vw, bias):
    lane = lax.broadcasted_iota(jnp.int32, q.shape, 1)
    scores = [_dg(jnp.where((lane // HEAD_DIM) == half, q, jnp.zeros_like(q)), kw, _NT) for half in range(2)]
    outs = []
    for s in scores:
        s = s + bias
        m = jnp.max(s, axis=1, keepdims=True)
        p = jnp.exp(s - m)
        den = jnp.sum(p, axis=1, keepdims=True)
        outs.append(_dg(p.astype(BF16), vw, _NN) / den)
    return jnp.where((lane // HEAD_DIM) == 0, outs[0], outs[1])


def _attn_prompt_kernel(q_ref, k_ref, v_ref, bias_ref, o_ref, q_scr, k_scr, v_scr):
    s_len = q_ref.shape[0]
    zeros = jnp.zeros((ATT_BUF, LANES), BF16)
    k_scr[0:ATT_BUF, :] = zeros
    v_scr[0:ATT_BUF, :] = zeros
    k_scr[ATT_BUF:, :] = k_ref[...].astype(BF16)
    v_scr[ATT_BUF:, :] = v_ref[...].astype(BF16)
    q_scr[...] = (q_ref[...] * (HEAD_DIM ** -0.5)).astype(BF16)
    col = lax.broadcasted_iota(jnp.int32, (1, ATT_WINDOW), 1)

    def qblock(qi, carry):
        r0 = pl.multiple_of(qi * ATT_QBLOCK, ATT_QBLOCK)
        bias = jnp.where(col >= ATT_BUF - r0, bias_ref[...], NEG_INF)
        o_ref[pl.ds(r0, ATT_QBLOCK), :] = _band_attend(
            q_scr[pl.ds(r0, ATT_QBLOCK), :], k_scr[pl.ds(r0, ATT_WINDOW), :], v_scr[pl.ds(r0, ATT_WINDOW), :], bias)
        return carry

    lax.fori_loop(0, s_len // ATT_QBLOCK, qblock, 0)


_Q_BLOCK0 = POOL_WIDTH // LANES
_K_BLOCK0 = (POOL_WIDTH + ATT_WIDTH) // LANES
_V_BLOCK0 = (POOL_WIDTH + 2 * ATT_WIDTH) // LANES


def _attn_prompt(proj, n_seq, s_len):
    blk = lambda first: pl.BlockSpec((s_len, LANES), lambda b, hp: (b, first + hp))
    bias = _band_bias(ATT_QBLOCK)
    return pl.pallas_call(
        _attn_prompt_kernel,
        grid=(n_seq, ATT_WIDTH // LANES),
        in_specs=[blk(_Q_BLOCK0), blk(_K_BLOCK0), blk(_V_BLOCK0), pl.BlockSpec(bias.shape, lambda b, hp: (0, 0))],
        out_specs=pl.BlockSpec((s_len, LANES), lambda b, hp: (b, hp)),
        out_shape=jax.ShapeDtypeStruct((n_seq * s_len, ATT_WIDTH), F32),
        scratch_shapes=[pltpu.VMEM((s_len, LANES), BF16),
                        pltpu.VMEM((ATT_BUF + s_len, LANES), BF16),
                        pltpu.VMEM((ATT_BUF + s_len, LANES), BF16)],
        compiler_params=_cparams(("parallel", "parallel")),
        name="attn_prompt",
    )(proj, proj, proj, bias)


def _attn_sample_kernel(*refs):
    q_ref, k_ref, v_ref, ck_ref, cv_ref, bias_ref = refs[:6]
    o_ref, nk_ref, nv_ref, k_scr, v_scr = refs[-5:]
    t = q_ref.shape[0]
    n_past = ck_ref.shape[0]
    pad = jnp.zeros((ATT_WINDOW - n_past - t, LANES), BF16)
    k_scr[0:n_past, :] = ck_ref[...].astype(BF16)
    v_scr[0:n_past, :] = cv_ref[...].astype(BF16)
    k_scr[n_past:n_past + t, :] = k_ref[...].astype(BF16)
    v_scr[n_past:n_past + t, :] = v_ref[...].astype(BF16)
    k_scr[n_past + t:, :] = pad
    v_scr[n_past + t:, :] = pad
    q = (q_ref[...] * (HEAD_DIM ** -0.5)).astype(BF16)
    o_ref[...] = _band_attend(q, k_scr[...], v_scr[...], bias_ref[...])
    nk_ref[0, 0:n_past - t, :] = ck_ref[t:n_past, :]
    nv_ref[0, 0:n_past - t, :] = cv_ref[t:n_past, :]
    nk_ref[0, n_past - t:n_past, :] = k_ref[...]
    nv_ref[0, n_past - t:n_past, :] = v_ref[...]


def _attn_sample(proj, row0, n_seq, t_len, cache_k, cache_v, layer, depth, stacks):
    assert row0 % t_len == 0 and t_len % 8 == 0 and t_len <= ATT_QBLOCK
    blk0 = row0 // t_len
    blk = lambda first: pl.BlockSpec((t_len, LANES), lambda b, hp: (blk0 + b, first + hp))
    cblk = pl.BlockSpec((ATT_BUF, LANES), lambda b, hp: (b, hp))
    sblk = pl.BlockSpec((1, ATT_BUF, LANES), lambda b, hp: (layer, b, hp))
    bias = _band_bias(t_len)
    stack_shape = jax.ShapeDtypeStruct((depth,) + cache_k.shape, F32)
    carried = [] if stacks is None else list(stacks)
    return pl.pallas_call(
        _attn_sample_kernel,
        grid=(n_seq, ATT_WIDTH // LANES),
        in_specs=[blk(_Q_BLOCK0), blk(_K_BLOCK0), blk(_V_BLOCK0), cblk, cblk,
                  pl.BlockSpec(bias.shape, lambda b, hp: (0, 0))]
                 + [pl.BlockSpec(memory_space=pl.ANY)] * len(carried),
        out_specs=[pl.BlockSpec((t_len, LANES), lambda b, hp: (b, hp)), sblk, sblk],
        out_shape=[jax.ShapeDtypeStruct((n_seq * t_len, ATT_WIDTH), F32), stack_shape, stack_shape],
        input_output_aliases={6 + j: 1 + j for j in range(len(carried))},
        scratch_shapes=[pltpu.VMEM((ATT_WINDOW, LANES), BF16), pltpu.VMEM((ATT_WINDOW, LANES), BF16)],
        compiler_params=_cparams(("parallel", "parallel")),
        name="attn_sample",
    )(proj, proj, proj, cache_k, cache_v, bias, *carried)


def _pool_kernel(x_ref, buf_ref, w_ref, scale_ref, o_ref, *, pos0):
    x = x_ref[...]
    n = x.shape[0]
    xf = jnp.concatenate([buf_ref[...], x], axis=0)
    s2 = xf + pltpu.roll(xf, 1, 0)
    s4 = s2 + pltpu.roll(s2, 2, 0)
    s8 = s4 + pltpu.roll(s4, 4, 0)
    s16 = s8 + pltpu.roll(s8, 8, 0)
    group = lax.broadcasted_iota(jnp.int32, (1, POOL_WIDTH), 1) // POOL_GROUP
    sums = jnp.where(group == 0, s2, jnp.where(group == 1, s4, jnp.where(group == 2, s8, s16)))[POOL_BUF + 1:]
    win = jnp.where(group == 0, 2.0, jnp.where(group == 1, 4.0, jnp.where(group == 2, 8.0, 16.0)))
    seen = (lax.broadcasted_iota(jnp.int32, (n, 1), 0) + (pos0 + 1)).astype(F32)
    diff = sums / jnp.minimum(win, seen) - x
    o_ref[...] = _mm3(diff, w_ref[...]) * scale_ref[...]


def _pool(proj, row0, n_seq, s_len, buf, w, scale, pos0):
    assert POOL_WINDOWS == (2, 4, 8, 16) and row0 % s_len == 0
    blk0 = row0 // s_len
    wbd = jnp.zeros((POOL_WIDTH, POOL_WIDTH), F32)
    for i in range(len(POOL_WINDOWS)):
        wbd = wbd.at[i * POOL_GROUP:(i + 1) * POOL_GROUP, i * POOL_GROUP:(i + 1) * POOL_GROUP].set(w[i])
    buf16 = jnp.pad(buf, ((0, 0), (1, 0), (0, 0))).reshape(n_seq * (POOL_BUF + 1), POOL_WIDTH)
    return pl.pallas_call(
        functools.partial(_pool_kernel, pos0=pos0),
        grid=(n_seq,),
        in_specs=[pl.BlockSpec((s_len, POOL_WIDTH), lambda b: (blk0 + b, 0)),
                  pl.BlockSpec((POOL_BUF + 1, POOL_WIDTH), lambda b: (b, 0)),
                  pl.BlockSpec((POOL_WIDTH, POOL_WIDTH), lambda b: (0, 0)),
                  pl.BlockSpec((1, POOL_WIDTH), lambda b: (0, 0))],
        out_specs=pl.BlockSpec((s_len, POOL_WIDTH), lambda b: (b, 0)),
        out_shape=jax.ShapeDtypeStruct((n_seq * s_len, POOL_WIDTH), F32),
        compiler_params=_cparams(("parallel",)),
        name="pool_mixer",
    )(proj, buf16, wbd, scale.reshape(1, POOL_WIDTH))


def _split(x):
    hi = x.astype(BF16)
    return hi, (x - hi.astype(F32)).astype(BF16)


def _dg(a, b, dims):
    return lax.dot_general(a, b, (dims, ((), ())), preferred_element_type=F32)


_NN = ((1,), (0,))
_NT = ((1,), (1,))
_TN = ((0,), (0,))


def _mm3(a, b, dims=_NN):
    a_hi, a_lo = _split(a)
    b_hi, b_lo = _split(b)
    return _dg(a_hi, b_hi, dims) + (_dg(a_lo, b_hi, dims) + _dg(a_hi, b_lo, dims))


def _split3(x):
    x0 = x.astype(BF16)
    r1 = x - x0.astype(F32)
    x1 = r1.astype(BF16)
    return x0, x1, (r1 - x1.astype(F32)).astype(BF16)


def _mm_exact_rhs(a, b_bf16):
    a0, a1, a2 = _split3(a)
    return _dg(a0, b_bf16, _NN) + (_dg(a1, b_bf16, _NN) + _dg(a2, b_bf16, _NN))


def _mm_exact_lhs(a_bf16, b):
    b0, b1, b2 = _split3(b)
    return _dg(a_bf16, b0, _NN) + (_dg(a_bf16, b1, _NN) + _dg(a_bf16, b2, _NN))


def _rwkv_chunk(pblks, prev_rows, s0, prm):
    (mu, w0, w_up, a0, a_up, g_up, k_k, k_a, r_k, ln_w, ln_b) = prm
    n_seq = len(pblks)
    c = pblks[0].shape[0]
    n = n_seq * c
    pblk = jnp.concatenate(pblks, axis=0) if n_seq > 1 else pblks[0]
    shifted = [jnp.concatenate([prev_rows[i], pblks[i][:-1]], axis=0) if c > 1 else prev_rows[i]
               for i in range(n_seq)]
    shifted = jnp.concatenate(shifted, axis=0) if n_seq > 1 else shifted[0]
    ps = pblk + (shifted - pblk) * mu
    c1, c2, c3 = RWKV_WIDTH, 2 * RWKV_WIDTH, 3 * RWKV_WIDTH
    c4 = c3 + DECAY_LORA
    c5 = c4 + ICLR_LORA
    r, k, v = ps[:, :c1], ps[:, c1:c2], ps[:, c2:c3]
    wd, ad, gd = ps[:, c3:c4], ps[:, c4:c5], ps[:, c5:]

    z = -(w0 + _mm3(jnp.tanh(wd), w_up))
    softplus = jnp.maximum(z, 0.0) + jnp.log(1.0 + jnp.exp(-jnp.abs(z)))
    lw = -jnp.exp(-softplus - 0.5)
    a = jax.nn.sigmoid(a0 + _mm3(ad, a_up))
    g = _mm3(jax.nn.sigmoid(gd), g_up)

    ch_r = lax.broadcasted_iota(jnp.int32, (RWKV_WIDTH, RWKV_WIDTH), 0) // HEAD_DIM
    ch_c = lax.broadcasted_iota(jnp.int32, (RWKV_WIDTH, RWKV_WIDTH), 1) // HEAD_DIM
    same_head = jnp.where(ch_r == ch_c, 1.0, 0.0).astype(BF16)

    kk = k * k_k
    kk = kk / jnp.maximum(jnp.sqrt(_mm_exact_rhs(kk * kk, same_head)), 1e-12)
    k2 = k * (1.0 + (a - 1.0) * k_a)
    bonus = _mm_exact_rhs(r * k2 * r_k, same_head) * v

    n_r = lax.broadcasted_iota(jnp.int32, (n, n), 0)
    n_c = lax.broadcasted_iota(jnp.int32, (n, n), 1)
    run = (n_r // c == n_c // c) & (n_c <= n_r)
    cs = _mm_exact_lhs(jnp.where(run, 1.0, 0.0).astype(BF16), lw)
    e_up = jnp.exp(cs)
    e_dn = jnp.exp(-cs)
    kap = kk * jnp.exp(cs - lw)
    bet = kk * a * e_dn
    kt = k2 * e_dn
    rt = r * e_up

    t_r = lax.broadcasted_iota(jnp.int32, (c, c), 0)
    t_c = lax.broadcasted_iota(jnp.int32, (c, c), 1)
    incl = t_c <= t_r
    strict = t_c < t_r
    eye = jnp.where(t_r == t_c, 1.0, 0.0)
    chains = [(i, h) for i in range(n_seq) for h in range(RWKV_HEADS)]
    cut = lambda x, i, h: x[i * c:(i + 1) * c, h * HEAD_DIM:(h + 1) * HEAD_DIM]
    kp_x = [cut(kap, i, h) for i, h in chains]
    bt_x = [cut(bet, i, h) for i, h in chains]
    kt_x = [cut(kt, i, h) for i, h in chains]
    rt_x = [cut(rt, i, h) for i, h in chains]
    v_x = [cut(v, i, h) for i, h in chains]
    s_x = [s0[i][h] for i, h in chains]
    every = range(len(chains))
    quad = [_mm3(jnp.concatenate([kp_x[x], rt_x[x]], axis=0), jnp.concatenate([bt_x[x], kt_x[x]], axis=0), _NT)
            for x in every]
    a_b = [jnp.where(strict, quad[x][:c, :c], 0.0) for x in every]
    a_k = [jnp.where(strict, quad[x][:c, c:], 0.0) for x in every]
    m_b = [jnp.where(incl, quad[x][c:, :c], 0.0) for x in every]
    m_k = [jnp.where(incl, quad[x][c:, c:], 0.0) for x in every]
    akv = [_mm3(a_k[x], v_x[x]) for x in every]
    mkv = [_mm3(m_k[x], v_x[x]) for x in every]
    rs = [_mm3(rt_x[x], s_x[x], _NT) for x in every]
    vk = [_mm3(v_x[x], kt_x[x], _TN) for x in every]
    inv = [eye - jnp.where(t_r // 2 == t_c // 2, a_b[x], 0.0) for x in every]
    b = 2
    while b < c:
        pair = (t_r // (2 * b) == t_c // (2 * b)) & ((t_r // b) % 2 == 1) & ((t_c // b) % 2 == 0)
        low = [_mm3(jnp.where(pair, a_b[x], 0.0), inv[x]) for x in every]
        inv = [inv[x] - _mm3(inv[x], low[x]) for x in every]
        b *= 2
    pq = [_mm3(inv[x], jnp.concatenate([kp_x[x], akv[x]], axis=1)) for x in every]
    p_x = [pq[x][:, :HEAD_DIM] for x in every]
    q_x = [pq[x][:, HEAD_DIM:] for x in every]
    u_x = [_mm3(p_x[x], s_x[x], _NT) + q_x[x] for x in every]
    o_x = [rs[x] + mkv[x] - _mm3(m_b[x], u_x[x]) for x in every]
    ub = [_mm3(u_x[x], bt_x[x], _TN) for x in every]
    s_new = [[None] * RWKV_HEADS for _ in range(n_seq)]
    for x, (i, h) in enumerate(chains):
        gam = e_up[(i + 1) * c - 1:(i + 1) * c, h * HEAD_DIM:(h + 1) * HEAD_DIM]
        s_new[i][h] = (s_x[x] + vk[x] - ub[x]) * gam
    o = [jnp.concatenate(o_x[i * RWKV_HEADS:(i + 1) * RWKV_HEADS], axis=1) for i in range(n_seq)]
    o = jnp.concatenate(o, axis=0) if n_seq > 1 else o[0]
    mean = _mm_exact_rhs(o, same_head) * (1.0 / HEAD_DIM)
    d = o - mean
    var = _mm_exact_rhs(d * d, same_head) * (1.0 / HEAD_DIM)
    o = (d * lax.rsqrt(var + GN_EPS) * ln_w + ln_b + bonus) * g
    return [o[i * c:(i + 1) * c] for i in range(n_seq)], s_new


def _rwkv_kernel(*refs, rows):
    p_refs = refs[:rows]
    (prev_ref, s0_ref, mu_ref, w0_ref, wup_ref, a0_ref, aup_ref, gup_ref, kk_ref, ka_ref, rk_ref, lnw_ref, lnb_ref,
     o_ref, sT_ref, s_scr, last_scr) = refs[rows:]
    ci = pl.program_id(1)

    @pl.when(ci == 0)
    def _init():
        s_scr[...] = s0_ref[...]
        last_scr[...] = prev_ref[...]

    prm = (mu_ref[...], w0_ref[...], wup_ref[...], a0_ref[...], aup_ref[...], gup_ref[...], kk_ref[...],
           ka_ref[...], rk_ref[...], lnw_ref[...], lnb_ref[...])
    pblks = [p_refs[i][...] for i in range(rows)]
    outs, s_new = _rwkv_chunk(pblks, [last_scr[i] for i in range(rows)],
                              [[s_scr[i, h] for h in range(RWKV_HEADS)] for i in range(rows)], prm)
    c = pblks[0].shape[0]
    for i in range(rows):
        o_ref[i] = outs[i]
        for h in range(RWKV_HEADS):
            s_scr[i, h] = s_new[i][h]
        last_scr[i] = pblks[i][c - 1:c, :]

    @pl.when(ci == pl.num_programs(1) - 1)
    def _fin():
        sT_ref[...] = s_scr[...]


def _rwkv_mixer(proj, row0, n_seq, s_len, prev, wkv0, lp, chunk, rows=RWKV_ROWS):
    assert s_len % chunk == 0 and n_seq % rows == 0 and row0 % chunk == 0
    assert proj.shape[1] == 2 * RWKV_PROJ
    n_chunks = s_len // chunk
    blk0 = row0 // chunk
    row = lambda x: x.reshape(1, -1).astype(F32)
    full = lambda a: pl.BlockSpec(a.shape, lambda b, c: (0,) * a.ndim)
    weights = [row(lp['rwkv_mu']), row(lp['rwkv_w0']), lp['rwkv_w_up'], row(lp['rwkv_a0']), lp['rwkv_a_up'],
               lp['rwkv_g_up'], row(lp['rwkv_k_k']), row(lp['rwkv_k_a']), row(lp['rwkv_r_k']),
               row(lp['rwkv_ln_w']), row(lp['rwkv_ln_b'])]
    p_spec = lambda i: pl.BlockSpec((chunk, RWKV_PROJ), lambda b, c: (blk0 + (b * rows + i) * n_chunks + c, 1))
    out, s_t = pl.pallas_call(
        functools.partial(_rwkv_kernel, rows=rows),
        grid=(n_seq // rows, n_chunks),
        in_specs=[p_spec(i) for i in range(rows)]
                 + [pl.BlockSpec((rows, 1, RWKV_PROJ), lambda b, c: (b, 0, 0)),
                    pl.BlockSpec((rows, RWKV_HEADS, HEAD_DIM, HEAD_DIM), lambda b, c: (b, 0, 0, 0))]
                 + [full(w) for w in weights],
        out_specs=[pl.BlockSpec((rows, chunk, RWKV_WIDTH), lambda b, c: (b, c, 0)),
                   pl.BlockSpec((rows, RWKV_HEADS, HEAD_DIM, HEAD_DIM), lambda b, c: (b, 0, 0, 0))],
        out_shape=[jax.ShapeDtypeStruct((n_seq, s_len, RWKV_WIDTH), F32),
                   jax.ShapeDtypeStruct((n_seq, RWKV_HEADS, HEAD_DIM, HEAD_DIM), F32)],
        scratch_shapes=[pltpu.VMEM((rows, RWKV_HEADS, HEAD_DIM, HEAD_DIM), F32),
                        pltpu.VMEM((rows, 1, RWKV_PROJ), F32)],
        compiler_params=_cparams(("parallel", "arbitrary")),
        name="rwkv_mixer",
    )(*([proj] * rows), prev.reshape(n_seq, 1, RWKV_PROJ), wkv0, *weights)
    return out.reshape(n_seq * s_len, RWKV_WIDTH), s_t


def kernel(x_prompt, x_sample, state_pool, cache_k, cache_v, state_shift, state_wkv, norm1_g, norm2_g, final_g, w_in, w_out, pool_w, pool_scale, rwkv_mu, rwkv_w0, rwkv_w_up, rwkv_a0, rwkv_a_up, rwkv_g_up, rwkv_k_k, rwkv_k_a, rwkv_r_k, rwkv_ln_w, rwkv_ln_b, peer_wq, peer_subkeys, peer_u, peer_v):
    bp, sp_len, d = x_prompt.shape
    bs, ss_len, _ = x_sample.shape
    tp = bp * sp_len
    ts = bs * ss_len
    depth = w_in.shape[0]
    keep = min(ATT_BUF, sp_len)
    h = jnp.concatenate([x_prompt.reshape(tp, d), x_sample.reshape(ts, d)], axis=0)
    pos = jnp.concatenate([jnp.tile(jnp.arange(sp_len, dtype=jnp.int32), bp),
                           jnp.tile(PAST_LEN + jnp.arange(ss_len, dtype=jnp.int32), bs)])
    rope_tabs = _rope_tables(pos)
    zero_buf = jnp.zeros((bp, POOL_BUF, POOL_WIDTH), F32)
    zero_shift = jnp.zeros((bp, RWKV_PROJ), F32)
    zero_wkv = jnp.zeros((bp, RWKV_HEADS, HEAD_DIM, HEAD_DIM), F32)
    k_lo, v_lo, r_lo = POOL_WIDTH + ATT_WIDTH, POOL_WIDTH + 2 * ATT_WIDTH, POOL_WIDTH + 3 * ATT_WIDTH
    heads = lambda x, n: x.reshape(n, -1, ATT_HEADS, HEAD_DIM)
    outs_p = [[] for _ in range(5)]
    outs_s = [[] for _ in range(5)]
    kv_stacks = None
    for l in range(depth):
        lp = {'rwkv_mu': rwkv_mu[l], 'rwkv_w0': rwkv_w0[l],
              'rwkv_w_up': rwkv_w_up[l], 'rwkv_a0': rwkv_a0[l], 'rwkv_a_up': rwkv_a_up[l],
              'rwkv_g_up': rwkv_g_up[l], 'rwkv_k_k': rwkv_k_k[l], 'rwkv_k_a': rwkv_k_a[l],
              'rwkv_r_k': rwkv_r_k[l], 'rwkv_ln_w': rwkv_ln_w[l], 'rwkv_ln_b': rwkv_ln_b[l]}
        proj = _linear(h, w_in[l].astype(BF16), g=norm1_g[l], rope=rope_tabs, name="in_proj")
        proj_s = proj[tp:].reshape(bs, ss_len, IN_WIDTH)

        pool_p = _pool(proj, 0, bp, sp_len, zero_buf, pool_w[l], pool_scale[l], 0)
        pool_s = _pool(proj, tp, bs, ss_len, state_pool[l], pool_w[l], pool_scale[l], PAST_LEN)
        att_p = _attn_prompt(proj, bp, sp_len)
        att_s, *kv_stacks = _attn_sample(proj, tp, bs, ss_len, cache_k[l].reshape(bs * ATT_BUF, ATT_WIDTH),
                                         cache_v[l].reshape(bs * ATT_BUF, ATT_WIDTH), l, depth, kv_stacks)
        rw_p, wkv_p = _rwkv_mixer(proj, 0, bp, sp_len, zero_shift, zero_wkv, lp, min(sp_len, RWKV_CHUNK))
        rw_s, wkv_s = _rwkv_mixer(proj, tp, bs, ss_len, state_shift[l], state_wkv[l], lp, min(ss_len, RWKV_CHUNK))

        tail = lambda n, lo, hi: jnp.stack([proj[(b + 1) * sp_len - n:(b + 1) * sp_len, lo:hi] for b in range(bp)])
        outs_p[0].append(tail(POOL_BUF, 0, POOL_WIDTH))
        outs_p[1].append(heads(tail(keep, k_lo, v_lo), bp))
        outs_p[2].append(heads(tail(keep, v_lo, r_lo), bp))
        outs_p[3].append(tail(1, r_lo, IN_WIDTH)[:, 0])
        outs_p[4].append(wkv_p)
        outs_s[0].append(jnp.concatenate([state_pool[l], proj_s[:, :, :POOL_WIDTH]], axis=1)[:, -POOL_BUF:])
        outs_s[3].append(proj_s[:, -1, r_lo:])
        outs_s[4].append(wkv_s)

        mixed = jnp.concatenate([jnp.concatenate([pool_p, att_p, rw_p], axis=1),
                                 jnp.concatenate([pool_s, att_s, rw_s], axis=1)], axis=0)
        h = _linear(mixed, w_out[l].astype(BF16), resid=h, name="out_proj")
        h = _peer_ffn(h, norm2_g[l], peer_wq[l], peer_subkeys[l], peer_u[l], peer_v[l],
                      final_g if l == depth - 1 else None)
    y = h
    y_prompt = y[:tp].reshape(bp, sp_len, d)
    y_sample = y[tp:].reshape(bs, ss_len, d)
    s_k, s_v = (x.reshape(depth, bs, ATT_BUF, ATT_HEADS, HEAD_DIM) for x in kv_stacks)
    return (y_prompt, y_sample, *[jnp.stack(a) for a in outs_p],
            jnp.stack(outs_s[0]), s_k, s_v, jnp.stack(outs_s[3]), jnp.stack(outs_s[4]))
```

```python
import functools
import math

import jax
import jax.numpy as jnp
import numpy as np
from jax import lax
from jax.experimental import pallas as pl
from jax.experimental.pallas import tpu as pltpu

F32 = jnp.float32
BF16 = jnp.bfloat16

D_MODEL = 1024
HEAD_DIM = 64
POOL_WINDOWS = (2, 4, 8, 16)
POOL_GROUP = 64
POOL_WIDTH = 256
POOL_BUF = 15
ATT_WIDTH = 384
ATT_HEADS = 6
DILATED_GROUPS = ((128, 1), (512, 4), (2048, 16))
ATT_BUF = 2048
ROPE_DIM = 16
ROPE_THETA = 500000.0
NEG_INF = -1e30
RWKV_WIDTH = 384
RWKV_HEADS = 6
DECAY_LORA = 64
ICLR_LORA = 64
GATE_LORA = 128
RWKV_PROJ = 1408
GN_EPS = HEAD_DIM * 1e-5
N_KEYS = 128
N_EXPERTS = N_KEYS * N_KEYS
PEER_HEADS = 8
PEER_TOPK = 16
RMS_EPS = 1e-6

TOKEN_BLOCK = 256
EXPERT_CHUNK = 4096
G_PAIR = 8
G_PITCH = 72
RWKV_CHUNK = 64
RWKV_ROWS = 2
LANES = 128
IN_WIDTH = POOL_WIDTH + 3 * ATT_WIDTH + RWKV_PROJ
ATT_QBLOCK = 128
ATT_WINDOW = ATT_BUF + ATT_QBLOCK
PAST_LEN = 8192
VMEM_LIMIT = 56 * 1024 * 1024


def _cparams(sem):
    return pltpu.CompilerParams(dimension_semantics=sem, vmem_limit_bytes=VMEM_LIMIT)


def _rope_tables(pos):
    half = ROPE_DIM // 2
    inv = ROPE_THETA ** (-jnp.arange(half, dtype=F32) * 2.0 / ROPE_DIM)
    ang = pos.astype(F32)[:, None] * inv[None, :]
    cos, sin = jnp.cos(ang), jnp.sin(ang)
    n = pos.shape[0]
    one = jnp.ones((n, HEAD_DIM - ROPE_DIM), F32)
    zero = jnp.zeros((n, HEAD_DIM - ROPE_DIM), F32)
    zh = jnp.zeros((n, half), F32)
    c = jnp.concatenate([cos, cos, one], axis=1)
    a = jnp.concatenate([-sin, zh, zero], axis=1)
    b = jnp.concatenate([zh, sin, zero], axis=1)
    tile = lambda x: jnp.concatenate([x, x], axis=1)
    return tile(c), tile(a), tile(b)


def _rope_lanes(x, c, a, b):
    half = ROPE_DIM // 2
    return x * c + pltpu.roll(x, LANES - half, 1) * a + pltpu.roll(x, half, 1) * b


def _linear_kernel(*refs, norm, resid, passes, emit_xn, rope):
    it = iter(refs)
    x_ref = next(it)
    g_ref = next(it) if norm else None
    w_ref = next(it)
    wlo_ref = next(it) if passes == 3 else None
    r_ref = next(it) if resid else None
    rope_refs = [next(it) for _ in range(3)] if rope else None
    o_ref = next(it)
    xn_ref = next(it) if emit_xn else None

    x = x_ref[...].astype(F32)
    if norm:
        x = x * lax.rsqrt(jnp.mean(x * x, axis=-1, keepdims=True) + RMS_EPS) * g_ref[...]
    if emit_xn:
        xn_ref[...] = x.astype(xn_ref.dtype)
    x_hi = x.astype(BF16)
    acc = jnp.dot(x_hi, w_ref[...], preferred_element_type=F32)
    if passes == 3:
        x_lo = (x - x_hi.astype(F32)).astype(BF16)
        acc = acc + jnp.dot(x_lo, w_ref[...], preferred_element_type=F32)
        acc = acc + jnp.dot(x_hi, wlo_ref[...], preferred_element_type=F32)
    if resid:
        acc = acc + r_ref[...]
    if rope:
        c, a, b = (r[...] for r in rope_refs)
        lo, hi = POOL_WIDTH, POOL_WIDTH + 2 * ATT_WIDTH
        o_ref[:, :lo] = acc[:, :lo]
        for j in range(lo, hi, LANES):
            o_ref[:, j:j + LANES] = _rope_lanes(acc[:, j:j + LANES], c, a, b)
        o_ref[:, hi:] = acc[:, hi:]
    else:
        o_ref[...] = acc


def _linear(x, w_hi, *, g=None, w_lo=None, resid=None, emit_xn=False, rope=None, tb=TOKEN_BLOCK, name="linear"):
    t, k = x.shape
    n = w_hi.shape[1]
    assert t % tb == 0
    norm = g is not None
    passes = 3 if w_lo is not None else 1
    ins = [x]
    specs = [pl.BlockSpec((tb, k), lambda i: (i, 0))]
    if norm:
        ins.append(g.reshape(1, k).astype(F32))
        specs.append(pl.BlockSpec((1, k), lambda i: (0, 0)))
    ins.append(w_hi)
    specs.append(pl.BlockSpec((k, n), lambda i: (0, 0)))
    if passes == 3:
        ins.append(w_lo)
        specs.append(pl.BlockSpec((k, n), lambda i: (0, 0)))
    if resid is not None:
        ins.append(resid)
        specs.append(pl.BlockSpec((tb, n), lambda i: (i, 0)))
    if rope is not None:
        ins.extend(rope)
        specs.extend([pl.BlockSpec((tb, LANES), lambda i: (i, 0))] * 3)
    out_shape = [jax.ShapeDtypeStruct((t, n), F32)]
    out_specs = [pl.BlockSpec((tb, n), lambda i: (i, 0))]
    if emit_xn:
        out_shape.append(jax.ShapeDtypeStruct((t, k), BF16))
        out_specs.append(pl.BlockSpec((tb, k), lambda i: (i, 0)))
    outs = pl.pallas_call(
        functools.partial(_linear_kernel, norm=norm, resid=resid is not None, passes=passes, emit_xn=emit_xn,
                          rope=rope is not None),
        grid=(t // tb,),
        in_specs=specs,
        out_specs=out_specs,
        out_shape=out_shape,
        compiler_params=_cparams(("parallel",)),
        name=name,
    )(*ins)
    return outs if emit_xn else outs[0]


def _split_bf16(w):
    hi = w.astype(BF16)
    lo = (w - hi.astype(F32)).astype(BF16)
    return hi, lo


def _dot_nt(a, b):
    return lax.dot_general(a, b, (((1,), (1,)), ((), ())), preferred_element_type=F32)


def _oddeven_merge(lo, hi, r):
    step = r * 2
    if step < hi - lo:
        yield from _oddeven_merge(lo, hi, step)
        yield from _oddeven_merge(lo + r, hi, step)
        yield from [(i, i + r) for i in range(lo + r, hi - r, step)]
    else:
        yield (lo, lo + r)


def _oddeven_sort(lo, hi):
    if hi - lo >= 1:
        mid = lo + (hi - lo) // 2
        yield from _oddeven_sort(lo, mid)
        yield from _oddeven_sort(mid + 1, hi)
        yield from _oddeven_merge(lo, hi, 1)


_SORT16 = tuple(_oddeven_sort(0, PEER_TOPK - 1))
_BITONIC16 = tuple((i, i + d) for d in (8, 4, 2, 1) for i in range(PEER_TOPK) if not i & d)
SUBLANES = 8


def _exchange(v, p, i, j):
    keep = v[i] >= v[j]
    v[i], v[j] = jnp.where(keep, v[i], v[j]), jnp.where(keep, v[j], v[i])
    p[i], p[j] = jnp.where(keep, p[i], p[j]), jnp.where(keep, p[j], p[i])


def _top16(v, p, n_real):
    v, p = list(v), list(p)
    for i, j in _SORT16:
        if j < n_real:
            _exchange(v, p, i, j)
    for shift in (4, 2, 1):
        vb = [pltpu.roll(x, shift, 0) for x in v]
        pb = [pltpu.roll(x, shift, 0) for x in p]
        for i in range(PEER_TOPK):
            keep = v[i] >= vb[PEER_TOPK - 1 - i]
            v[i] = jnp.where(keep, v[i], vb[PEER_TOPK - 1 - i])
            p[i] = jnp.where(keep, p[i], pb[PEER_TOPK - 1 - i])
        for i, j in _BITONIC16:
            _exchange(v, p, i, j)
    return v, p


def _rows16(xs):
    sub = lax.broadcasted_iota(jnp.int32, xs[0].shape, 0)
    halves = []
    for base in (0, SUBLANES):
        acc = xs[base]
        for k in range(1, SUBLANES):
            acc = jnp.where(sub == k, xs[base + k], acc)
        halves.append(acc)
    return jnp.concatenate(halves, axis=0)


def _route_kernel(q_ref, sk_ref, i1_ref, i2_ref, gate_ref, tv_ref, ti_ref, e1_ref, e2_ref, gt_ref):
    tb = q_ref.shape[0]
    neg = jnp.float32(-jnp.inf)
    sub_iota = lax.broadcasted_iota(jnp.int32, (SUBLANES, tb), 0).astype(F32)

    def half_topk(hc, carry):
        off = pl.multiple_of(hc * N_KEYS, N_KEYS)
        qh = q_ref[:, pl.ds(off, N_KEYS)]
        sk = sk_ref[hc]
        q_hi = qh.astype(BF16)
        q_lo = (qh - q_hi.astype(F32)).astype(BF16)
        s_hi = sk.astype(BF16)
        s_lo = (sk - s_hi.astype(F32)).astype(BF16)
        sc = _dot_nt(s_hi, q_hi) + _dot_nt(s_lo, q_hi) + _dot_nt(s_hi, q_lo)
        n_slab = N_KEYS // SUBLANES
        vals = [sc[j * SUBLANES:(j + 1) * SUBLANES] for j in range(n_slab)]
        idxs = [sub_iota + float(j * SUBLANES) for j in range(n_slab)]
        vals, idxs = _top16(vals, idxs, n_slab)
        tv_ref[hc] = _rows16(vals)
        ti_ref[hc] = _rows16(idxs)
        return carry

    lax.fori_loop(0, 2 * PEER_HEADS, half_topk, 0)

    row8 = lax.broadcasted_iota(jnp.int32, (SUBLANES, tb), 0)

    def pair_topk(h, carry):
        a = tv_ref[2 * h]
        b = tv_ref[2 * h + 1]
        ia = ti_ref[2 * h] * float(N_KEYS)
        ib = ti_ref[2 * h + 1]
        vs = [a[0:1] + b[0:8], a[0:1] + b[8:16]]
        code = [ia[0:1] + ib[0:8], ia[0:1] + ib[8:16]]
        for i in range(1, 8):
            lim = PEER_TOPK // (i + 1)
            v = a[i:i + 1] + b[0:8]
            vs.append(jnp.where(row8 < lim, v, neg) if lim < 8 else v)
            code.append(ia[i:i + 1] + ib[0:8])
        vs.append(a[8:16] + b[0:1])
        code.append(ia[8:16] + ib[0:1])
        n_real = len(vs)
        pad_v = jnp.full((SUBLANES, tb), neg, F32)
        pad_c = jnp.zeros((SUBLANES, tb), F32)
        vs += [pad_v] * (PEER_TOPK - n_real)
        code += [pad_c] * (PEER_TOPK - n_real)
        tops, code = _top16(vs, code, n_real)
        e = [jnp.exp(t - tops[0]) for t in tops]
        total = e[0]
        for x in e[1:]:
            total = total + x
        first = [jnp.floor(cd * (1.0 / N_KEYS)) for cd in code]
        second = [cd - f * float(N_KEYS) for cd, f in zip(code, first)]
        row = pl.multiple_of(h * PEER_TOPK, PEER_TOPK)
        gt_ref[pl.ds(row, PEER_TOPK), :] = _rows16([x / total for x in e])
        e1_ref[pl.ds(row, PEER_TOPK), :] = _rows16(first)
        e2_ref[pl.ds(row, PEER_TOPK), :] = _rows16(second)
        return carry

    lax.fori_loop(0, PEER_HEADS, pair_topk, 0)
    i1_ref[...] = e1_ref[...].T
    i2_ref[...] = e2_ref[...].T
    gate_ref[...] = gt_ref[...].T


def _route(q, subkeys, tb=TOKEN_BLOCK):
    t = q.shape[0]
    nsel = PEER_HEADS * PEER_TOPK
    sk = subkeys.reshape(2 * PEER_HEADS, N_KEYS, N_KEYS)
    out = jax.ShapeDtypeStruct((t, nsel), F32)
    spec = pl.BlockSpec((tb, nsel), lambda i: (i, 0))
    return pl.pallas_call(
        _route_kernel,
        grid=(t // tb,),
        in_specs=[pl.BlockSpec((tb, q.shape[1]), lambda i: (i, 0)),
                  pl.BlockSpec(sk.shape, lambda i: (0, 0, 0))],
        out_specs=[spec, spec, spec],
        out_shape=[out, out, out],
        scratch_shapes=[pltpu.VMEM((2 * PEER_HEADS, PEER_TOPK, tb), F32),
                        pltpu.VMEM((2 * PEER_HEADS, PEER_TOPK, tb), F32),
                        pltpu.VMEM((nsel, tb), F32),
                        pltpu.VMEM((nsel, tb), F32),
                        pltpu.VMEM((nsel, tb), F32)],
        compiler_params=_cparams(("parallel",)),
        name="peer_route",
    )(q, sk)


def _gelu_exact(x):
    return 0.5 * x * (1.0 + lax.erf(x * (1.0 / math.sqrt(2.0))))


def _store_gate_tile(i1_ref, i2_ref, gate_ref, t, gs_ref, key_iota):
    r1 = i1_ref[pl.ds(t, 1), :]
    r2 = i2_ref[pl.ds(t, 1), :]
    gr = gate_ref[pl.ds(t, 1), :]
    a_t = jnp.where(key_iota == r1, gr, 0.0).astype(BF16)
    b_t = jnp.where(key_iota == r2, 1.0, 0.0).astype(BF16)
    g = _dot_nt(a_t, b_t)
    groups = range(0, N_KEYS, 2 * G_PAIR)
    hi = jnp.concatenate([g[r:r + G_PAIR] for r in groups], axis=0)
    lo = jnp.concatenate([g[r + G_PAIR:r + 2 * G_PAIR] for r in groups], axis=0)
    bits = lambda x: lax.bitcast_convert_type(x.astype(BF16).astype(F32), jnp.uint32)
    row = pl.multiple_of(t * G_PITCH, SUBLANES)
    gs_ref[pl.ds(row, N_KEYS // 2), :] = bits(hi) | (bits(lo) >> 16)


def _peer_kernel(xn_ref, i1_ref, i2_ref, gate_ref, h_ref, u_ref, v_ref, *rest):
    norm_ref = rest[0] if len(rest) == 5 else None
    o_ref, gs_ref, w_ref, acc_ref = rest[-4:]
    c = pl.program_id(1)
    tb = xn_ref.shape[0]
    ec = u_ref.shape[0]
    a_per_chunk = ec // N_KEYS
    assert a_per_chunk % (2 * G_PAIR) == 0

    @pl.when(c == 0)
    def _build_gates():
        key_iota = lax.broadcasted_iota(jnp.int32, (N_KEYS, N_KEYS), 0).astype(F32)

        def one_token(t, carry):
            _store_gate_tile(i1_ref, i2_ref, gate_ref, t, gs_ref, key_iota)
            return carry

        lax.fori_loop(0, tb, one_token, 0, unroll=32)
        acc_ref[...] = jnp.zeros_like(acc_ref)

    hid = _dot_nt(xn_ref[...], u_ref[...])
    for p in range(a_per_chunk // 2):
        packed = gs_ref[pl.ds(c * (a_per_chunk // 2) + p, tb, stride=G_PITCH), :]
        al = (p // G_PAIR) * 2 * G_PAIR + p % G_PAIR
        for al_h, g_a in ((al, lax.bitcast_convert_type(packed & jnp.uint32(0xFFFF0000), F32)),
                          (al + G_PAIR, lax.bitcast_convert_type(packed << 16, F32))):
            act = _gelu_exact(hid[:, al_h * N_KEYS:(al_h + 1) * N_KEYS])
            w_ref[:, al_h * N_KEYS:(al_h + 1) * N_KEYS] = (g_a * act).astype(BF16)
    acc_ref[...] += jnp.dot(w_ref[...], v_ref[...], preferred_element_type=F32)

    @pl.when(c == pl.num_programs(1) - 1)
    def _finish():
        y = h_ref[...] + acc_ref[...]
        if norm_ref is not None:
            y = y * lax.rsqrt(jnp.mean(y * y, axis=-1, keepdims=True) + RMS_EPS) * norm_ref[...]
        o_ref[...] = y


def _peer_experts(xn, i1, i2, gate, h, u, v, layer, out_norm_g=None, tb=TOKEN_BLOCK, ec=EXPERT_CHUNK):
    t, d = h.shape
    n_exp = N_EXPERTS
    chunk0 = layer * (n_exp // ec)
    nsel = i1.shape[1]
    tok = lambda i, c: (i, 0)
    extra = [] if out_norm_g is None else [out_norm_g.reshape(1, d)]
    return pl.pallas_call(
        _peer_kernel,
        grid=(t // tb, n_exp // ec),
        in_specs=[pl.BlockSpec((tb, d), tok),
                  pl.BlockSpec((tb, nsel), tok),
                  pl.BlockSpec((tb, nsel), tok),
                  pl.BlockSpec((tb, nsel), tok),
                  pl.BlockSpec((tb, d), tok),
                  pl.BlockSpec((ec, d), lambda i, c: (chunk0 + c, 0)),
                  pl.BlockSpec((ec, d), lambda i, c: (chunk0 + c, 0))]
                 + [pl.BlockSpec((1, d), lambda i, c: (0, 0))] * len(extra),
        out_specs=pl.BlockSpec((tb, d), tok),
        out_shape=jax.ShapeDtypeStruct((t, d), F32),
        scratch_shapes=[pltpu.VMEM((tb * G_PITCH, N_KEYS), jnp.uint32),
                        pltpu.VMEM((tb, ec), BF16),
                        pltpu.VMEM((tb, d), F32)],
        compiler_params=_cparams(("parallel", "arbitrary")),
        name="peer_experts",
    )(xn, i1, i2, gate, h, u, v, *extra)


def _peer_ffn(h, g2, wq, subkeys, u, v, layer, out_norm_g=None):
    wq_hi, wq_lo = _split_bf16(wq)
    q, xn = _linear(h, wq_hi, g=g2, w_lo=wq_lo, emit_xn=True, name="peer_query")
    i1, i2, gate = _route(q, subkeys)
    return _peer_experts(xn, i1, i2, gate, h, u, v, layer, out_norm_g)


def _band_bias(rows):
    r = np.arange(rows)[:, None]
    c = np.arange(ATT_WINDOW)[None, :]
    delta = ATT_BUF + r - c
    mult = np.zeros(delta.shape, np.float64)
    for window, dil in DILATED_GROUPS:
        mult += (delta >= 0) & (delta <= window) & (delta % dil == 0)
    return jnp.asarray(np.where(mult > 0, np.log(np.maximum(mult, 1.0)), NEG_INF), F32)


def _band_attend(q, kw, vw, bias):
    lane = lax.broadcasted_iota(jnp.int32, q.shape, 1)
    scores = [_dg(jnp.where((lane // HEAD_DIM) == half, q, jnp.zeros_like(q)), kw, _NT) for half in range(2)]
    outs = []
    for s in scores:
        s = s + bias
        m = jnp.max(s, axis=1, keepdims=True)
        p = jnp.exp(s - m)
        den = jnp.sum(p, axis=1, keepdims=True)
        outs.append(_dg(p.astype(BF16), vw, _NN) / den)
    return jnp.where((lane // HEAD_DIM) == 0, outs[0], outs[1])


def _attn_prompt_kernel(q_ref, k_ref, v_ref, bias_ref, o_ref, q_scr, k_scr, v_scr):
    s_len = q_ref.shape[0]
    zeros = jnp.zeros((ATT_BUF, LANES), BF16)
    k_scr[0:ATT_BUF, :] = zeros
    v_scr[0:ATT_BUF, :] = zeros
    k_scr[ATT_BUF:, :] = k_ref[...].astype(BF16)
    v_scr[ATT_BUF:, :] = v_ref[...].astype(BF16)
    q_scr[...] = (q_ref[...] * (HEAD_DIM ** -0.5)).astype(BF16)
    col = lax.broadcasted_iota(jnp.int32, (1, ATT_WINDOW), 1)

    def qblock(qi, carry):
        r0 = pl.multiple_of(qi * ATT_QBLOCK, ATT_QBLOCK)
        bias = jnp.where(col >= ATT_BUF - r0, bias_ref[...], NEG_INF)
        o_ref[pl.ds(r0, ATT_QBLOCK), :] = _band_attend(
            q_scr[pl.ds(r0, ATT_QBLOCK), :], k_scr[pl.ds(r0, ATT_WINDOW), :], v_scr[pl.ds(r0, ATT_WINDOW), :], bias)
        return carry

    lax.fori_loop(0, s_len // ATT_QBLOCK, qblock, 0)


_Q_BLOCK0 = POOL_WIDTH // LANES
_K_BLOCK0 = (POOL_WIDTH + ATT_WIDTH) // LANES
_V_BLOCK0 = (POOL_WIDTH + 2 * ATT_WIDTH) // LANES


def _attn_prompt(proj, n_seq, s_len):
    blk = lambda first: pl.BlockSpec((s_len, LANES), lambda b, hp: (b, first + hp))
    bias = _band_bias(ATT_QBLOCK)
    return pl.pallas_call(
        _attn_prompt_kernel,
        grid=(n_seq, ATT_WIDTH // LANES),
        in_specs=[blk(_Q_BLOCK0), blk(_K_BLOCK0), blk(_V_BLOCK0), pl.BlockSpec(bias.shape, lambda b, hp: (0, 0))],
        out_specs=pl.BlockSpec((s_len, LANES), lambda b, hp: (b, hp)),
        out_shape=jax.ShapeDtypeStruct((n_seq * s_len, ATT_WIDTH), F32),
        scratch_shapes=[pltpu.VMEM((s_len, LANES), BF16),
                        pltpu.VMEM((ATT_BUF + s_len, LANES), BF16),
                        pltpu.VMEM((ATT_BUF + s_len, LANES), BF16)],
        compiler_params=_cparams(("parallel", "parallel")),
        name="attn_prompt",
    )(proj, proj, proj, bias)


def _attn_sample_kernel(*refs):
    q_ref, k_ref, v_ref, ck_ref, cv_ref, bias_ref = refs[:6]
    o_ref, nk_ref, nv_ref, k_scr, v_scr = refs[-5:]
    t = q_ref.shape[0]
    n_past = ck_ref.shape[0]
    pad = jnp.zeros((ATT_WINDOW - n_past - t, LANES), BF16)
    k_scr[0:n_past, :] = ck_ref[...].astype(BF16)
    v_scr[0:n_past, :] = cv_ref[...].astype(BF16)
    k_scr[n_past:n_past + t, :] = k_ref[...].astype(BF16)
    v_scr[n_past:n_past + t, :] = v_ref[...].astype(BF16)
    k_scr[n_past + t:, :] = pad
    v_scr[n_past + t:, :] = pad
    q = (q_ref[...] * (HEAD_DIM ** -0.5)).astype(BF16)
    o_ref[...] = _band_attend(q, k_scr[...], v_scr[...], bias_ref[...])
    nk_ref[0, 0:n_past - t, :] = ck_ref[t:n_past, :]
    nv_ref[0, 0:n_past - t, :] = cv_ref[t:n_past, :]
    nk_ref[0, n_past - t:n_past, :] = k_ref[...]
    nv_ref[0, n_past - t:n_past, :] = v_ref[...]


def _attn_sample(proj, row0, n_seq, t_len, cache_k, cache_v, layer, depth, stacks):
    assert row0 % t_len == 0 and t_len % 8 == 0 and t_len <= ATT_QBLOCK
    blk0 = row0 // t_len
    blk = lambda first: pl.BlockSpec((t_len, LANES), lambda b, hp: (blk0 + b, first + hp))
    cblk = pl.BlockSpec((ATT_BUF, LANES), lambda b, hp: (layer * n_seq + b, hp))
    sblk = pl.BlockSpec((1, ATT_BUF, LANES), lambda b, hp: (layer, b, hp))
    bias = _band_bias(t_len)
    stack_shape = jax.ShapeDtypeStruct((depth, n_seq * ATT_BUF, ATT_WIDTH), F32)
    carried = [] if stacks is None else list(stacks)
    return pl.pallas_call(
        _attn_sample_kernel,
        grid=(n_seq, ATT_WIDTH // LANES),
        in_specs=[blk(_Q_BLOCK0), blk(_K_BLOCK0), blk(_V_BLOCK0), cblk, cblk,
                  pl.BlockSpec(bias.shape, lambda b, hp: (0, 0))]
                 + [pl.BlockSpec(memory_space=pl.ANY)] * len(carried),
        out_specs=[pl.BlockSpec((t_len, LANES), lambda b, hp: (b, hp)), sblk, sblk],
        out_shape=[jax.ShapeDtypeStruct((n_seq * t_len, ATT_WIDTH), F32), stack_shape, stack_shape],
        input_output_aliases={6 + j: 1 + j for j in range(len(carried))},
        scratch_shapes=[pltpu.VMEM((ATT_WINDOW, LANES), BF16), pltpu.VMEM((ATT_WINDOW, LANES), BF16)],
        compiler_params=_cparams(("parallel", "parallel")),
        name="attn_sample",
    )(proj, proj, proj, cache_k, cache_v, bias, *carried)


def _pool_kernel(x_ref, buf_ref, w_ref, scale_ref, o_ref, *, pos0):
    x = x_ref[...]
    n = x.shape[0]
    xf = jnp.concatenate([buf_ref[...], x], axis=0)
    s2 = xf + pltpu.roll(xf, 1, 0)
    s4 = s2 + pltpu.roll(s2, 2, 0)
    s8 = s4 + pltpu.roll(s4, 4, 0)
    s16 = s8 + pltpu.roll(s8, 8, 0)
    group = lax.broadcasted_iota(jnp.int32, (1, POOL_WIDTH), 1) // POOL_GROUP
    sums = jnp.where(group == 0, s2, jnp.where(group == 1, s4, jnp.where(group == 2, s8, s16)))[POOL_BUF + 1:]
    win = jnp.where(group == 0, 2.0, jnp.where(group == 1, 4.0, jnp.where(group == 2, 8.0, 16.0)))
    seen = (lax.broadcasted_iota(jnp.int32, (n, 1), 0) + (pos0 + 1)).astype(F32)
    diff = sums / jnp.minimum(win, seen) - x
    o_ref[...] = _mm3(diff, w_ref[...]) * scale_ref[...]


def _pool(proj, row0, n_seq, s_len, buf, w, scale, pos0):
    assert POOL_WINDOWS == (2, 4, 8, 16) and row0 % s_len == 0
    blk0 = row0 // s_len
    wbd = jnp.zeros((POOL_WIDTH, POOL_WIDTH), F32)
    for i in range(len(POOL_WINDOWS)):
        wbd = wbd.at[i * POOL_GROUP:(i + 1) * POOL_GROUP, i * POOL_GROUP:(i + 1) * POOL_GROUP].set(w[i])
    buf16 = jnp.pad(buf, ((0, 0), (1, 0), (0, 0))).reshape(n_seq * (POOL_BUF + 1), POOL_WIDTH)
    return pl.pallas_call(
        functools.partial(_pool_kernel, pos0=pos0),
        grid=(n_seq,),
        in_specs=[pl.BlockSpec((s_len, POOL_WIDTH), lambda b: (blk0 + b, 0)),
                  pl.BlockSpec((POOL_BUF + 1, POOL_WIDTH), lambda b: (b, 0)),
                  pl.BlockSpec((POOL_WIDTH, POOL_WIDTH), lambda b: (0, 0)),
                  pl.BlockSpec((1, POOL_WIDTH), lambda b: (0, 0))],
        out_specs=pl.BlockSpec((s_len, POOL_WIDTH), lambda b: (b, 0)),
        out_shape=jax.ShapeDtypeStruct((n_seq * s_len, POOL_WIDTH), F32),
        compiler_params=_cparams(("parallel",)),
        name="pool_mixer",
    )(proj, buf16, wbd, scale.reshape(1, POOL_WIDTH))


def _split(x):
    hi = x.astype(BF16)
    return hi, (x - hi.astype(F32)).astype(BF16)


def _dg(a, b, dims):
    return lax.dot_general(a, b, (dims, ((), ())), preferred_element_type=F32)


_NN = ((1,), (0,))
_NT = ((1,), (1,))
_TN = ((0,), (0,))


def _mm3(a, b, dims=_NN):
    a_hi, a_lo = _split(a)
    b_hi, b_lo = _split(b)
    return _dg(a_hi, b_hi, dims) + (_dg(a_lo, b_hi, dims) + _dg(a_hi, b_lo, dims))


def _split3(x):
    x0 = x.astype(BF16)
    r1 = x - x0.astype(F32)
    x1 = r1.astype(BF16)
    return x0, x1, (r1 - x1.astype(F32)).astype(BF16)


def _mm_exact_rhs(a, b_bf16):
    a0, a1, a2 = _split3(a)
    return _dg(a0, b_bf16, _NN) + (_dg(a1, b_bf16, _NN) + _dg(a2, b_bf16, _NN))


def _mm_exact_lhs(a_bf16, b):
    b0, b1, b2 = _split3(b)
    return _dg(a_bf16, b0, _NN) + (_dg(a_bf16, b1, _NN) + _dg(a_bf16, b2, _NN))


def _rwkv_chunk(pblks, prev_rows, s0, prm):
    (mu, w0, w_up, a0, a_up, g_up, k_k, k_a, r_k, ln_w, ln_b) = prm
    n_seq = len(pblks)
    c = pblks[0].shape[0]
    n = n_seq * c
    pblk = jnp.concatenate(pblks, axis=0) if n_seq > 1 else pblks[0]
    shifted = [jnp.concatenate([prev_rows[i], pblks[i][:-1]], axis=0) if c > 1 else prev_rows[i]
               for i in range(n_seq)]
    shifted = jnp.concatenate(shifted, axis=0) if n_seq > 1 else shifted[0]
    ps = pblk + (shifted - pblk) * mu
    c1, c2, c3 = RWKV_WIDTH, 2 * RWKV_WIDTH, 3 * RWKV_WIDTH
    c4 = c3 + DECAY_LORA
    c5 = c4 + ICLR_LORA
    r, k, v = ps[:, :c1], ps[:, c1:c2], ps[:, c2:c3]
    wd, ad, gd = ps[:, c3:c4], ps[:, c4:c5], ps[:, c5:]

    z = -(w0 + _mm3(jnp.tanh(wd), w_up))
    softplus = jnp.maximum(z, 0.0) + jnp.log(1.0 + jnp.exp(-jnp.abs(z)))
    lw = -jnp.exp(-softplus - 0.5)
    a = jax.nn.sigmoid(a0 + _mm3(ad, a_up))
    g = _mm3(jax.nn.sigmoid(gd), g_up)

    ch_r = lax.broadcasted_iota(jnp.int32, (RWKV_WIDTH, RWKV_WIDTH), 0) // HEAD_DIM
    ch_c = lax.broadcasted_iota(jnp.int32, (RWKV_WIDTH, RWKV_WIDTH), 1) // HEAD_DIM
    same_head = jnp.where(ch_r == ch_c, 1.0, 0.0).astype(BF16)

    kk = k * k_k
    kk = kk / jnp.maximum(jnp.sqrt(_mm_exact_rhs(kk * kk, same_head)), 1e-12)
    k2 = k * (1.0 + (a - 1.0) * k_a)
    bonus = _mm_exact_rhs(r * k2 * r_k, same_head) * v

    n_r = lax.broadcasted_iota(jnp.int32, (n, n), 0)
    n_c = lax.broadcasted_iota(jnp.int32, (n, n), 1)
    run = (n_r // c == n_c // c) & (n_c <= n_r)
    cs = _mm_exact_lhs(jnp.where(run, 1.0, 0.0).astype(BF16), lw)
    e_up = jnp.exp(cs)
    e_dn = jnp.exp(-cs)
    kap = kk * jnp.exp(cs - lw)
    bet = kk * a * e_dn
    kt = k2 * e_dn
    rt = r * e_up

    t_r = lax.broadcasted_iota(jnp.int32, (c, c), 0)
    t_c = lax.broadcasted_iota(jnp.int32, (c, c), 1)
    incl = t_c <= t_r
    strict = t_c < t_r
    eye = jnp.where(t_r == t_c, 1.0, 0.0)
    chains = [(i, h) for i in range(n_seq) for h in range(RWKV_HEADS)]
    cut = lambda x, i, h: x[i * c:(i + 1) * c, h * HEAD_DIM:(h + 1) * HEAD_DIM]
    kp_x = [cut(kap, i, h) for i, h in chains]
    bt_x = [cut(bet, i, h) for i, h in chains]
    kt_x = [cut(kt, i, h) for i, h in chains]
    rt_x = [cut(rt, i, h) for i, h in chains]
    v_x = [cut(v, i, h) for i, h in chains]
    s_x = [s0[i][h] for i, h in chains]
    every = range(len(chains))
    quad = [_mm3(jnp.concatenate([kp_x[x], rt_x[x]], axis=0), jnp.concatenate([bt_x[x], kt_x[x]], axis=0), _NT)
            for x in every]
    a_b = [jnp.where(strict, quad[x][:c, :c], 0.0) for x in every]
    a_k = [jnp.where(strict, quad[x][:c, c:], 0.0) for x in every]
    m_b = [jnp.where(incl, quad[x][c:, :c], 0.0) for x in every]
    m_k = [jnp.where(incl, quad[x][c:, c:], 0.0) for x in every]
    akv = [_mm3(a_k[x], v_x[x]) for x in every]
    mkv = [_mm3(m_k[x], v_x[x]) for x in every]
    rs = [_mm3(rt_x[x], s_x[x], _NT) for x in every]
    vk = [_mm3(v_x[x], kt_x[x], _TN) for x in every]
    inv = [eye - jnp.where(t_r // 2 == t_c // 2, a_b[x], 0.0) for x in every]
    b = 2
    while b < c:
        pair = (t_r // (2 * b) == t_c // (2 * b)) & ((t_r // b) % 2 == 1) & ((t_c // b) % 2 == 0)
        low = [_mm3(jnp.where(pair, a_b[x], 0.0), inv[x]) for x in every]
        inv = [inv[x] - _mm3(inv[x], low[x]) for x in every]
        b *= 2
    pq = [_mm3(inv[x], jnp.concatenate([kp_x[x], akv[x]], axis=1)) for x in every]
    p_x = [pq[x][:, :HEAD_DIM] for x in every]
    q_x = [pq[x][:, HEAD_DIM:] for x in every]
    u_x = [_mm3(p_x[x], s_x[x], _NT) + q_x[x] for x in every]
    o_x = [rs[x] + mkv[x] - _mm3(m_b[x], u_x[x]) for x in every]
    ub = [_mm3(u_x[x], bt_x[x], _TN) for x in every]
    s_new = [[None] * RWKV_HEADS for _ in range(n_seq)]
    for x, (i, h) in enumerate(chains):
        gam = e_up[(i + 1) * c - 1:(i + 1) * c, h * HEAD_DIM:(h + 1) * HEAD_DIM]
        s_new[i][h] = (s_x[x] + vk[x] - ub[x]) * gam
    o = [jnp.concatenate(o_x[i * RWKV_HEADS:(i + 1) * RWKV_HEADS], axis=1) for i in range(n_seq)]
    o = jnp.concatenate(o, axis=0) if n_seq > 1 else o[0]
    mean = _mm_exact_rhs(o, same_head) * (1.0 / HEAD_DIM)
    d = o - mean
    var = _mm_exact_rhs(d * d, same_head) * (1.0 / HEAD_DIM)
    o = (d * lax.rsqrt(var + GN_EPS) * ln_w + ln_b + bonus) * g
    return [o[i * c:(i + 1) * c] for i in range(n_seq)], s_new


def _rwkv_kernel(*refs, rows):
    p_refs = refs[:rows]
    (prev_ref, s0_ref, mu_ref, w0_ref, wup_ref, a0_ref, aup_ref, gup_ref, kk_ref, ka_ref, rk_ref, lnw_ref, lnb_ref,
     o_ref, sT_ref, s_scr, last_scr) = refs[rows:]
    ci = pl.program_id(1)

    @pl.when(ci == 0)
    def _init():
        s_scr[...] = s0_ref[...]
        last_scr[...] = prev_ref[...]

    prm = (mu_ref[...], w0_ref[...], wup_ref[...], a0_ref[...], aup_ref[...], gup_ref[...], kk_ref[...],
           ka_ref[...], rk_ref[...], lnw_ref[...], lnb_ref[...])
    pblks = [p_refs[i][...] for i in range(rows)]
    outs, s_new = _rwkv_chunk(pblks, [last_scr[i] for i in range(rows)],
                              [[s_scr[i, h] for h in range(RWKV_HEADS)] for i in range(rows)], prm)
    c = pblks[0].shape[0]
    for i in range(rows):
        o_ref[i] = outs[i]
        for h in range(RWKV_HEADS):
            s_scr[i, h] = s_new[i][h]
        last_scr[i] = pblks[i][c - 1:c, :]

    @pl.when(ci == pl.num_programs(1) - 1)
    def _fin():
        sT_ref[...] = s_scr[...]


def _rwkv_mixer(proj, row0, n_seq, s_len, prev, wkv0, lp, chunk, rows=RWKV_ROWS):
    assert s_len % chunk == 0 and n_seq % rows == 0 and row0 % chunk == 0
    assert proj.shape[1] == 2 * RWKV_PROJ
    n_chunks = s_len // chunk
    blk0 = row0 // chunk
    row = lambda x: x.reshape(1, -1).astype(F32)
    full = lambda a: pl.BlockSpec(a.shape, lambda b, c: (0,) * a.ndim)
    weights = [row(lp['rwkv_mu']), row(lp['rwkv_w0']), lp['rwkv_w_up'], row(lp['rwkv_a0']), lp['rwkv_a_up'],
               lp['rwkv_g_up'], row(lp['rwkv_k_k']), row(lp['rwkv_k_a']), row(lp['rwkv_r_k']),
               row(lp['rwkv_ln_w']), row(lp['rwkv_ln_b'])]
    p_spec = lambda i: pl.BlockSpec((chunk, RWKV_PROJ), lambda b, c: (blk0 + (b * rows + i) * n_chunks + c, 1))
    out, s_t = pl.pallas_call(
        functools.partial(_rwkv_kernel, rows=rows),
        grid=(n_seq // rows, n_chunks),
        in_specs=[p_spec(i) for i in range(rows)]
                 + [pl.BlockSpec((rows, 1, RWKV_PROJ), lambda b, c: (b, 0, 0)),
                    pl.BlockSpec((rows, RWKV_HEADS, HEAD_DIM, HEAD_DIM), lambda b, c: (b, 0, 0, 0))]
                 + [full(w) for w in weights],
        out_specs=[pl.BlockSpec((rows, chunk, RWKV_WIDTH), lambda b, c: (b, c, 0)),
                   pl.BlockSpec((rows, RWKV_HEADS, HEAD_DIM, HEAD_DIM), lambda b, c: (b, 0, 0, 0))],
        out_shape=[jax.ShapeDtypeStruct((n_seq, s_len, RWKV_WIDTH), F32),
                   jax.ShapeDtypeStruct((n_seq, RWKV_HEADS, HEAD_DIM, HEAD_DIM), F32)],
        scratch_shapes=[pltpu.VMEM((rows, RWKV_HEADS, HEAD_DIM, HEAD_DIM), F32),
                        pltpu.VMEM((rows, 1, RWKV_PROJ), F32)],
        compiler_params=_cparams(("parallel", "arbitrary")),
        name="rwkv_mixer",
    )(*([proj] * rows), prev.reshape(n_seq, 1, RWKV_PROJ), wkv0, *weights)
    return out.reshape(n_seq * s_len, RWKV_WIDTH), s_t


def kernel(x_prompt, x_sample, state_pool, cache_k, cache_v, state_shift, state_wkv, norm1_g, norm2_g, final_g, w_in, w_out, pool_w, pool_scale, rwkv_mu, rwkv_w0, rwkv_w_up, rwkv_a0, rwkv_a_up, rwkv_g_up, rwkv_k_k, rwkv_k_a, rwkv_r_k, rwkv_ln_w, rwkv_ln_b, peer_wq, peer_subkeys, peer_u, peer_v):
    bp, sp_len, d = x_prompt.shape
    bs, ss_len, _ = x_sample.shape
    tp = bp * sp_len
    ts = bs * ss_len
    depth = w_in.shape[0]
    keep = min(ATT_BUF, sp_len)
    h = jnp.concatenate([x_prompt.reshape(tp, d), x_sample.reshape(ts, d)], axis=0)
    pos = jnp.concatenate([jnp.tile(jnp.arange(sp_len, dtype=jnp.int32), bp),
                           jnp.tile(PAST_LEN + jnp.arange(ss_len, dtype=jnp.int32), bs)])
    rope_tabs = _rope_tables(pos)
    zero_buf = jnp.zeros((bp, POOL_BUF, POOL_WIDTH), F32)
    zero_shift = jnp.zeros((bp, RWKV_PROJ), F32)
    zero_wkv = jnp.zeros((bp, RWKV_HEADS, HEAD_DIM, HEAD_DIM), F32)
    k_lo, v_lo, r_lo = POOL_WIDTH + ATT_WIDTH, POOL_WIDTH + 2 * ATT_WIDTH, POOL_WIDTH + 3 * ATT_WIDTH
    heads = lambda x, n: x.reshape(n, -1, ATT_HEADS, HEAD_DIM)
    outs_p = [[] for _ in range(5)]
    outs_s = [[] for _ in range(5)]
    kv_stacks = None
    ck_all = cache_k.reshape(depth * bs * ATT_BUF, ATT_WIDTH)
    cv_all = cache_v.reshape(depth * bs * ATT_BUF, ATT_WIDTH)
    u_all = peer_u.astype(BF16).reshape(depth * N_EXPERTS, d)
    v_all = peer_v.astype(BF16).reshape(depth * N_EXPERTS, d)
    for l in range(depth):
        lp = {'rwkv_mu': rwkv_mu[l], 'rwkv_w0': rwkv_w0[l],
              'rwkv_w_up': rwkv_w_up[l], 'rwkv_a0': rwkv_a0[l], 'rwkv_a_up': rwkv_a_up[l],
              'rwkv_g_up': rwkv_g_up[l], 'rwkv_k_k': rwkv_k_k[l], 'rwkv_k_a': rwkv_k_a[l],
              'rwkv_r_k': rwkv_r_k[l], 'rwkv_ln_w': rwkv_ln_w[l], 'rwkv_ln_b': rwkv_ln_b[l]}
        proj = _linear(h, w_in[l].astype(BF16), g=norm1_g[l], rope=rope_tabs, name="in_proj")
        proj_s = proj[tp:].reshape(bs, ss_len, IN_WIDTH)

        pool_p = _pool(proj, 0, bp, sp_len, zero_buf, pool_w[l], pool_scale[l], 0)
        pool_s = _pool(proj, tp, bs, ss_len, state_pool[l], pool_w[l], pool_scale[l], PAST_LEN)
        att_p = _attn_prompt(proj, bp, sp_len)
        att_s, *kv_stacks = _attn_sample(proj, tp, bs, ss_len, ck_all, cv_all, l, depth, kv_stacks)
        rw_p, wkv_p = _rwkv_mixer(proj, 0, bp, sp_len, zero_shift, zero_wkv, lp, min(sp_len, RWKV_CHUNK))
        rw_s, wkv_s = _rwkv_mixer(proj, tp, bs, ss_len, state_shift[l], state_wkv[l], lp, min(ss_len, RWKV_CHUNK))

        tail = lambda n, lo, hi: jnp.stack([proj[(b + 1) * sp_len - n:(b + 1) * sp_len, lo:hi] for b in range(bp)])
        outs_p[0].append(tail(POOL_BUF, 0, POOL_WIDTH))
        outs_p[1].append(heads(tail(keep, k_lo, v_lo), bp))
        outs_p[2].append(heads(tail(keep, v_lo, r_lo), bp))
        outs_p[3].append(tail(1, r_lo, IN_WIDTH)[:, 0])
        outs_p[4].append(wkv_p)
        outs_s[0].append(jnp.concatenate([state_pool[l], proj_s[:, :, :POOL_WIDTH]], axis=1)[:, -POOL_BUF:])
        outs_s[3].append(proj_s[:, -1, r_lo:])
        outs_s[4].append(wkv_s)

        mixed = jnp.concatenate([jnp.concatenate([pool_p, att_p, rw_p], axis=1),
                                 jnp.concatenate([pool_s, att_s, rw_s], axis=1)], axis=0)
        h = _linear(mixed, w_out[l].astype(BF16), resid=h, name="out_proj")
        h = _peer_ffn(h, norm2_g[l], peer_wq[l], peer_subkeys[l], u_all, v_all, l,
                      final_g if l == depth - 1 else None)
    y = h
    y_prompt = y[:tp].reshape(bp, sp_len, d)
    y_sample = y[tp:].reshape(bs, ss_len, d)
    s_k, s_v = (x.reshape(depth, bs, ATT_BUF, ATT_HEADS, HEAD_DIM) for x in kv_stacks)
    return (y_prompt, y_sample, *[jnp.stack(a) for a in outs_p],
            jnp.stack(outs_s[0]), s_k, s_v, jnp.stack(outs_s[3]), jnp.stack(outs_s[4]))
```

```python
import functools
import math

import jax
import jax.numpy as jnp
import numpy as np
from jax import lax
from jax.experimental import pallas as pl
from jax.experimental.pallas import tpu as pltpu

F32 = jnp.float32
BF16 = jnp.bfloat16

D_MODEL = 1024
HEAD_DIM = 64
POOL_WINDOWS = (2, 4, 8, 16)
POOL_GROUP = 64
POOL_WIDTH = 256
POOL_BUF = 15
ATT_WIDTH = 384
ATT_HEADS = 6
DILATED_GROUPS = ((128, 1), (512, 4), (2048, 16))
ATT_BUF = 2048
ROPE_DIM = 16
ROPE_THETA = 500000.0
NEG_INF = -1e30
RWKV_WIDTH = 384
RWKV_HEADS = 6
DECAY_LORA = 64
ICLR_LORA = 64
GATE_LORA = 128
RWKV_PROJ = 1408
GN_EPS = HEAD_DIM * 1e-5
N_KEYS = 128
N_EXPERTS = N_KEYS * N_KEYS
PEER_HEADS = 8
PEER_TOPK = 16
RMS_EPS = 1e-6

TOKEN_BLOCK = 256
EXPERT_CHUNK = 4096
G_PAIR = 8
G_PITCH = 72
RWKV_CHUNK = 64
RWKV_ROWS = 2
LANES = 128
IN_WIDTH = POOL_WIDTH + 3 * ATT_WIDTH + RWKV_PROJ
ATT_QBLOCK = 128
ATT_WINDOW = ATT_BUF + ATT_QBLOCK
PAST_LEN = 8192
VMEM_LIMIT = 56 * 1024 * 1024


def _cparams(sem):
    return pltpu.CompilerParams(dimension_semantics=sem, vmem_limit_bytes=VMEM_LIMIT)


def _rope_tables(pos):
    half = ROPE_DIM // 2
    inv = ROPE_THETA ** (-jnp.arange(half, dtype=F32) * 2.0 / ROPE_DIM)
    ang = pos.astype(F32)[:, None] * inv[None, :]
    cos, sin = jnp.cos(ang), jnp.sin(ang)
    n = pos.shape[0]
    one = jnp.ones((n, HEAD_DIM - ROPE_DIM), F32)
    zero = jnp.zeros((n, HEAD_DIM - ROPE_DIM), F32)
    zh = jnp.zeros((n, half), F32)
    c = jnp.concatenate([cos, cos, one], axis=1)
    a = jnp.concatenate([-sin, zh, zero], axis=1)
    b = jnp.concatenate([zh, sin, zero], axis=1)
    tile = lambda x: jnp.concatenate([x, x], axis=1)
    return tile(c), tile(a), tile(b)


def _rope_lanes(x, c, a, b):
    half = ROPE_DIM // 2
    return x * c + pltpu.roll(x, LANES - half, 1) * a + pltpu.roll(x, half, 1) * b


def _linear_kernel(*refs, norm, resid, passes, emit_xn, rope):
    it = iter(refs)
    x_ref = next(it)
    g_ref = next(it) if norm else None
    w_ref = next(it)
    wlo_ref = next(it) if passes == 3 else None
    r_ref = next(it) if resid else None
    rope_refs = [next(it) for _ in range(3)] if rope else None
    o_ref = next(it)
    xn_ref = next(it) if emit_xn else None

    x = x_ref[...].astype(F32)
    if norm:
        x = x * lax.rsqrt(jnp.mean(x * x, axis=-1, keepdims=True) + RMS_EPS) * g_ref[...]
    if emit_xn:
        xn_ref[...] = x.astype(xn_ref.dtype)
    x_hi = x.astype(BF16)
    acc = jnp.dot(x_hi, w_ref[...], preferred_element_type=F32)
    if passes == 3:
        x_lo = (x - x_hi.astype(F32)).astype(BF16)
        acc = acc + jnp.dot(x_lo, w_ref[...], preferred_element_type=F32)
        acc = acc + jnp.dot(x_hi, wlo_ref[...], preferred_element_type=F32)
    if resid:
        acc = acc + r_ref[...]
    if rope:
        c, a, b = (r[...] for r in rope_refs)
        lo, hi = POOL_WIDTH, POOL_WIDTH + 2 * ATT_WIDTH
        o_ref[:, :lo] = acc[:, :lo]
        for j in range(lo, hi, LANES):
            o_ref[:, j:j + LANES] = _rope_lanes(acc[:, j:j + LANES], c, a, b)
        o_ref[:, hi:] = acc[:, hi:]
    else:
        o_ref[...] = acc


def _linear(x, w_hi, *, g=None, w_lo=None, resid=None, emit_xn=False, rope=None, tb=TOKEN_BLOCK, name="linear"):
    t, k = x.shape
    n = w_hi.shape[1]
    assert t % tb == 0
    norm = g is not None
    passes = 3 if w_lo is not None else 1
    ins = [x]
    specs = [pl.BlockSpec((tb, k), lambda i: (i, 0))]
    if norm:
        ins.append(g.reshape(1, k).astype(F32))
        specs.append(pl.BlockSpec((1, k), lambda i: (0, 0)))
    ins.append(w_hi)
    specs.append(pl.BlockSpec((k, n), lambda i: (0, 0)))
    if passes == 3:
        ins.append(w_lo)
        specs.append(pl.BlockSpec((k, n), lambda i: (0, 0)))
    if resid is not None:
        ins.append(resid)
        specs.append(pl.BlockSpec((tb, n), lambda i: (i, 0)))
    if rope is not None:
        ins.extend(rope)
        specs.extend([pl.BlockSpec((tb, LANES), lambda i: (i, 0))] * 3)
    out_shape = [jax.ShapeDtypeStruct((t, n), F32)]
    out_specs = [pl.BlockSpec((tb, n), lambda i: (i, 0))]
    if emit_xn:
        out_shape.append(jax.ShapeDtypeStruct((t, k), BF16))
        out_specs.append(pl.BlockSpec((tb, k), lambda i: (i, 0)))
    outs = pl.pallas_call(
        functools.partial(_linear_kernel, norm=norm, resid=resid is not None, passes=passes, emit_xn=emit_xn,
                          rope=rope is not None),
        grid=(t // tb,),
        in_specs=specs,
        out_specs=out_specs,
        out_shape=out_shape,
        compiler_params=_cparams(("parallel",)),
        name=name,
    )(*ins)
    return outs if emit_xn else outs[0]


def _split_bf16(w):
    hi = w.astype(BF16)
    lo = (w - hi.astype(F32)).astype(BF16)
    return hi, lo


def _dot_nt(a, b):
    return lax.dot_general(a, b, (((1,), (1,)), ((), ())), preferred_element_type=F32)


def _oddeven_merge(lo, hi, r):
    step = r * 2
    if step < hi - lo:
        yield from _oddeven_merge(lo, hi, step)
        yield from _oddeven_merge(lo + r, hi, step)
        yield from [(i, i + r) for i in range(lo + r, hi - r, step)]
    else:
        yield (lo, lo + r)


def _oddeven_sort(lo, hi):
    if hi - lo >= 1:
        mid = lo + (hi - lo) // 2
        yield from _oddeven_sort(lo, mid)
        yield from _oddeven_sort(mid + 1, hi)
        yield from _oddeven_merge(lo, hi, 1)


_SORT16 = tuple(_oddeven_sort(0, PEER_TOPK - 1))
_BITONIC16 = tuple((i, i + d) for d in (8, 4, 2, 1) for i in range(PEER_TOPK) if not i & d)
SUBLANES = 8


def _exchange(v, p, i, j):
    keep = v[i] >= v[j]
    v[i], v[j] = jnp.where(keep, v[i], v[j]), jnp.where(keep, v[j], v[i])
    p[i], p[j] = jnp.where(keep, p[i], p[j]), jnp.where(keep, p[j], p[i])


def _top16(v, p, n_real):
    v, p = list(v), list(p)
    for i, j in _SORT16:
        if j < n_real:
            _exchange(v, p, i, j)
    for shift in (4, 2, 1):
        vb = [pltpu.roll(x, shift, 0) for x in v]
        pb = [pltpu.roll(x, shift, 0) for x in p]
        for i in range(PEER_TOPK):
            keep = v[i] >= vb[PEER_TOPK - 1 - i]
            v[i] = jnp.where(keep, v[i], vb[PEER_TOPK - 1 - i])
            p[i] = jnp.where(keep, p[i], pb[PEER_TOPK - 1 - i])
        for i, j in _BITONIC16:
            _exchange(v, p, i, j)
    return v, p


def _rows16(xs):
    sub = lax.broadcasted_iota(jnp.int32, xs[0].shape, 0)
    halves = []
    for base in (0, SUBLANES):
        acc = xs[base]
        for k in range(1, SUBLANES):
            acc = jnp.where(sub == k, xs[base + k], acc)
        halves.append(acc)
    return jnp.concatenate(halves, axis=0)


def _route_kernel(q_ref, sk_ref, i1_ref, i2_ref, gate_ref, tv_ref, ti_ref, e1_ref, e2_ref, gt_ref):
    tb = q_ref.shape[0]
    neg = jnp.float32(-jnp.inf)
    sub_iota = lax.broadcasted_iota(jnp.int32, (SUBLANES, tb), 0).astype(F32)

    def half_topk(hc, carry):
        off = pl.multiple_of(hc * N_KEYS, N_KEYS)
        qh = q_ref[:, pl.ds(off, N_KEYS)]
        sk = sk_ref[hc]
        q_hi = qh.astype(BF16)
        q_lo = (qh - q_hi.astype(F32)).astype(BF16)
        s_hi = sk.astype(BF16)
        s_lo = (sk - s_hi.astype(F32)).astype(BF16)
        sc = _dot_nt(s_hi, q_hi) + _dot_nt(s_lo, q_hi) + _dot_nt(s_hi, q_lo)
        n_slab = N_KEYS // SUBLANES
        vals = [sc[j * SUBLANES:(j + 1) * SUBLANES] for j in range(n_slab)]
        idxs = [sub_iota + float(j * SUBLANES) for j in range(n_slab)]
        vals, idxs = _top16(vals, idxs, n_slab)
        tv_ref[hc] = _rows16(vals)
        ti_ref[hc] = _rows16(idxs)
        return carry

    lax.fori_loop(0, 2 * PEER_HEADS, half_topk, 0)

    row8 = lax.broadcasted_iota(jnp.int32, (SUBLANES, tb), 0)

    def pair_topk(h, carry):
        a = tv_ref[2 * h]
        b = tv_ref[2 * h + 1]
        ia = ti_ref[2 * h] * float(N_KEYS)
        ib = ti_ref[2 * h + 1]
        vs = [a[0:1] + b[0:8], a[0:1] + b[8:16]]
        code = [ia[0:1] + ib[0:8], ia[0:1] + ib[8:16]]
        for i in range(1, 8):
            lim = PEER_TOPK // (i + 1)
            v = a[i:i + 1] + b[0:8]
            vs.append(jnp.where(row8 < lim, v, neg) if lim < 8 else v)
            code.append(ia[i:i + 1] + ib[0:8])
        vs.append(a[8:16] + b[0:1])
        code.append(ia[8:16] + ib[0:1])
        n_real = len(vs)
        pad_v = jnp.full((SUBLANES, tb), neg, F32)
        pad_c = jnp.zeros((SUBLANES, tb), F32)
        vs += [pad_v] * (PEER_TOPK - n_real)
        code += [pad_c] * (PEER_TOPK - n_real)
        tops, code = _top16(vs, code, n_real)
        e = [jnp.exp(t - tops[0]) for t in tops]
        total = e[0]
        for x in e[1:]:
            total = total + x
        first = [jnp.floor(cd * (1.0 / N_KEYS)) for cd in code]
        second = [cd - f * float(N_KEYS) for cd, f in zip(code, first)]
        row = pl.multiple_of(h * PEER_TOPK, PEER_TOPK)
        gt_ref[pl.ds(row, PEER_TOPK), :] = _rows16([x / total for x in e])
        e1_ref[pl.ds(row, PEER_TOPK), :] = _rows16(first)
        e2_ref[pl.ds(row, PEER_TOPK), :] = _rows16(second)
        return carry

    lax.fori_loop(0, PEER_HEADS, pair_topk, 0)
    i1_ref[...] = e1_ref[...].T
    i2_ref[...] = e2_ref[...].T
    gate_ref[...] = gt_ref[...].T


def _route(q, subkeys, tb=TOKEN_BLOCK):
    t = q.shape[0]
    nsel = PEER_HEADS * PEER_TOPK
    sk = subkeys.reshape(2 * PEER_HEADS, N_KEYS, N_KEYS)
    out = jax.ShapeDtypeStruct((t, nsel), F32)
    spec = pl.BlockSpec((tb, nsel), lambda i: (i, 0))
    return pl.pallas_call(
        _route_kernel,
        grid=(t // tb,),
        in_specs=[pl.BlockSpec((tb, q.shape[1]), lambda i: (i, 0)),
                  pl.BlockSpec(sk.shape, lambda i: (0, 0, 0))],
        out_specs=[spec, spec, spec],
        out_shape=[out, out, out],
        scratch_shapes=[pltpu.VMEM((2 * PEER_HEADS, PEER_TOPK, tb), F32),
                        pltpu.VMEM((2 * PEER_HEADS, PEER_TOPK, tb), F32),
                        pltpu.VMEM((nsel, tb), F32),
                        pltpu.VMEM((nsel, tb), F32),
                        pltpu.VMEM((nsel, tb), F32)],
        compiler_params=_cparams(("parallel",)),
        name="peer_route",
    )(q, sk)


def _gelu_exact(x):
    return 0.5 * x * (1.0 + lax.erf(x * (1.0 / math.sqrt(2.0))))


def _store_gate_tile(i1_ref, i2_ref, gate_ref, t, gs_ref, key_iota):
    r1 = i1_ref[pl.ds(t, 1), :]
    r2 = i2_ref[pl.ds(t, 1), :]
    gr = gate_ref[pl.ds(t, 1), :]
    a_t = jnp.where(key_iota == r1, gr, 0.0).astype(BF16)
    b_t = jnp.where(key_iota == r2, 1.0, 0.0).astype(BF16)
    g = _dot_nt(a_t, b_t)
    groups = range(0, N_KEYS, 2 * G_PAIR)
    hi = jnp.concatenate([g[r:r + G_PAIR] for r in groups], axis=0)
    lo = jnp.concatenate([g[r + G_PAIR:r + 2 * G_PAIR] for r in groups], axis=0)
    bits = lambda x: lax.bitcast_convert_type(x.astype(BF16).astype(F32), jnp.uint32)
    row = pl.multiple_of(t * G_PITCH, SUBLANES)
    gs_ref[pl.ds(row, N_KEYS // 2), :] = bits(hi) | (bits(lo) >> 16)


def _peer_kernel(xn_ref, i1_ref, i2_ref, gate_ref, h_ref, u_ref, v_ref, *rest):
    norm_ref = rest[0] if len(rest) == 5 else None
    o_ref, gs_ref, w_ref, acc_ref = rest[-4:]
    c = pl.program_id(1)
    tb = xn_ref.shape[0]
    ec = u_ref.shape[0]
    a_per_chunk = ec // N_KEYS
    assert a_per_chunk % (2 * G_PAIR) == 0

    @pl.when(c == 0)
    def _build_gates():
        key_iota = lax.broadcasted_iota(jnp.int32, (N_KEYS, N_KEYS), 0).astype(F32)

        def one_token(t, carry):
            _store_gate_tile(i1_ref, i2_ref, gate_ref, t, gs_ref, key_iota)
            return carry

        lax.fori_loop(0, tb, one_token, 0, unroll=32)
        acc_ref[...] = jnp.zeros_like(acc_ref)

    hid = _dot_nt(xn_ref[...], u_ref[...])
    for p in range(a_per_chunk // 2):
        packed = gs_ref[pl.ds(c * (a_per_chunk // 2) + p, tb, stride=G_PITCH), :]
        al = (p // G_PAIR) * 2 * G_PAIR + p % G_PAIR
        for al_h, g_a in ((al, lax.bitcast_convert_type(packed & jnp.uint32(0xFFFF0000), F32)),
                          (al + G_PAIR, lax.bitcast_convert_type(packed << 16, F32))):
            act = _gelu_exact(hid[:, al_h * N_KEYS:(al_h + 1) * N_KEYS])
            w_ref[:, al_h * N_KEYS:(al_h + 1) * N_KEYS] = (g_a * act).astype(BF16)
    acc_ref[...] += jnp.dot(w_ref[...], v_ref[...], preferred_element_type=F32)

    @pl.when(c == pl.num_programs(1) - 1)
    def _finish():
        y = h_ref[...] + acc_ref[...]
        if norm_ref is not None:
            y = y * lax.rsqrt(jnp.mean(y * y, axis=-1, keepdims=True) + RMS_EPS) * norm_ref[...]
        o_ref[...] = y


def _peer_experts(xn, i1, i2, gate, h, u, v, layer, out_norm_g=None, tb=TOKEN_BLOCK, ec=EXPERT_CHUNK):
    t, d = h.shape
    n_exp = N_EXPERTS
    chunk0 = layer * (n_exp // ec)
    nsel = i1.shape[1]
    tok = lambda i, c: (i, 0)
    extra = [] if out_norm_g is None else [out_norm_g.reshape(1, d)]
    return pl.pallas_call(
        _peer_kernel,
        grid=(t // tb, n_exp // ec),
        in_specs=[pl.BlockSpec((tb, d), tok),
                  pl.BlockSpec((tb, nsel), tok),
                  pl.BlockSpec((tb, nsel), tok),
                  pl.BlockSpec((tb, nsel), tok),
                  pl.BlockSpec((tb, d), tok),
                  pl.BlockSpec((ec, d), lambda i, c: (chunk0 + c, 0)),
                  pl.BlockSpec((ec, d), lambda i, c: (chunk0 + c, 0))]
                 + [pl.BlockSpec((1, d), lambda i, c: (0, 0))] * len(extra),
        out_specs=pl.BlockSpec((tb, d), tok),
        out_shape=jax.ShapeDtypeStruct((t, d), F32),
        scratch_shapes=[pltpu.VMEM((tb * G_PITCH, N_KEYS), jnp.uint32),
                        pltpu.VMEM((tb, ec), BF16),
                        pltpu.VMEM((tb, d), F32)],
        compiler_params=_cparams(("parallel", "arbitrary")),
        name="peer_experts",
    )(xn, i1, i2, gate, h, u, v, *extra)


def _peer_ffn(h, g2, wq, subkeys, u, v, layer, out_norm_g=None):
    wq_hi, wq_lo = _split_bf16(wq)
    q, xn = _linear(h, wq_hi, g=g2, w_lo=wq_lo, emit_xn=True, name="peer_query")
    i1, i2, gate = _route(q, subkeys)
    return _peer_experts(xn, i1, i2, gate, h, u, v, layer, out_norm_g)


def _band_bias(rows):
    r = np.arange(rows)[:, None]
    c = np.arange(ATT_WINDOW)[None, :]
    delta = ATT_BUF + r - c
    mult = np.zeros(delta.shape, np.float64)
    for window, dil in DILATED_GROUPS:
        mult += (delta >= 0) & (delta <= window) & (delta % dil == 0)
    return jnp.asarray(np.where(mult > 0, np.log(np.maximum(mult, 1.0)), NEG_INF), F32)


def _band_attend(q, kw, vw, bias):
    lane = lax.broadcasted_iota(jnp.int32, q.shape, 1)
    scores = [_dg(jnp.where((lane // HEAD_DIM) == half, q, jnp.zeros_like(q)), kw, _NT) for half in range(2)]
    outs = []
    for s in scores:
        s = s + bias
        m = jnp.max(s, axis=1, keepdims=True)
        p = jnp.exp(s - m)
        den = jnp.sum(p, axis=1, keepdims=True)
        outs.append(_dg(p.astype(BF16), vw, _NN) / den)
    return jnp.where((lane // HEAD_DIM) == 0, outs[0], outs[1])


def _attn_prompt_kernel(q_ref, k_ref, v_ref, bias_ref, o_ref, q_scr, k_scr, v_scr):
    s_len = q_ref.shape[0]
    zeros = jnp.zeros((ATT_BUF, LANES), BF16)
    k_scr[0:ATT_BUF, :] = zeros
    v_scr[0:ATT_BUF, :] = zeros
    k_scr[ATT_BUF:, :] = k_ref[...].astype(BF16)
    v_scr[ATT_BUF:, :] = v_ref[...].astype(BF16)
    q_scr[...] = (q_ref[...] * (HEAD_DIM ** -0.5)).astype(BF16)
    col = lax.broadcasted_iota(jnp.int32, (1, ATT_WINDOW), 1)

    def qblock(qi, carry):
        r0 = pl.multiple_of(qi * ATT_QBLOCK, ATT_QBLOCK)
        bias = jnp.where(col >= ATT_BUF - r0, bias_ref[...], NEG_INF)
        o_ref[pl.ds(r0, ATT_QBLOCK), :] = _band_attend(
            q_scr[pl.ds(r0, ATT_QBLOCK), :], k_scr[pl.ds(r0, ATT_WINDOW), :], v_scr[pl.ds(r0, ATT_WINDOW), :], bias)
        return carry

    lax.fori_loop(0, s_len // ATT_QBLOCK, qblock, 0)


_Q_BLOCK0 = POOL_WIDTH // LANES
_K_BLOCK0 = (POOL_WIDTH + ATT_WIDTH) // LANES
_V_BLOCK0 = (POOL_WIDTH + 2 * ATT_WIDTH) // LANES


def _attn_prompt(proj, n_seq, s_len):
    blk = lambda first: pl.BlockSpec((s_len, LANES), lambda b, hp: (b, first + hp))
    bias = _band_bias(ATT_QBLOCK)
    return pl.pallas_call(
        _attn_prompt_kernel,
        grid=(n_seq, ATT_WIDTH // LANES),
        in_specs=[blk(_Q_BLOCK0), blk(_K_BLOCK0), blk(_V_BLOCK0), pl.BlockSpec(bias.shape, lambda b, hp: (0, 0))],
        out_specs=pl.BlockSpec((s_len, LANES), lambda b, hp: (b, hp)),
        out_shape=jax.ShapeDtypeStruct((n_seq * s_len, ATT_WIDTH), F32),
        scratch_shapes=[pltpu.VMEM((s_len, LANES), BF16),
                        pltpu.VMEM((ATT_BUF + s_len, LANES), BF16),
                        pltpu.VMEM((ATT_BUF + s_len, LANES), BF16)],
        compiler_params=_cparams(("parallel", "parallel")),
        name="attn_prompt",
    )(proj, proj, proj, bias)


def _attn_sample_kernel(*refs):
    q_ref, k_ref, v_ref, ck_ref, cv_ref, bias_ref = refs[:6]
    o_ref, nk_ref, nv_ref, k_scr, v_scr = refs[-5:]
    t = q_ref.shape[0]
    n_past = ck_ref.shape[1]
    assert ATT_WINDOW - n_past == LANES

    def as_columns(x):
        return jnp.concatenate([x, jnp.zeros((LANES - t, LANES), F32)], axis=0).T

    new_k = as_columns(k_ref[...])
    new_v = as_columns(v_ref[...])
    k_scr[:, 0:n_past] = ck_ref[...].astype(BF16)
    v_scr[:, 0:n_past] = cv_ref[...].astype(BF16)
    k_scr[:, n_past:] = new_k.astype(BF16)
    v_scr[:, n_past:] = new_v.astype(BF16)
    q = (q_ref[...] * (HEAD_DIM ** -0.5)).astype(BF16)
    lane = lax.broadcasted_iota(jnp.int32, q.shape, 1)
    scores = [_dg(jnp.where((lane // HEAD_DIM) == half, q, jnp.zeros_like(q)), k_scr[...], _NN) for half in range(2)]
    outs = []
    for s in scores:
        s = s + bias_ref[...]
        m = jnp.max(s, axis=1, keepdims=True)
        p = jnp.exp(s - m)
        den = jnp.sum(p, axis=1, keepdims=True)
        outs.append(_dg(p.astype(BF16), v_scr[...], _NT) / den)
    o_ref[...] = jnp.where((lane // HEAD_DIM) == 0, outs[0], outs[1])

    col = lax.broadcasted_iota(jnp.int32, (LANES, LANES), 1)
    for old_ref, new_cols, out_ref in ((ck_ref, new_k, nk_ref), (cv_ref, new_v, nv_ref)):
        shifted = pltpu.roll(old_ref[...], n_past - t, 1)
        out_ref[:, 0:n_past - LANES] = shifted[:, 0:n_past - LANES]
        out_ref[:, n_past - LANES:] = jnp.where(col < LANES - t, shifted[:, n_past - LANES:],
                                                 pltpu.roll(new_cols, LANES - t, 1))


def _attn_sample(proj, row0, n_seq, t_len, cache_k, cache_v, layer, depth, stacks):
    assert row0 % t_len == 0 and t_len % 8 == 0 and t_len <= ATT_QBLOCK
    blk0 = row0 // t_len
    pairs = ATT_WIDTH // LANES
    blk = lambda first: pl.BlockSpec((t_len, LANES), lambda b, hp: (blk0 + b, first + hp))
    cblk = pl.BlockSpec((LANES, ATT_BUF), lambda b, hp: ((layer * n_seq + b) * pairs + hp, 0))
    bias = _band_bias(t_len)
    stack_shape = jax.ShapeDtypeStruct(cache_k.shape, F32)
    carried = [] if stacks is None else list(stacks)
    return pl.pallas_call(
        _attn_sample_kernel,
        grid=(n_seq, pairs),
        in_specs=[blk(_Q_BLOCK0), blk(_K_BLOCK0), blk(_V_BLOCK0), cblk, cblk,
                  pl.BlockSpec(bias.shape, lambda b, hp: (0, 0))]
                 + [pl.BlockSpec(memory_space=pl.ANY)] * len(carried),
        out_specs=[pl.BlockSpec((t_len, LANES), lambda b, hp: (b, hp)), cblk, cblk],
        out_shape=[jax.ShapeDtypeStruct((n_seq * t_len, ATT_WIDTH), F32), stack_shape, stack_shape],
        input_output_aliases={6 + j: 1 + j for j in range(len(carried))},
        scratch_shapes=[pltpu.VMEM((LANES, ATT_WINDOW), BF16), pltpu.VMEM((LANES, ATT_WINDOW), BF16)],
        compiler_params=_cparams(("parallel", "parallel")),
        name="attn_sample",
    )(proj, proj, proj, cache_k, cache_v, bias, *carried)


def _pool_kernel(x_ref, buf_ref, w_ref, scale_ref, o_ref, *, pos0):
    x = x_ref[...]
    n = x.shape[0]
    xf = jnp.concatenate([buf_ref[...], x], axis=0)
    s2 = xf + pltpu.roll(xf, 1, 0)
    s4 = s2 + pltpu.roll(s2, 2, 0)
    s8 = s4 + pltpu.roll(s4, 4, 0)
    s16 = s8 + pltpu.roll(s8, 8, 0)
    group = lax.broadcasted_iota(jnp.int32, (1, POOL_WIDTH), 1) // POOL_GROUP
    sums = jnp.where(group == 0, s2, jnp.where(group == 1, s4, jnp.where(group == 2, s8, s16)))[POOL_BUF + 1:]
    win = jnp.where(group == 0, 2.0, jnp.where(group == 1, 4.0, jnp.where(group == 2, 8.0, 16.0)))
    seen = (lax.broadcasted_iota(jnp.int32, (n, 1), 0) + (pos0 + 1)).astype(F32)
    diff = sums / jnp.minimum(win, seen) - x
    o_ref[...] = _mm3(diff, w_ref[...]) * scale_ref[...]


def _pool(proj, row0, n_seq, s_len, buf, w, scale, pos0):
    assert POOL_WINDOWS == (2, 4, 8, 16) and row0 % s_len == 0
    blk0 = row0 // s_len
    wbd = jnp.zeros((POOL_WIDTH, POOL_WIDTH), F32)
    for i in range(len(POOL_WINDOWS)):
        wbd = wbd.at[i * POOL_GROUP:(i + 1) * POOL_GROUP, i * POOL_GROUP:(i + 1) * POOL_GROUP].set(w[i])
    buf16 = jnp.pad(buf, ((0, 0), (1, 0), (0, 0))).reshape(n_seq * (POOL_BUF + 1), POOL_WIDTH)
    return pl.pallas_call(
        functools.partial(_pool_kernel, pos0=pos0),
        grid=(n_seq,),
        in_specs=[pl.BlockSpec((s_len, POOL_WIDTH), lambda b: (blk0 + b, 0)),
                  pl.BlockSpec((POOL_BUF + 1, POOL_WIDTH), lambda b: (b, 0)),
                  pl.BlockSpec((POOL_WIDTH, POOL_WIDTH), lambda b: (0, 0)),
                  pl.BlockSpec((1, POOL_WIDTH), lambda b: (0, 0))],
        out_specs=pl.BlockSpec((s_len, POOL_WIDTH), lambda b: (b, 0)),
        out_shape=jax.ShapeDtypeStruct((n_seq * s_len, POOL_WIDTH), F32),
        compiler_params=_cparams(("parallel",)),
        name="pool_mixer",
    )(proj, buf16, wbd, scale.reshape(1, POOL_WIDTH))


def _split(x):
    hi = x.astype(BF16)
    return hi, (x - hi.astype(F32)).astype(BF16)


def _dg(a, b, dims):
    return lax.dot_general(a, b, (dims, ((), ())), preferred_element_type=F32)


_NN = ((1,), (0,))
_NT = ((1,), (1,))
_TN = ((0,), (0,))


def _mm3(a, b, dims=_NN):
    a_hi, a_lo = _split(a)
    b_hi, b_lo = _split(b)
    return _dg(a_hi, b_hi, dims) + (_dg(a_lo, b_hi, dims) + _dg(a_hi, b_lo, dims))


def _split3(x):
    x0 = x.astype(BF16)
    r1 = x - x0.astype(F32)
    x1 = r1.astype(BF16)
    return x0, x1, (r1 - x1.astype(F32)).astype(BF16)


def _mm_exact_rhs(a, b_bf16):
    a0, a1, a2 = _split3(a)
    return _dg(a0, b_bf16, _NN) + (_dg(a1, b_bf16, _NN) + _dg(a2, b_bf16, _NN))


def _mm_exact_lhs(a_bf16, b):
    b0, b1, b2 = _split3(b)
    return _dg(a_bf16, b0, _NN) + (_dg(a_bf16, b1, _NN) + _dg(a_bf16, b2, _NN))


def _rwkv_chunk(pblks, prev_rows, s0, prm):
    (mu, w0, w_up, a0, a_up, g_up, k_k, k_a, r_k, ln_w, ln_b) = prm
    n_seq = len(pblks)
    c = pblks[0].shape[0]
    n = n_seq * c
    pblk = jnp.concatenate(pblks, axis=0) if n_seq > 1 else pblks[0]
    shifted = [jnp.concatenate([prev_rows[i], pblks[i][:-1]], axis=0) if c > 1 else prev_rows[i]
               for i in range(n_seq)]
    shifted = jnp.concatenate(shifted, axis=0) if n_seq > 1 else shifted[0]
    ps = pblk + (shifted - pblk) * mu
    c1, c2, c3 = RWKV_WIDTH, 2 * RWKV_WIDTH, 3 * RWKV_WIDTH
    c4 = c3 + DECAY_LORA
    c5 = c4 + ICLR_LORA
    r, k, v = ps[:, :c1], ps[:, c1:c2], ps[:, c2:c3]
    wd, ad, gd = ps[:, c3:c4], ps[:, c4:c5], ps[:, c5:]

    z = -(w0 + _mm3(jnp.tanh(wd), w_up))
    softplus = jnp.maximum(z, 0.0) + jnp.log(1.0 + jnp.exp(-jnp.abs(z)))
    lw = -jnp.exp(-softplus - 0.5)
    a = jax.nn.sigmoid(a0 + _mm3(ad, a_up))
    g = _mm3(jax.nn.sigmoid(gd), g_up)

    ch_r = lax.broadcasted_iota(jnp.int32, (RWKV_WIDTH, RWKV_WIDTH), 0) // HEAD_DIM
    ch_c = lax.broadcasted_iota(jnp.int32, (RWKV_WIDTH, RWKV_WIDTH), 1) // HEAD_DIM
    same_head = jnp.where(ch_r == ch_c, 1.0, 0.0).astype(BF16)

    kk = k * k_k
    kk = kk / jnp.maximum(jnp.sqrt(_mm_exact_rhs(kk * kk, same_head)), 1e-12)
    k2 = k * (1.0 + (a - 1.0) * k_a)
    bonus = _mm_exact_rhs(r * k2 * r_k, same_head) * v

    n_r = lax.broadcasted_iota(jnp.int32, (n, n), 0)
    n_c = lax.broadcasted_iota(jnp.int32, (n, n), 1)
    run = (n_r // c == n_c // c) & (n_c <= n_r)
    cs = _mm_exact_lhs(jnp.where(run, 1.0, 0.0).astype(BF16), lw)
    e_up = jnp.exp(cs)
    e_dn = jnp.exp(-cs)
    kap = kk * jnp.exp(cs - lw)
    bet = kk * a * e_dn
    kt = k2 * e_dn
    rt = r * e_up

    t_r = lax.broadcasted_iota(jnp.int32, (c, c), 0)
    t_c = lax.broadcasted_iota(jnp.int32, (c, c), 1)
    incl = t_c <= t_r
    strict = t_c < t_r
    eye = jnp.where(t_r == t_c, 1.0, 0.0)
    chains = [(i, h) for i in range(n_seq) for h in range(RWKV_HEADS)]
    cut = lambda x, i, h: x[i * c:(i + 1) * c, h * HEAD_DIM:(h + 1) * HEAD_DIM]
    kp_x = [cut(kap, i, h) for i, h in chains]
    bt_x = [cut(bet, i, h) for i, h in chains]
    kt_x = [cut(kt, i, h) for i, h in chains]
    rt_x = [cut(rt, i, h) for i, h in chains]
    v_x = [cut(v, i, h) for i, h in chains]
    s_x = [s0[i][h] for i, h in chains]
    every = range(len(chains))
    quad = [_mm3(jnp.concatenate([kp_x[x], rt_x[x]], axis=0), jnp.concatenate([bt_x[x], kt_x[x]], axis=0), _NT)
            for x in every]
    a_b = [jnp.where(strict, quad[x][:c, :c], 0.0) for x in every]
    a_k = [jnp.where(strict, quad[x][:c, c:], 0.0) for x in every]
    m_b = [jnp.where(incl, quad[x][c:, :c], 0.0) for x in every]
    m_k = [jnp.where(incl, quad[x][c:, c:], 0.0) for x in every]
    akv = [_mm3(a_k[x], v_x[x]) for x in every]
    mkv = [_mm3(m_k[x], v_x[x]) for x in every]
    rs = [_mm3(rt_x[x], s_x[x], _NT) for x in every]
    vk = [_mm3(v_x[x], kt_x[x], _TN) for x in every]
    inv = [eye - jnp.where(t_r // 2 == t_c // 2, a_b[x], 0.0) for x in every]
    b = 2
    while b < c:
        pair = (t_r // (2 * b) == t_c // (2 * b)) & ((t_r // b) % 2 == 1) & ((t_c // b) % 2 == 0)
        low = [_mm3(jnp.where(pair, a_b[x], 0.0), inv[x]) for x in every]
        inv = [inv[x] - _mm3(inv[x], low[x]) for x in every]
        b *= 2
    pq = [_mm3(inv[x], jnp.concatenate([kp_x[x], akv[x]], axis=1)) for x in every]
    p_x = [pq[x][:, :HEAD_DIM] for x in every]
    q_x = [pq[x][:, HEAD_DIM:] for x in every]
    u_x = [_mm3(p_x[x], s_x[x], _NT) + q_x[x] for x in every]
    o_x = [rs[x] + mkv[x] - _mm3(m_b[x], u_x[x]) for x in every]
    ub = [_mm3(u_x[x], bt_x[x], _TN) for x in every]
    s_new = [[None] * RWKV_HEADS for _ in range(n_seq)]
    for x, (i, h) in enumerate(chains):
        gam = e_up[(i + 1) * c - 1:(i + 1) * c, h * HEAD_DIM:(h + 1) * HEAD_DIM]
        s_new[i][h] = (s_x[x] + vk[x] - ub[x]) * gam
    o = [jnp.concatenate(o_x[i * RWKV_HEADS:(i + 1) * RWKV_HEADS], axis=1) for i in range(n_seq)]
    o = jnp.concatenate(o, axis=0) if n_seq > 1 else o[0]
    mean = _mm_exact_rhs(o, same_head) * (1.0 / HEAD_DIM)
    d = o - mean
    var = _mm_exact_rhs(d * d, same_head) * (1.0 / HEAD_DIM)
    o = (d * lax.rsqrt(var + GN_EPS) * ln_w + ln_b + bonus) * g
    return [o[i * c:(i + 1) * c] for i in range(n_seq)], s_new


def _rwkv_kernel(*refs, rows):
    p_refs = refs[:rows]
    (prev_ref, s0_ref, mu_ref, w0_ref, wup_ref, a0_ref, aup_ref, gup_ref, kk_ref, ka_ref, rk_ref, lnw_ref, lnb_ref,
     o_ref, sT_ref, s_scr, last_scr) = refs[rows:]
    ci = pl.program_id(1)

    @pl.when(ci == 0)
    def _init():
        s_scr[...] = s0_ref[...]
        last_scr[...] = prev_ref[...]

    prm = (mu_ref[...], w0_ref[...], wup_ref[...], a0_ref[...], aup_ref[...], gup_ref[...], kk_ref[...],
           ka_ref[...], rk_ref[...], lnw_ref[...], lnb_ref[...])
    pblks = [p_refs[i][...] for i in range(rows)]
    outs, s_new = _rwkv_chunk(pblks, [last_scr[i] for i in range(rows)],
                              [[s_scr[i, h] for h in range(RWKV_HEADS)] for i in range(rows)], prm)
    c = pblks[0].shape[0]
    for i in range(rows):
        o_ref[i] = outs[i]
        for h in range(RWKV_HEADS):
            s_scr[i, h] = s_new[i][h]
        last_scr[i] = pblks[i][c - 1:c, :]

    @pl.when(ci == pl.num_programs(1) - 1)
    def _fin():
        sT_ref[...] = s_scr[...]


def _rwkv_mixer(proj, row0, n_seq, s_len, prev, wkv0, lp, chunk, rows=RWKV_ROWS):
    assert s_len % chunk == 0 and n_seq % rows == 0 and row0 % chunk == 0
    assert proj.shape[1] == 2 * RWKV_PROJ
    n_chunks = s_len // chunk
    blk0 = row0 // chunk
    row = lambda x: x.reshape(1, -1).astype(F32)
    full = lambda a: pl.BlockSpec(a.shape, lambda b, c: (0,) * a.ndim)
    weights = [row(lp['rwkv_mu']), row(lp['rwkv_w0']), lp['rwkv_w_up'], row(lp['rwkv_a0']), lp['rwkv_a_up'],
               lp['rwkv_g_up'], row(lp['rwkv_k_k']), row(lp['rwkv_k_a']), row(lp['rwkv_r_k']),
               row(lp['rwkv_ln_w']), row(lp['rwkv_ln_b'])]
    p_spec = lambda i: pl.BlockSpec((chunk, RWKV_PROJ), lambda b, c: (blk0 + (b * rows + i) * n_chunks + c, 1))
    out, s_t = pl.pallas_call(
        functools.partial(_rwkv_kernel, rows=rows),
        grid=(n_seq // rows, n_chunks),
        in_specs=[p_spec(i) for i in range(rows)]
                 + [pl.BlockSpec((rows, 1, RWKV_PROJ), lambda b, c: (b, 0, 0)),
                    pl.BlockSpec((rows, RWKV_HEADS, HEAD_DIM, HEAD_DIM), lambda b, c: (b, 0, 0, 0))]
                 + [full(w) for w in weights],
        out_specs=[pl.BlockSpec((rows, chunk, RWKV_WIDTH), lambda b, c: (b, c, 0)),
                   pl.BlockSpec((rows, RWKV_HEADS, HEAD_DIM, HEAD_DIM), lambda b, c: (b, 0, 0, 0))],
        out_shape=[jax.ShapeDtypeStruct((n_seq, s_len, RWKV_WIDTH), F32),
                   jax.ShapeDtypeStruct((n_seq, RWKV_HEADS, HEAD_DIM, HEAD_DIM), F32)],
        scratch_shapes=[pltpu.VMEM((rows, RWKV_HEADS, HEAD_DIM, HEAD_DIM), F32),
                        pltpu.VMEM((rows, 1, RWKV_PROJ), F32)],
        compiler_params=_cparams(("parallel", "arbitrary")),
        name="rwkv_mixer",
    )(*([proj] * rows), prev.reshape(n_seq, 1, RWKV_PROJ), wkv0, *weights)
    return out.reshape(n_seq * s_len, RWKV_WIDTH), s_t


def kernel(x_prompt, x_sample, state_pool, cache_k, cache_v, state_shift, state_wkv, norm1_g, norm2_g, final_g, w_in, w_out, pool_w, pool_scale, rwkv_mu, rwkv_w0, rwkv_w_up, rwkv_a0, rwkv_a_up, rwkv_g_up, rwkv_k_k, rwkv_k_a, rwkv_r_k, rwkv_ln_w, rwkv_ln_b, peer_wq, peer_subkeys, peer_u, peer_v):
    bp, sp_len, d = x_prompt.shape
    bs, ss_len, _ = x_sample.shape
    tp = bp * sp_len
    ts = bs * ss_len
    depth = w_in.shape[0]
    keep = min(ATT_BUF, sp_len)
    h = jnp.concatenate([x_prompt.reshape(tp, d), x_sample.reshape(ts, d)], axis=0)
    pos = jnp.concatenate([jnp.tile(jnp.arange(sp_len, dtype=jnp.int32), bp),
                           jnp.tile(PAST_LEN + jnp.arange(ss_len, dtype=jnp.int32), bs)])
    rope_tabs = _rope_tables(pos)
    zero_buf = jnp.zeros((bp, POOL_BUF, POOL_WIDTH), F32)
    zero_shift = jnp.zeros((bp, RWKV_PROJ), F32)
    zero_wkv = jnp.zeros((bp, RWKV_HEADS, HEAD_DIM, HEAD_DIM), F32)
    k_lo, v_lo, r_lo = POOL_WIDTH + ATT_WIDTH, POOL_WIDTH + 2 * ATT_WIDTH, POOL_WIDTH + 3 * ATT_WIDTH
    heads = lambda x, n: x.reshape(n, -1, ATT_HEADS, HEAD_DIM)
    outs_p = [[] for _ in range(5)]
    outs_s = [[] for _ in range(5)]
    kv_stacks = None
    feature_major = lambda x: jnp.transpose(x, (0, 1, 3, 4, 2)).reshape(depth * bs * ATT_WIDTH, ATT_BUF)
    ck_all = feature_major(cache_k)
    cv_all = feature_major(cache_v)
    u_all = peer_u.astype(BF16).reshape(depth * N_EXPERTS, d)
    v_all = peer_v.astype(BF16).reshape(depth * N_EXPERTS, d)
    for l in range(depth):
        lp = {'rwkv_mu': rwkv_mu[l], 'rwkv_w0': rwkv_w0[l],
              'rwkv_w_up': rwkv_w_up[l], 'rwkv_a0': rwkv_a0[l], 'rwkv_a_up': rwkv_a_up[l],
              'rwkv_g_up': rwkv_g_up[l], 'rwkv_k_k': rwkv_k_k[l], 'rwkv_k_a': rwkv_k_a[l],
              'rwkv_r_k': rwkv_r_k[l], 'rwkv_ln_w': rwkv_ln_w[l], 'rwkv_ln_b': rwkv_ln_b[l]}
        proj = _linear(h, w_in[l].astype(BF16), g=norm1_g[l], rope=rope_tabs, name="in_proj")
        proj_s = proj[tp:].reshape(bs, ss_len, IN_WIDTH)

        pool_p = _pool(proj, 0, bp, sp_len, zero_buf, pool_w[l], pool_scale[l], 0)
        pool_s = _pool(proj, tp, bs, ss_len, state_pool[l], pool_w[l], pool_scale[l], PAST_LEN)
        att_p = _attn_prompt(proj, bp, sp_len)
        att_s, *kv_stacks = _attn_sample(proj, tp, bs, ss_len, ck_all, cv_all, l, depth, kv_stacks)
        rw_p, wkv_p = _rwkv_mixer(proj, 0, bp, sp_len, zero_shift, zero_wkv, lp, min(sp_len, RWKV_CHUNK))
        rw_s, wkv_s = _rwkv_mixer(proj, tp, bs, ss_len, state_shift[l], state_wkv[l], lp, min(ss_len, RWKV_CHUNK))

        tail = lambda n, lo, hi: jnp.stack([proj[(b + 1) * sp_len - n:(b + 1) * sp_len, lo:hi] for b in range(bp)])
        outs_p[0].append(tail(POOL_BUF, 0, POOL_WIDTH))
        outs_p[1].append(heads(tail(keep, k_lo, v_lo), bp))
        outs_p[2].append(heads(tail(keep, v_lo, r_lo), bp))
        outs_p[3].append(tail(1, r_lo, IN_WIDTH)[:, 0])
        outs_p[4].append(wkv_p)
        outs_s[0].append(jnp.concatenate([state_pool[l], proj_s[:, :, :POOL_WIDTH]], axis=1)[:, -POOL_BUF:])
        outs_s[3].append(proj_s[:, -1, r_lo:])
        outs_s[4].append(wkv_s)

        mixed = jnp.concatenate([jnp.concatenate([pool_p, att_p, rw_p], axis=1),
                                 jnp.concatenate([pool_s, att_s, rw_s], axis=1)], axis=0)
        h = _linear(mixed, w_out[l].astype(BF16), resid=h, name="out_proj")
        h = _peer_ffn(h, norm2_g[l], peer_wq[l], peer_subkeys[l], u_all, v_all, l,
                      final_g if l == depth - 1 else None)
    y = h
    y_prompt = y[:tp].reshape(bp, sp_len, d)
    y_sample = y[tp:].reshape(bs, ss_len, d)
    s_k, s_v = (jnp.transpose(x.reshape(depth, bs, ATT_HEADS, HEAD_DIM, ATT_BUF), (0, 1, 4, 2, 3)) for x in kv_stacks)
    return (y_prompt, y_sample, *[jnp.stack(a) for a in outs_p],
            jnp.stack(outs_s[0]), s_k, s_v, jnp.stack(outs_s[3]), jnp.stack(outs_s[4]))
```

```python
import functools
import math

import jax
import jax.numpy as jnp
import numpy as np
from jax import lax
from jax.experimental import pallas as pl
from jax.experimental.pallas import tpu as pltpu

F32 = jnp.float32
BF16 = jnp.bfloat16

D_MODEL = 1024
HEAD_DIM = 64
POOL_WINDOWS = (2, 4, 8, 16)
POOL_GROUP = 64
POOL_WIDTH = 256
POOL_BUF = 15
ATT_WIDTH = 384
ATT_HEADS = 6
DILATED_GROUPS = ((128, 1), (512, 4), (2048, 16))
ATT_BUF = 2048
ROPE_DIM = 16
ROPE_THETA = 500000.0
NEG_INF = -1e30
RWKV_WIDTH = 384
RWKV_HEADS = 6
DECAY_LORA = 64
ICLR_LORA = 64
GATE_LORA = 128
RWKV_PROJ = 1408
GN_EPS = HEAD_DIM * 1e-5
N_KEYS = 128
N_EXPERTS = N_KEYS * N_KEYS
PEER_HEADS = 8
PEER_TOPK = 16
RMS_EPS = 1e-6

TOKEN_BLOCK = 256
EXPERT_CHUNK = 4096
G_PAIR = 8
G_PITCH = 72
RWKV_CHUNK = 64
RWKV_ROWS = 2
LANES = 128
IN_WIDTH = POOL_WIDTH + 3 * ATT_WIDTH + RWKV_PROJ
ATT_QBLOCK = 128
ATT_WINDOW = ATT_BUF + ATT_QBLOCK
PAST_LEN = 8192
VMEM_LIMIT = 56 * 1024 * 1024


def _cparams(sem):
    return pltpu.CompilerParams(dimension_semantics=sem, vmem_limit_bytes=VMEM_LIMIT)


def _rope_tables(pos):
    half = ROPE_DIM // 2
    inv = ROPE_THETA ** (-jnp.arange(half, dtype=F32) * 2.0 / ROPE_DIM)
    ang = pos.astype(F32)[:, None] * inv[None, :]
    cos, sin = jnp.cos(ang), jnp.sin(ang)
    n = pos.shape[0]
    one = jnp.ones((n, HEAD_DIM - ROPE_DIM), F32)
    zero = jnp.zeros((n, HEAD_DIM - ROPE_DIM), F32)
    zh = jnp.zeros((n, half), F32)
    c = jnp.concatenate([cos, cos, one], axis=1)
    a = jnp.concatenate([-sin, zh, zero], axis=1)
    b = jnp.concatenate([zh, sin, zero], axis=1)
    tile = lambda x: jnp.concatenate([x, x], axis=1)
    return tile(c), tile(a), tile(b)


def _rope_lanes(x, c, a, b):
    half = ROPE_DIM // 2
    return x * c + pltpu.roll(x, LANES - half, 1) * a + pltpu.roll(x, half, 1) * b


def _linear_kernel(*refs, norm, resid, passes, emit_xn, rope):
    it = iter(refs)
    x_ref = next(it)
    g_ref = next(it) if norm else None
    w_ref = next(it)
    wlo_ref = next(it) if passes == 3 else None
    r_ref = next(it) if resid else None
    rope_refs = [next(it) for _ in range(3)] if rope else None
    o_ref = next(it)
    xn_ref = next(it) if emit_xn else None

    x = x_ref[...].astype(F32)
    if norm:
        x = x * lax.rsqrt(jnp.mean(x * x, axis=-1, keepdims=True) + RMS_EPS) * g_ref[...]
    if emit_xn:
        xn_ref[...] = x.astype(xn_ref.dtype)
    x_hi = x.astype(BF16)
    acc = jnp.dot(x_hi, w_ref[...], preferred_element_type=F32)
    if passes == 3:
        x_lo = (x - x_hi.astype(F32)).astype(BF16)
        acc = acc + jnp.dot(x_lo, w_ref[...], preferred_element_type=F32)
        acc = acc + jnp.dot(x_hi, wlo_ref[...], preferred_element_type=F32)
    if resid:
        acc = acc + r_ref[...]
    if rope:
        c, a, b = (r[...] for r in rope_refs)
        lo, hi = POOL_WIDTH, POOL_WIDTH + 2 * ATT_WIDTH
        o_ref[:, :lo] = acc[:, :lo]
        for j in range(lo, hi, LANES):
            o_ref[:, j:j + LANES] = _rope_lanes(acc[:, j:j + LANES], c, a, b)
        o_ref[:, hi:] = acc[:, hi:]
    else:
        o_ref[...] = acc


def _linear(x, w_hi, *, g=None, w_lo=None, resid=None, emit_xn=False, rope=None, tb=TOKEN_BLOCK, name="linear"):
    t, k = x.shape
    n = w_hi.shape[1]
    assert t % tb == 0
    norm = g is not None
    passes = 3 if w_lo is not None else 1
    ins = [x]
    specs = [pl.BlockSpec((tb, k), lambda i: (i, 0))]
    if norm:
        ins.append(g.reshape(1, k).astype(F32))
        specs.append(pl.BlockSpec((1, k), lambda i: (0, 0)))
    ins.append(w_hi)
    specs.append(pl.BlockSpec((k, n), lambda i: (0, 0)))
    if passes == 3:
        ins.append(w_lo)
        specs.append(pl.BlockSpec((k, n), lambda i: (0, 0)))
    if resid is not None:
        ins.append(resid)
        specs.append(pl.BlockSpec((tb, n), lambda i: (i, 0)))
    if rope is not None:
        ins.extend(rope)
        specs.extend([pl.BlockSpec((tb, LANES), lambda i: (i, 0))] * 3)
    out_shape = [jax.ShapeDtypeStruct((t, n), F32)]
    out_specs = [pl.BlockSpec((tb, n), lambda i: (i, 0))]
    if emit_xn:
        out_shape.append(jax.ShapeDtypeStruct((t, k), BF16))
        out_specs.append(pl.BlockSpec((tb, k), lambda i: (i, 0)))
    outs = pl.pallas_call(
        functools.partial(_linear_kernel, norm=norm, resid=resid is not None, passes=passes, emit_xn=emit_xn,
                          rope=rope is not None),
        grid=(t // tb,),
        in_specs=specs,
        out_specs=out_specs,
        out_shape=out_shape,
        compiler_params=_cparams(("parallel",)),
        name=name,
    )(*ins)
    return outs if emit_xn else outs[0]


def _dot_nt(a, b):
    return lax.dot_general(a, b, (((1,), (1,)), ((), ())), preferred_element_type=F32)


def _oddeven_merge(lo, hi, r):
    step = r * 2
    if step < hi - lo:
        yield from _oddeven_merge(lo, hi, step)
        yield from _oddeven_merge(lo + r, hi, step)
        yield from [(i, i + r) for i in range(lo + r, hi - r, step)]
    else:
        yield (lo, lo + r)


def _oddeven_sort(lo, hi):
    if hi - lo >= 1:
        mid = lo + (hi - lo) // 2
        yield from _oddeven_sort(lo, mid)
        yield from _oddeven_sort(mid + 1, hi)
        yield from _oddeven_merge(lo, hi, 1)


_SORT16 = tuple(_oddeven_sort(0, PEER_TOPK - 1))
_BITONIC16 = tuple((i, i + d) for d in (8, 4, 2, 1) for i in range(PEER_TOPK) if not i & d)
SUBLANES = 8


def _exchange(v, p, i, j):
    keep = v[i] >= v[j]
    v[i], v[j] = jnp.where(keep, v[i], v[j]), jnp.where(keep, v[j], v[i])
    p[i], p[j] = jnp.where(keep, p[i], p[j]), jnp.where(keep, p[j], p[i])


def _top16(v, p, n_real):
    v, p = list(v), list(p)
    for i, j in _SORT16:
        if j < n_real:
            _exchange(v, p, i, j)
    for shift in (4, 2, 1):
        vb = [pltpu.roll(x, shift, 0) for x in v]
        pb = [pltpu.roll(x, shift, 0) for x in p]
        for i in range(PEER_TOPK):
            keep = v[i] >= vb[PEER_TOPK - 1 - i]
            v[i] = jnp.where(keep, v[i], vb[PEER_TOPK - 1 - i])
            p[i] = jnp.where(keep, p[i], pb[PEER_TOPK - 1 - i])
        for i, j in _BITONIC16:
            _exchange(v, p, i, j)
    return v, p


def _rows16(xs):
    sub = lax.broadcasted_iota(jnp.int32, xs[0].shape, 0)
    halves = []
    for base in (0, SUBLANES):
        acc = xs[base]
        for k in range(1, SUBLANES):
            acc = jnp.where(sub == k, xs[base + k], acc)
        halves.append(acc)
    return jnp.concatenate(halves, axis=0)


def _route_kernel(q_ref, sk_ref, i1_ref, i2_ref, gate_ref, tv_ref, ti_ref, e1_ref, e2_ref, gt_ref):
    tb = q_ref.shape[0]
    neg = jnp.float32(-jnp.inf)
    sub_iota = lax.broadcasted_iota(jnp.int32, (SUBLANES, tb), 0).astype(F32)

    def half_topk(hc, carry):
        off = pl.multiple_of(hc * N_KEYS, N_KEYS)
        qh = q_ref[:, pl.ds(off, N_KEYS)]
        sk = sk_ref[hc]
        q_hi = qh.astype(BF16)
        q_lo = (qh - q_hi.astype(F32)).astype(BF16)
        s_hi = sk.astype(BF16)
        s_lo = (sk - s_hi.astype(F32)).astype(BF16)
        sc = _dot_nt(s_hi, q_hi) + _dot_nt(s_lo, q_hi) + _dot_nt(s_hi, q_lo)
        n_slab = N_KEYS // SUBLANES
        vals = [sc[j * SUBLANES:(j + 1) * SUBLANES] for j in range(n_slab)]
        idxs = [sub_iota + float(j * SUBLANES) for j in range(n_slab)]
        vals, idxs = _top16(vals, idxs, n_slab)
        tv_ref[hc] = _rows16(vals)
        ti_ref[hc] = _rows16(idxs)
        return carry

    lax.fori_loop(0, 2 * PEER_HEADS, half_topk, 0)

    row8 = lax.broadcasted_iota(jnp.int32, (SUBLANES, tb), 0)

    def pair_topk(h, carry):
        a = tv_ref[2 * h]
        b = tv_ref[2 * h + 1]
        ia = ti_ref[2 * h] * float(N_KEYS)
        ib = ti_ref[2 * h + 1]
        vs = [a[0:1] + b[0:8], a[0:1] + b[8:16]]
        code = [ia[0:1] + ib[0:8], ia[0:1] + ib[8:16]]
        for i in range(1, 8):
            lim = PEER_TOPK // (i + 1)
            v = a[i:i + 1] + b[0:8]
            vs.append(jnp.where(row8 < lim, v, neg) if lim < 8 else v)
            code.append(ia[i:i + 1] + ib[0:8])
        vs.append(a[8:16] + b[0:1])
        code.append(ia[8:16] + ib[0:1])
        n_real = len(vs)
        pad_v = jnp.full((SUBLANES, tb), neg, F32)
        pad_c = jnp.zeros((SUBLANES, tb), F32)
        vs += [pad_v] * (PEER_TOPK - n_real)
        code += [pad_c] * (PEER_TOPK - n_real)
        tops, code = _top16(vs, code, n_real)
        e = [jnp.exp(t - tops[0]) for t in tops]
        total = e[0]
        for x in e[1:]:
            total = total + x
        first = [jnp.floor(cd * (1.0 / N_KEYS)) for cd in code]
        second = [cd - f * float(N_KEYS) for cd, f in zip(code, first)]
        row = pl.multiple_of(h * PEER_TOPK, PEER_TOPK)
        gt_ref[pl.ds(row, PEER_TOPK), :] = _rows16([x / total for x in e])
        e1_ref[pl.ds(row, PEER_TOPK), :] = _rows16(first)
        e2_ref[pl.ds(row, PEER_TOPK), :] = _rows16(second)
        return carry

    lax.fori_loop(0, PEER_HEADS, pair_topk, 0)
    i1_ref[...] = e1_ref[...].T
    i2_ref[...] = e2_ref[...].T
    gate_ref[...] = gt_ref[...].T


def _route(q, subkeys, tb=TOKEN_BLOCK):
    t = q.shape[0]
    nsel = PEER_HEADS * PEER_TOPK
    sk = subkeys.reshape(2 * PEER_HEADS, N_KEYS, N_KEYS)
    out = jax.ShapeDtypeStruct((t, nsel), F32)
    spec = pl.BlockSpec((tb, nsel), lambda i: (i, 0))
    return pl.pallas_call(
        _route_kernel,
        grid=(t // tb,),
        in_specs=[pl.BlockSpec((tb, q.shape[1]), lambda i: (i, 0)),
                  pl.BlockSpec(sk.shape, lambda i: (0, 0, 0))],
        out_specs=[spec, spec, spec],
        out_shape=[out, out, out],
        scratch_shapes=[pltpu.VMEM((2 * PEER_HEADS, PEER_TOPK, tb), F32),
                        pltpu.VMEM((2 * PEER_HEADS, PEER_TOPK, tb), F32),
                        pltpu.VMEM((nsel, tb), F32),
                        pltpu.VMEM((nsel, tb), F32),
                        pltpu.VMEM((nsel, tb), F32)],
        compiler_params=_cparams(("parallel",)),
        name="peer_route",
    )(q, sk)


def _gelu_exact(x):
    return 0.5 * x * (1.0 + lax.erf(x * (1.0 / math.sqrt(2.0))))


def _store_gate_tile(i1_ref, i2_ref, gate_ref, t, gs_ref, key_iota):
    r1 = i1_ref[pl.ds(t, 1), :]
    r2 = i2_ref[pl.ds(t, 1), :]
    gr = gate_ref[pl.ds(t, 1), :]
    a_t = jnp.where(key_iota == r1, gr, 0.0).astype(BF16)
    b_t = jnp.where(key_iota == r2, 1.0, 0.0).astype(BF16)
    g = _dot_nt(a_t, b_t)
    groups = range(0, N_KEYS, 2 * G_PAIR)
    hi = jnp.concatenate([g[r:r + G_PAIR] for r in groups], axis=0)
    lo = jnp.concatenate([g[r + G_PAIR:r + 2 * G_PAIR] for r in groups], axis=0)
    bits = lambda x: lax.bitcast_convert_type(x.astype(BF16).astype(F32), jnp.uint32)
    row = pl.multiple_of(t * G_PITCH, SUBLANES)
    gs_ref[pl.ds(row, N_KEYS // 2), :] = bits(hi) | (bits(lo) >> 16)


def _peer_kernel(xn_ref, i1_ref, i2_ref, gate_ref, h_ref, u_ref, v_ref, *rest):
    norm_ref = rest[0] if len(rest) == 5 else None
    o_ref, gs_ref, w_ref, acc_ref = rest[-4:]
    c = pl.program_id(1)
    tb = xn_ref.shape[0]
    ec = u_ref.shape[0]
    a_per_chunk = ec // N_KEYS
    assert a_per_chunk % (2 * G_PAIR) == 0

    @pl.when(c == 0)
    def _build_gates():
        key_iota = lax.broadcasted_iota(jnp.int32, (N_KEYS, N_KEYS), 0).astype(F32)

        def one_token(t, carry):
            _store_gate_tile(i1_ref, i2_ref, gate_ref, t, gs_ref, key_iota)
            return carry

        lax.fori_loop(0, tb, one_token, 0, unroll=32)
        acc_ref[...] = jnp.zeros_like(acc_ref)

    hid = _dot_nt(xn_ref[...], u_ref[...])
    for p in range(a_per_chunk // 2):
        packed = gs_ref[pl.ds(c * (a_per_chunk // 2) + p, tb, stride=G_PITCH), :]
        al = (p // G_PAIR) * 2 * G_PAIR + p % G_PAIR
        for al_h, g_a in ((al, lax.bitcast_convert_type(packed & jnp.uint32(0xFFFF0000), F32)),
                          (al + G_PAIR, lax.bitcast_convert_type(packed << 16, F32))):
            act = _gelu_exact(hid[:, al_h * N_KEYS:(al_h + 1) * N_KEYS])
            w_ref[:, al_h * N_KEYS:(al_h + 1) * N_KEYS] = (g_a * act).astype(BF16)
    acc_ref[...] += jnp.dot(w_ref[...], v_ref[...], preferred_element_type=F32)

    @pl.when(c == pl.num_programs(1) - 1)
    def _finish():
        y = h_ref[...] + acc_ref[...]
        if norm_ref is not None:
            y = y * lax.rsqrt(jnp.mean(y * y, axis=-1, keepdims=True) + RMS_EPS) * norm_ref[...]
        o_ref[...] = y


def _peer_experts(xn, i1, i2, gate, h, u, v, layer, out_norm_g=None, tb=TOKEN_BLOCK, ec=EXPERT_CHUNK):
    t, d = h.shape
    n_exp = N_EXPERTS
    chunk0 = layer * (n_exp // ec)
    nsel = i1.shape[1]
    tok = lambda i, c: (i, 0)
    extra = [] if out_norm_g is None else [out_norm_g.reshape(1, d)]
    return pl.pallas_call(
        _peer_kernel,
        grid=(t // tb, n_exp // ec),
        in_specs=[pl.BlockSpec((tb, d), tok),
                  pl.BlockSpec((tb, nsel), tok),
                  pl.BlockSpec((tb, nsel), tok),
                  pl.BlockSpec((tb, nsel), tok),
                  pl.BlockSpec((tb, d), tok),
                  pl.BlockSpec((ec, d), lambda i, c: (chunk0 + c, 0)),
                  pl.BlockSpec((ec, d), lambda i, c: (chunk0 + c, 0))]
                 + [pl.BlockSpec((1, d), lambda i, c: (0, 0))] * len(extra),
        out_specs=pl.BlockSpec((tb, d), tok),
        out_shape=jax.ShapeDtypeStruct((t, d), F32),
        scratch_shapes=[pltpu.VMEM((tb * G_PITCH, N_KEYS), jnp.uint32),
                        pltpu.VMEM((tb, ec), BF16),
                        pltpu.VMEM((tb, d), F32)],
        compiler_params=_cparams(("parallel", "arbitrary")),
        name="peer_experts",
    )(xn, i1, i2, gate, h, u, v, *extra)


def _peer_ffn(h, g2, wq, subkeys, u, v, layer, out_norm_g=None):
    q, xn = _linear(h, wq.astype(BF16), g=g2, emit_xn=True, name="peer_query")
    i1, i2, gate = _route(q, subkeys)
    return _peer_experts(xn, i1, i2, gate, h, u, v, layer, out_norm_g)


def _band_bias(rows):
    r = np.arange(rows)[:, None]
    c = np.arange(ATT_WINDOW)[None, :]
    delta = ATT_BUF + r - c
    mult = np.zeros(delta.shape, np.float64)
    for window, dil in DILATED_GROUPS:
        mult += (delta >= 0) & (delta <= window) & (delta % dil == 0)
    return jnp.asarray(np.where(mult > 0, np.log(np.maximum(mult, 1.0)), NEG_INF), F32)


def _band_attend(q, kw, vw, bias):
    lane = lax.broadcasted_iota(jnp.int32, q.shape, 1)
    scores = [_dg(jnp.where((lane // HEAD_DIM) == half, q, jnp.zeros_like(q)), kw, _NT) for half in range(2)]
    outs = []
    for s in scores:
        s = s + bias
        m = jnp.max(s, axis=1, keepdims=True)
        p = jnp.exp(s - m)
        den = jnp.sum(p, axis=1, keepdims=True)
        outs.append(_dg(p.astype(BF16), vw, _NN) / den)
    return jnp.where((lane // HEAD_DIM) == 0, outs[0], outs[1])


def _attn_prompt_kernel(q_ref, k_ref, v_ref, bias_ref, o_ref, q_scr, k_scr, v_scr):
    s_len = q_ref.shape[0]
    zeros = jnp.zeros((ATT_BUF, LANES), BF16)
    k_scr[0:ATT_BUF, :] = zeros
    v_scr[0:ATT_BUF, :] = zeros
    k_scr[ATT_BUF:, :] = k_ref[...].astype(BF16)
    v_scr[ATT_BUF:, :] = v_ref[...].astype(BF16)
    q_scr[...] = (q_ref[...] * (HEAD_DIM ** -0.5)).astype(BF16)
    col = lax.broadcasted_iota(jnp.int32, (1, ATT_WINDOW), 1)

    def qblock(qi, carry):
        r0 = pl.multiple_of(qi * ATT_QBLOCK, ATT_QBLOCK)
        bias = jnp.where(col >= ATT_BUF - r0, bias_ref[...], NEG_INF)
        o_ref[pl.ds(r0, ATT_QBLOCK), :] = _band_attend(
            q_scr[pl.ds(r0, ATT_QBLOCK), :], k_scr[pl.ds(r0, ATT_WINDOW), :], v_scr[pl.ds(r0, ATT_WINDOW), :], bias)
        return carry

    lax.fori_loop(0, s_len // ATT_QBLOCK, qblock, 0)


_Q_BLOCK0 = POOL_WIDTH // LANES
_K_BLOCK0 = (POOL_WIDTH + ATT_WIDTH) // LANES
_V_BLOCK0 = (POOL_WIDTH + 2 * ATT_WIDTH) // LANES


def _attn_prompt(proj, n_seq, s_len):
    blk = lambda first: pl.BlockSpec((s_len, LANES), lambda b, hp: (b, first + hp))
    bias = _band_bias(ATT_QBLOCK)
    return pl.pallas_call(
        _attn_prompt_kernel,
        grid=(n_seq, ATT_WIDTH // LANES),
        in_specs=[blk(_Q_BLOCK0), blk(_K_BLOCK0), blk(_V_BLOCK0), pl.BlockSpec(bias.shape, lambda b, hp: (0, 0))],
        out_specs=pl.BlockSpec((s_len, LANES), lambda b, hp: (b, hp)),
        out_shape=jax.ShapeDtypeStruct((n_seq * s_len, ATT_WIDTH), F32),
        scratch_shapes=[pltpu.VMEM((s_len, LANES), BF16),
                        pltpu.VMEM((ATT_BUF + s_len, LANES), BF16),
                        pltpu.VMEM((ATT_BUF + s_len, LANES), BF16)],
        compiler_params=_cparams(("parallel", "parallel")),
        name="attn_prompt",
    )(proj, proj, proj, bias)


def _attn_sample_kernel(*refs):
    q_ref, k_ref, v_ref, ck_ref, cv_ref, bias_ref = refs[:6]
    o_ref, nk_ref, nv_ref, k_scr, v_scr = refs[-5:]
    t = q_ref.shape[0]
    n_past = ck_ref.shape[1]
    assert ATT_WINDOW - n_past == LANES

    def as_columns(x):
        return jnp.concatenate([x, jnp.zeros((LANES - t, LANES), F32)], axis=0).T

    new_k = as_columns(k_ref[...])
    new_v = as_columns(v_ref[...])
    k_scr[:, 0:n_past] = ck_ref[...].astype(BF16)
    v_scr[:, 0:n_past] = cv_ref[...].astype(BF16)
    k_scr[:, n_past:] = new_k.astype(BF16)
    v_scr[:, n_past:] = new_v.astype(BF16)
    q = (q_ref[...] * (HEAD_DIM ** -0.5)).astype(BF16)
    lane = lax.broadcasted_iota(jnp.int32, q.shape, 1)
    scores = [_dg(jnp.where((lane // HEAD_DIM) == half, q, jnp.zeros_like(q)), k_scr[...], _NN) for half in range(2)]
    outs = []
    for s in scores:
        s = s + bias_ref[...]
        m = jnp.max(s, axis=1, keepdims=True)
        p = jnp.exp(s - m)
        den = jnp.sum(p, axis=1, keepdims=True)
        outs.append(_dg(p.astype(BF16), v_scr[...], _NT) / den)
    o_ref[...] = jnp.where((lane // HEAD_DIM) == 0, outs[0], outs[1])

    col = lax.broadcasted_iota(jnp.int32, (LANES, LANES), 1)
    for old_ref, new_cols, out_ref in ((ck_ref, new_k, nk_ref), (cv_ref, new_v, nv_ref)):
        shifted = pltpu.roll(old_ref[...], n_past - t, 1)
        out_ref[:, 0:n_past - LANES] = shifted[:, 0:n_past - LANES]
        out_ref[:, n_past - LANES:] = jnp.where(col < LANES - t, shifted[:, n_past - LANES:],
                                                 pltpu.roll(new_cols, LANES - t, 1))


def _attn_sample(proj, row0, n_seq, t_len, cache_k, cache_v, layer, depth, stacks):
    assert row0 % t_len == 0 and t_len % 8 == 0 and t_len <= ATT_QBLOCK
    blk0 = row0 // t_len
    pairs = ATT_WIDTH // LANES
    blk = lambda first: pl.BlockSpec((t_len, LANES), lambda b, hp: (blk0 + b, first + hp))
    cblk = pl.BlockSpec((LANES, ATT_BUF), lambda b, hp: ((layer * n_seq + b) * pairs + hp, 0))
    bias = _band_bias(t_len)
    stack_shape = jax.ShapeDtypeStruct(cache_k.shape, F32)
    carried = [] if stacks is None else list(stacks)
    return pl.pallas_call(
        _attn_sample_kernel,
        grid=(n_seq, pairs),
        in_specs=[blk(_Q_BLOCK0), blk(_K_BLOCK0), blk(_V_BLOCK0), cblk, cblk,
                  pl.BlockSpec(bias.shape, lambda b, hp: (0, 0))]
                 + [pl.BlockSpec(memory_space=pl.ANY)] * len(carried),
        out_specs=[pl.BlockSpec((t_len, LANES), lambda b, hp: (b, hp)), cblk, cblk],
        out_shape=[jax.ShapeDtypeStruct((n_seq * t_len, ATT_WIDTH), F32), stack_shape, stack_shape],
        input_output_aliases={6 + j: 1 + j for j in range(len(carried))},
        scratch_shapes=[pltpu.VMEM((LANES, ATT_WINDOW), BF16), pltpu.VMEM((LANES, ATT_WINDOW), BF16)],
        compiler_params=_cparams(("parallel", "parallel")),
        name="attn_sample",
    )(proj, proj, proj, cache_k, cache_v, bias, *carried)


def _pool_kernel(x_ref, buf_ref, w_ref, scale_ref, o_ref, *, pos0):
    x = x_ref[...]
    n = x.shape[0]
    xf = jnp.concatenate([buf_ref[...], x], axis=0)
    s2 = xf + pltpu.roll(xf, 1, 0)
    s4 = s2 + pltpu.roll(s2, 2, 0)
    s8 = s4 + pltpu.roll(s4, 4, 0)
    s16 = s8 + pltpu.roll(s8, 8, 0)
    group = lax.broadcasted_iota(jnp.int32, (1, POOL_WIDTH), 1) // POOL_GROUP
    sums = jnp.where(group == 0, s2, jnp.where(group == 1, s4, jnp.where(group == 2, s8, s16)))[POOL_BUF + 1:]
    win = jnp.where(group == 0, 2.0, jnp.where(group == 1, 4.0, jnp.where(group == 2, 8.0, 16.0)))
    seen = (lax.broadcasted_iota(jnp.int32, (n, 1), 0) + (pos0 + 1)).astype(F32)
    diff = sums / jnp.minimum(win, seen) - x
    o_ref[...] = _mm3(diff, w_ref[...]) * scale_ref[...]


def _pool(proj, row0, n_seq, s_len, buf, w, scale, pos0):
    assert POOL_WINDOWS == (2, 4, 8, 16) and row0 % s_len == 0
    blk0 = row0 // s_len
    wbd = jnp.zeros((POOL_WIDTH, POOL_WIDTH), F32)
    for i in range(len(POOL_WINDOWS)):
        wbd = wbd.at[i * POOL_GROUP:(i + 1) * POOL_GROUP, i * POOL_GROUP:(i + 1) * POOL_GROUP].set(w[i])
    buf16 = jnp.pad(buf, ((0, 0), (1, 0), (0, 0))).reshape(n_seq * (POOL_BUF + 1), POOL_WIDTH)
    return pl.pallas_call(
        functools.partial(_pool_kernel, pos0=pos0),
        grid=(n_seq,),
        in_specs=[pl.BlockSpec((s_len, POOL_WIDTH), lambda b: (blk0 + b, 0)),
                  pl.BlockSpec((POOL_BUF + 1, POOL_WIDTH), lambda b: (b, 0)),
                  pl.BlockSpec((POOL_WIDTH, POOL_WIDTH), lambda b: (0, 0)),
                  pl.BlockSpec((1, POOL_WIDTH), lambda b: (0, 0))],
        out_specs=pl.BlockSpec((s_len, POOL_WIDTH), lambda b: (b, 0)),
        out_shape=jax.ShapeDtypeStruct((n_seq * s_len, POOL_WIDTH), F32),
        compiler_params=_cparams(("parallel",)),
        name="pool_mixer",
    )(proj, buf16, wbd, scale.reshape(1, POOL_WIDTH))


def _split(x):
    hi = x.astype(BF16)
    return hi, (x - hi.astype(F32)).astype(BF16)


def _dg(a, b, dims):
    return lax.dot_general(a, b, (dims, ((), ())), preferred_element_type=F32)


_NN = ((1,), (0,))
_NT = ((1,), (1,))
_TN = ((0,), (0,))


def _mm3(a, b, dims=_NN):
    a_hi, a_lo = _split(a)
    b_hi, b_lo = _split(b)
    return _dg(a_hi, b_hi, dims) + (_dg(a_lo, b_hi, dims) + _dg(a_hi, b_lo, dims))


def _split3(x):
    x0 = x.astype(BF16)
    r1 = x - x0.astype(F32)
    x1 = r1.astype(BF16)
    return x0, x1, (r1 - x1.astype(F32)).astype(BF16)


def _mm_exact_rhs(a, b_bf16):
    a0, a1, a2 = _split3(a)
    return _dg(a0, b_bf16, _NN) + (_dg(a1, b_bf16, _NN) + _dg(a2, b_bf16, _NN))


def _mm_exact_lhs(a_bf16, b):
    b0, b1, b2 = _split3(b)
    return _dg(a_bf16, b0, _NN) + (_dg(a_bf16, b1, _NN) + _dg(a_bf16, b2, _NN))


def _rwkv_chunk(pblks, prev_rows, s0, prm):
    (mu, w0, w_up, a0, a_up, g_up, k_k, k_a, r_k, ln_w, ln_b) = prm
    n_seq = len(pblks)
    c = pblks[0].shape[0]
    n = n_seq * c
    pblk = jnp.concatenate(pblks, axis=0) if n_seq > 1 else pblks[0]
    shifted = [jnp.concatenate([prev_rows[i], pblks[i][:-1]], axis=0) if c > 1 else prev_rows[i]
               for i in range(n_seq)]
    shifted = jnp.concatenate(shifted, axis=0) if n_seq > 1 else shifted[0]
    ps = pblk + (shifted - pblk) * mu
    c1, c2, c3 = RWKV_WIDTH, 2 * RWKV_WIDTH, 3 * RWKV_WIDTH
    c4 = c3 + DECAY_LORA
    c5 = c4 + ICLR_LORA
    r, k, v = ps[:, :c1], ps[:, c1:c2], ps[:, c2:c3]
    wd, ad, gd = ps[:, c3:c4], ps[:, c4:c5], ps[:, c5:]

    z = -(w0 + _mm3(jnp.tanh(wd), w_up))
    softplus = jnp.maximum(z, 0.0) + jnp.log(1.0 + jnp.exp(-jnp.abs(z)))
    lw = -jnp.exp(-softplus - 0.5)
    a = jax.nn.sigmoid(a0 + _mm3(ad, a_up))
    g = _mm3(jax.nn.sigmoid(gd), g_up)

    ch_r = lax.broadcasted_iota(jnp.int32, (RWKV_WIDTH, RWKV_WIDTH), 0) // HEAD_DIM
    ch_c = lax.broadcasted_iota(jnp.int32, (RWKV_WIDTH, RWKV_WIDTH), 1) // HEAD_DIM
    same_head = jnp.where(ch_r == ch_c, 1.0, 0.0).astype(BF16)

    kk = k * k_k
    kk = kk / jnp.maximum(jnp.sqrt(_mm_exact_rhs(kk * kk, same_head)), 1e-12)
    k2 = k * (1.0 + (a - 1.0) * k_a)
    bonus = _mm_exact_rhs(r * k2 * r_k, same_head) * v

    n_r = lax.broadcasted_iota(jnp.int32, (n, n), 0)
    n_c = lax.broadcasted_iota(jnp.int32, (n, n), 1)
    run = (n_r // c == n_c // c) & (n_c <= n_r)
    cs = _mm_exact_lhs(jnp.where(run, 1.0, 0.0).astype(BF16), lw)
    e_up = jnp.exp(cs)
    e_dn = jnp.exp(-cs)
    kap = kk * jnp.exp(cs - lw)
    bet = kk * a * e_dn
    kt = k2 * e_dn
    rt = r * e_up

    t_r = lax.broadcasted_iota(jnp.int32, (c, c), 0)
    t_c = lax.broadcasted_iota(jnp.int32, (c, c), 1)
    incl = t_c <= t_r
    strict = t_c < t_r
    eye = jnp.where(t_r == t_c, 1.0, 0.0)
    chains = [(i, h) for i in range(n_seq) for h in range(RWKV_HEADS)]
    cut = lambda x, i, h: x[i * c:(i + 1) * c, h * HEAD_DIM:(h + 1) * HEAD_DIM]
    kp_x = [cut(kap, i, h) for i, h in chains]
    bt_x = [cut(bet, i, h) for i, h in chains]
    kt_x = [cut(kt, i, h) for i, h in chains]
    rt_x = [cut(rt, i, h) for i, h in chains]
    v_x = [cut(v, i, h) for i, h in chains]
    s_x = [s0[i][h] for i, h in chains]
    every = range(len(chains))
    quad = [_mm3(jnp.concatenate([kp_x[x], rt_x[x]], axis=0), jnp.concatenate([bt_x[x], kt_x[x]], axis=0), _NT)
            for x in every]
    a_b = [jnp.where(strict, quad[x][:c, :c], 0.0) for x in every]
    a_k = [jnp.where(strict, quad[x][:c, c:], 0.0) for x in every]
    m_b = [jnp.where(incl, quad[x][c:, :c], 0.0) for x in every]
    m_k = [jnp.where(incl, quad[x][c:, c:], 0.0) for x in every]
    akv = [_mm3(a_k[x], v_x[x]) for x in every]
    mkv = [_mm3(m_k[x], v_x[x]) for x in every]
    rs = [_mm3(rt_x[x], s_x[x], _NT) for x in every]
    vk = [_mm3(v_x[x], kt_x[x], _TN) for x in every]
    inv = [eye - jnp.where(t_r // 2 == t_c // 2, a_b[x], 0.0) for x in every]
    b = 2
    while b < c:
        pair = (t_r // (2 * b) == t_c // (2 * b)) & ((t_r // b) % 2 == 1) & ((t_c // b) % 2 == 0)
        low = [_mm3(jnp.where(pair, a_b[x], 0.0), inv[x]) for x in every]
        inv = [inv[x] - _mm3(inv[x], low[x]) for x in every]
        b *= 2
    pq = [_mm3(inv[x], jnp.concatenate([kp_x[x], akv[x]], axis=1)) for x in every]
    p_x = [pq[x][:, :HEAD_DIM] for x in every]
    q_x = [pq[x][:, HEAD_DIM:] for x in every]
    u_x = [_mm3(p_x[x], s_x[x], _NT) + q_x[x] for x in every]
    o_x = [rs[x] + mkv[x] - _mm3(m_b[x], u_x[x]) for x in every]
    ub = [_mm3(u_x[x], bt_x[x], _TN) for x in every]
    s_new = [[None] * RWKV_HEADS for _ in range(n_seq)]
    for x, (i, h) in enumerate(chains):
        gam = e_up[(i + 1) * c - 1:(i + 1) * c, h * HEAD_DIM:(h + 1) * HEAD_DIM]
        s_new[i][h] = (s_x[x] + vk[x] - ub[x]) * gam
    o = [jnp.concatenate(o_x[i * RWKV_HEADS:(i + 1) * RWKV_HEADS], axis=1) for i in range(n_seq)]
    o = jnp.concatenate(o, axis=0) if n_seq > 1 else o[0]
    mean = _mm_exact_rhs(o, same_head) * (1.0 / HEAD_DIM)
    d = o - mean
    var = _mm_exact_rhs(d * d, same_head) * (1.0 / HEAD_DIM)
    o = (d * lax.rsqrt(var + GN_EPS) * ln_w + ln_b + bonus) * g
    return [o[i * c:(i + 1) * c] for i in range(n_seq)], s_new


def _rwkv_kernel(*refs, rows):
    p_refs = refs[:rows]
    (prev_ref, s0_ref, mu_ref, w0_ref, wup_ref, a0_ref, aup_ref, gup_ref, kk_ref, ka_ref, rk_ref, lnw_ref, lnb_ref,
     o_ref, sT_ref, s_scr, last_scr) = refs[rows:]
    ci = pl.program_id(1)

    @pl.when(ci == 0)
    def _init():
        s_scr[...] = s0_ref[...]
        last_scr[...] = prev_ref[...]

    prm = (mu_ref[...], w0_ref[...], wup_ref[...], a0_ref[...], aup_ref[...], gup_ref[...], kk_ref[...],
           ka_ref[...], rk_ref[...], lnw_ref[...], lnb_ref[...])
    pblks = [p_refs[i][...] for i in range(rows)]
    outs, s_new = _rwkv_chunk(pblks, [last_scr[i] for i in range(rows)],
                              [[s_scr[i, h] for h in range(RWKV_HEADS)] for i in range(rows)], prm)
    c = pblks[0].shape[0]
    for i in range(rows):
        o_ref[i] = outs[i]
        for h in range(RWKV_HEADS):
            s_scr[i, h] = s_new[i][h]
        last_scr[i] = pblks[i][c - 1:c, :]

    @pl.when(ci == pl.num_programs(1) - 1)
    def _fin():
        sT_ref[...] = s_scr[...]


def _rwkv_mixer(proj, row0, n_seq, s_len, prev, wkv0, lp, chunk, rows=RWKV_ROWS):
    assert s_len % chunk == 0 and n_seq % rows == 0 and row0 % chunk == 0
    assert proj.shape[1] == 2 * RWKV_PROJ
    n_chunks = s_len // chunk
    blk0 = row0 // chunk
    row = lambda x: x.reshape(1, -1).astype(F32)
    full = lambda a: pl.BlockSpec(a.shape, lambda b, c: (0,) * a.ndim)
    weights = [row(lp['rwkv_mu']), row(lp['rwkv_w0']), lp['rwkv_w_up'], row(lp['rwkv_a0']), lp['rwkv_a_up'],
               lp['rwkv_g_up'], row(lp['rwkv_k_k']), row(lp['rwkv_k_a']), row(lp['rwkv_r_k']),
               row(lp['rwkv_ln_w']), row(lp['rwkv_ln_b'])]
    p_spec = lambda i: pl.BlockSpec((chunk, RWKV_PROJ), lambda b, c: (blk0 + (b * rows + i) * n_chunks + c, 1))
    out, s_t = pl.pallas_call(
        functools.partial(_rwkv_kernel, rows=rows),
        grid=(n_seq // rows, n_chunks),
        in_specs=[p_spec(i) for i in range(rows)]
                 + [pl.BlockSpec((rows, 1, RWKV_PROJ), lambda b, c: (b, 0, 0)),
                    pl.BlockSpec((rows, RWKV_HEADS, HEAD_DIM, HEAD_DIM), lambda b, c: (b, 0, 0, 0))]
                 + [full(w) for w in weights],
        out_specs=[pl.BlockSpec((rows, chunk, RWKV_WIDTH), lambda b, c: (b, c, 0)),
                   pl.BlockSpec((rows, RWKV_HEADS, HEAD_DIM, HEAD_DIM), lambda b, c: (b, 0, 0, 0))],
        out_shape=[jax.ShapeDtypeStruct((n_seq, s_len, RWKV_WIDTH), F32),
                   jax.ShapeDtypeStruct((n_seq, RWKV_HEADS, HEAD_DIM, HEAD_DIM), F32)],
        scratch_shapes=[pltpu.VMEM((rows, RWKV_HEADS, HEAD_DIM, HEAD_DIM), F32),
                        pltpu.VMEM((rows, 1, RWKV_PROJ), F32)],
        compiler_params=_cparams(("parallel", "arbitrary")),
        name="rwkv_mixer",
    )(*([proj] * rows), prev.reshape(n_seq, 1, RWKV_PROJ), wkv0, *weights)
    return out.reshape(n_seq * s_len, RWKV_WIDTH), s_t


def kernel(x_prompt, x_sample, state_pool, cache_k, cache_v, state_shift, state_wkv, norm1_g, norm2_g, final_g, w_in, w_out, pool_w, pool_scale, rwkv_mu, rwkv_w0, rwkv_w_up, rwkv_a0, rwkv_a_up, rwkv_g_up, rwkv_k_k, rwkv_k_a, rwkv_r_k, rwkv_ln_w, rwkv_ln_b, peer_wq, peer_subkeys, peer_u, peer_v):
    bp, sp_len, d = x_prompt.shape
    bs, ss_len, _ = x_sample.shape
    tp = bp * sp_len
    ts = bs * ss_len
    depth = w_in.shape[0]
    keep = min(ATT_BUF, sp_len)
    h = jnp.concatenate([x_prompt.reshape(tp, d), x_sample.reshape(ts, d)], axis=0)
    pos = jnp.concatenate([jnp.tile(jnp.arange(sp_len, dtype=jnp.int32), bp),
                           jnp.tile(PAST_LEN + jnp.arange(ss_len, dtype=jnp.int32), bs)])
    rope_tabs = _rope_tables(pos)
    zero_buf = jnp.zeros((bp, POOL_BUF, POOL_WIDTH), F32)
    zero_shift = jnp.zeros((bp, RWKV_PROJ), F32)
    zero_wkv = jnp.zeros((bp, RWKV_HEADS, HEAD_DIM, HEAD_DIM), F32)
    k_lo, v_lo, r_lo = POOL_WIDTH + ATT_WIDTH, POOL_WIDTH + 2 * ATT_WIDTH, POOL_WIDTH + 3 * ATT_WIDTH
    heads = lambda x, n: x.reshape(n, -1, ATT_HEADS, HEAD_DIM)
    outs_p = [[] for _ in range(5)]
    outs_s = [[] for _ in range(5)]
    kv_stacks = None
    feature_major = lambda x: jnp.transpose(x, (0, 1, 3, 4, 2)).reshape(depth * bs * ATT_WIDTH, ATT_BUF)
    ck_all = feature_major(cache_k)
    cv_all = feature_major(cache_v)
    u_all = peer_u.astype(BF16).reshape(depth * N_EXPERTS, d)
    v_all = peer_v.astype(BF16).reshape(depth * N_EXPERTS, d)
    for l in range(depth):
        lp = {'rwkv_mu': rwkv_mu[l], 'rwkv_w0': rwkv_w0[l],
              'rwkv_w_up': rwkv_w_up[l], 'rwkv_a0': rwkv_a0[l], 'rwkv_a_up': rwkv_a_up[l],
              'rwkv_g_up': rwkv_g_up[l], 'rwkv_k_k': rwkv_k_k[l], 'rwkv_k_a': rwkv_k_a[l],
              'rwkv_r_k': rwkv_r_k[l], 'rwkv_ln_w': rwkv_ln_w[l], 'rwkv_ln_b': rwkv_ln_b[l]}
        proj = _linear(h, w_in[l].astype(BF16), g=norm1_g[l], rope=rope_tabs, name="in_proj")
        proj_s = proj[tp:].reshape(bs, ss_len, IN_WIDTH)

        pool_p = _pool(proj, 0, bp, sp_len, zero_buf, pool_w[l], pool_scale[l], 0)
        pool_s = _pool(proj, tp, bs, ss_len, state_pool[l], pool_w[l], pool_scale[l], PAST_LEN)
        att_p = _attn_prompt(proj, bp, sp_len)
        att_s, *kv_stacks = _attn_sample(proj, tp, bs, ss_len, ck_all, cv_all, l, depth, kv_stacks)
        rw_p, wkv_p = _rwkv_mixer(proj, 0, bp, sp_len, zero_shift, zero_wkv, lp, min(sp_len, RWKV_CHUNK))
        rw_s, wkv_s = _rwkv_mixer(proj, tp, bs, ss_len, state_shift[l], state_wkv[l], lp, min(ss_len, RWKV_CHUNK))

        tail = lambda n, lo, hi: jnp.stack([proj[(b + 1) * sp_len - n:(b + 1) * sp_len, lo:hi] for b in range(bp)])
        outs_p[0].append(tail(POOL_BUF, 0, POOL_WIDTH))
        outs_p[1].append(heads(tail(keep, k_lo, v_lo), bp))
        outs_p[2].append(heads(tail(keep, v_lo, r_lo), bp))
        outs_p[3].append(tail(1, r_lo, IN_WIDTH)[:, 0])
        outs_p[4].append(wkv_p)
        outs_s[0].append(jnp.concatenate([state_pool[l], proj_s[:, :, :POOL_WIDTH]], axis=1)[:, -POOL_BUF:])
        outs_s[3].append(proj_s[:, -1, r_lo:])
        outs_s[4].append(wkv_s)

        mixed = jnp.concatenate([jnp.concatenate([pool_p, att_p, rw_p], axis=1),
                                 jnp.concatenate([pool_s, att_s, rw_s], axis=1)], axis=0)
        h = _linear(mixed, w_out[l].astype(BF16), resid=h, name="out_proj")
        h = _peer_ffn(h, norm2_g[l], peer_wq[l], peer_subkeys[l], u_all, v_all, l,
                      final_g if l == depth - 1 else None)
    y = h
    y_prompt = y[:tp].reshape(bp, sp_len, d)
    y_sample = y[tp:].reshape(bs, ss_len, d)
    s_k, s_v = (jnp.transpose(x.reshape(depth, bs, ATT_HEADS, HEAD_DIM, ATT_BUF), (0, 1, 4, 2, 3)) for x in kv_stacks)
    return (y_prompt, y_sample, *[jnp.stack(a) for a in outs_p],
            jnp.stack(outs_s[0]), s_k, s_v, jnp.stack(outs_s[3]), jnp.stack(outs_s[4]))
```

```python
import functools
import math

import jax
import jax.numpy as jnp
import numpy as np
from jax import lax
from jax.experimental import pallas as pl
from jax.experimental.pallas import tpu as pltpu

F32 = jnp.float32
BF16 = jnp.bfloat16

D_MODEL = 1024
HEAD_DIM = 64
POOL_WINDOWS = (2, 4, 8, 16)
POOL_GROUP = 64
POOL_WIDTH = 256
POOL_BUF = 15
ATT_WIDTH = 384
ATT_HEADS = 6
DILATED_GROUPS = ((128, 1), (512, 4), (2048, 16))
ATT_BUF = 2048
ROPE_DIM = 16
ROPE_THETA = 500000.0
NEG_INF = -1e30
RWKV_WIDTH = 384
RWKV_HEADS = 6
DECAY_LORA = 64
ICLR_LORA = 64
GATE_LORA = 128
RWKV_PROJ = 1408
GN_EPS = HEAD_DIM * 1e-5
N_KEYS = 128
N_EXPERTS = N_KEYS * N_KEYS
PEER_HEADS = 8
PEER_TOPK = 16
RMS_EPS = 1e-6

TOKEN_BLOCK = 256
EXPERT_CHUNK = 4096
G_PAIR = 8
G_PITCH = 72
RWKV_CHUNK = 64
RWKV_ROWS = 2
LANES = 128
IN_WIDTH = POOL_WIDTH + 3 * ATT_WIDTH + RWKV_PROJ
ATT_QBLOCK = 128
ATT_WINDOW = ATT_BUF + ATT_QBLOCK
PAST_LEN = 8192
VMEM_LIMIT = 56 * 1024 * 1024


def _cparams(sem):
    return pltpu.CompilerParams(dimension_semantics=sem, vmem_limit_bytes=VMEM_LIMIT)


def _rope_tables(pos):
    half = ROPE_DIM // 2
    inv = ROPE_THETA ** (-jnp.arange(half, dtype=F32) * 2.0 / ROPE_DIM)
    ang = pos.astype(F32)[:, None] * inv[None, :]
    cos, sin = jnp.cos(ang), jnp.sin(ang)
    n = pos.shape[0]
    one = jnp.ones((n, HEAD_DIM - ROPE_DIM), F32)
    zero = jnp.zeros((n, HEAD_DIM - ROPE_DIM), F32)
    zh = jnp.zeros((n, half), F32)
    c = jnp.concatenate([cos, cos, one], axis=1)
    a = jnp.concatenate([-sin, zh, zero], axis=1)
    b = jnp.concatenate([zh, sin, zero], axis=1)
    tile = lambda x: jnp.concatenate([x, x], axis=1)
    return tile(c), tile(a), tile(b)


def _rope_lanes(x, c, a, b):
    half = ROPE_DIM // 2
    return x * c + pltpu.roll(x, LANES - half, 1) * a + pltpu.roll(x, half, 1) * b


def _linear_kernel(*refs, norm, resid, emit_xn, rope):
    it = iter(refs)
    x_ref = next(it)
    g_ref = next(it) if norm else None
    w_ref = next(it)
    r_ref = next(it) if resid else None
    rope_refs = [next(it) for _ in range(3)] if rope else None
    o_ref = next(it)
    xn_ref = next(it) if emit_xn else None

    x = x_ref[...].astype(F32)
    if norm:
        x = x * lax.rsqrt(jnp.mean(x * x, axis=-1, keepdims=True) + RMS_EPS) * g_ref[...]
    if emit_xn:
        xn_ref[...] = x.astype(xn_ref.dtype)
    acc = jnp.dot(x.astype(BF16), w_ref[...], preferred_element_type=F32)
    if resid:
        acc = acc + r_ref[...]
    if rope:
        c, a, b = (r[...] for r in rope_refs)
        lo, hi = POOL_WIDTH, POOL_WIDTH + 2 * ATT_WIDTH
        o_ref[:, :lo] = acc[:, :lo]
        for j in range(lo, hi, LANES):
            o_ref[:, j:j + LANES] = _rope_lanes(acc[:, j:j + LANES], c, a, b)
        o_ref[:, hi:] = acc[:, hi:]
    else:
        o_ref[...] = acc


def _linear(x, w, *, g=None, resid=None, emit_xn=False, rope=None, tb=TOKEN_BLOCK, name="linear"):
    t, k = x.shape
    n = w.shape[1]
    assert t % tb == 0
    norm = g is not None
    ins = [x]
    specs = [pl.BlockSpec((tb, k), lambda i: (i, 0))]
    if norm:
        ins.append(g.reshape(1, k).astype(F32))
        specs.append(pl.BlockSpec((1, k), lambda i: (0, 0)))
    ins.append(w)
    specs.append(pl.BlockSpec((k, n), lambda i: (0, 0)))
    if resid is not None:
        ins.append(resid)
        specs.append(pl.BlockSpec((tb, n), lambda i: (i, 0)))
    if rope is not None:
        ins.extend(rope)
        specs.extend([pl.BlockSpec((tb, LANES), lambda i: (i, 0))] * 3)
    out_shape = [jax.ShapeDtypeStruct((t, n), F32)]
    out_specs = [pl.BlockSpec((tb, n), lambda i: (i, 0))]
    if emit_xn:
        out_shape.append(jax.ShapeDtypeStruct((t, k), BF16))
        out_specs.append(pl.BlockSpec((tb, k), lambda i: (i, 0)))
    outs = pl.pallas_call(
        functools.partial(_linear_kernel, norm=norm, resid=resid is not None, emit_xn=emit_xn,
                          rope=rope is not None),
        grid=(t // tb,),
        in_specs=specs,
        out_specs=out_specs,
        out_shape=out_shape,
        compiler_params=_cparams(("parallel",)),
        name=name,
    )(*ins)
    return outs if emit_xn else outs[0]


def _dot_nt(a, b):
    return lax.dot_general(a, b, (((1,), (1,)), ((), ())), preferred_element_type=F32)


def _oddeven_merge(lo, hi, r):
    step = r * 2
    if step < hi - lo:
        yield from _oddeven_merge(lo, hi, step)
        yield from _oddeven_merge(lo + r, hi, step)
        yield from [(i, i + r) for i in range(lo + r, hi - r, step)]
    else:
        yield (lo, lo + r)


def _oddeven_sort(lo, hi):
    if hi - lo >= 1:
        mid = lo + (hi - lo) // 2
        yield from _oddeven_sort(lo, mid)
        yield from _oddeven_sort(mid + 1, hi)
        yield from _oddeven_merge(lo, hi, 1)


_SORT16 = tuple(_oddeven_sort(0, PEER_TOPK - 1))
_BITONIC16 = tuple((i, i + d) for d in (8, 4, 2, 1) for i in range(PEER_TOPK) if not i & d)
SUBLANES = 8


def _exchange(v, p, i, j):
    keep = v[i] >= v[j]
    v[i], v[j] = jnp.where(keep, v[i], v[j]), jnp.where(keep, v[j], v[i])
    p[i], p[j] = jnp.where(keep, p[i], p[j]), jnp.where(keep, p[j], p[i])


def _top16(v, p, n_real):
    v, p = list(v), list(p)
    for i, j in _SORT16:
        if j < n_real:
            _exchange(v, p, i, j)
    for shift in (4, 2, 1):
        vb = [pltpu.roll(x, shift, 0) for x in v]
        pb = [pltpu.roll(x, shift, 0) for x in p]
        for i in range(PEER_TOPK):
            keep = v[i] >= vb[PEER_TOPK - 1 - i]
            v[i] = jnp.where(keep, v[i], vb[PEER_TOPK - 1 - i])
            p[i] = jnp.where(keep, p[i], pb[PEER_TOPK - 1 - i])
        for i, j in _BITONIC16:
            _exchange(v, p, i, j)
    return v, p


def _rows16(xs):
    sub = lax.broadcasted_iota(jnp.int32, xs[0].shape, 0)
    halves = []
    for base in (0, SUBLANES):
        acc = xs[base]
        for k in range(1, SUBLANES):
            acc = jnp.where(sub == k, xs[base + k], acc)
        halves.append(acc)
    return jnp.concatenate(halves, axis=0)


def _route_kernel(q_ref, sk_ref, i1_ref, i2_ref, gate_ref, tv_ref, ti_ref, e1_ref, e2_ref, gt_ref):
    tb = q_ref.shape[0]
    neg = jnp.float32(-jnp.inf)
    sub_iota = lax.broadcasted_iota(jnp.int32, (SUBLANES, tb), 0).astype(F32)

    def half_topk(hc, carry):
        off = pl.multiple_of(hc * N_KEYS, N_KEYS)
        qh = q_ref[:, pl.ds(off, N_KEYS)]
        sk = sk_ref[hc]
        q_hi = qh.astype(BF16)
        q_lo = (qh - q_hi.astype(F32)).astype(BF16)
        s_hi = sk.astype(BF16)
        s_lo = (sk - s_hi.astype(F32)).astype(BF16)
        sc = _dot_nt(s_hi, q_hi) + _dot_nt(s_lo, q_hi) + _dot_nt(s_hi, q_lo)
        n_slab = N_KEYS // SUBLANES
        vals = [sc[j * SUBLANES:(j + 1) * SUBLANES] for j in range(n_slab)]
        idxs = [sub_iota + float(j * SUBLANES) for j in range(n_slab)]
        vals, idxs = _top16(vals, idxs, n_slab)
        tv_ref[hc] = _rows16(vals)
        ti_ref[hc] = _rows16(idxs)
        return carry

    lax.fori_loop(0, 2 * PEER_HEADS, half_topk, 0)

    row8 = lax.broadcasted_iota(jnp.int32, (SUBLANES, tb), 0)

    def pair_topk(h, carry):
        a = tv_ref[2 * h]
        b = tv_ref[2 * h + 1]
        ia = ti_ref[2 * h] * float(N_KEYS)
        ib = ti_ref[2 * h + 1]
        vs = [a[0:1] + b[0:8], a[0:1] + b[8:16]]
        code = [ia[0:1] + ib[0:8], ia[0:1] + ib[8:16]]
        for i in range(1, 8):
            lim = PEER_TOPK // (i + 1)
            v = a[i:i + 1] + b[0:8]
            vs.append(jnp.where(row8 < lim, v, neg) if lim < 8 else v)
            code.append(ia[i:i + 1] + ib[0:8])
        vs.append(a[8:16] + b[0:1])
        code.append(ia[8:16] + ib[0:1])
        n_real = len(vs)
        pad_v = jnp.full((SUBLANES, tb), neg, F32)
        pad_c = jnp.zeros((SUBLANES, tb), F32)
        vs += [pad_v] * (PEER_TOPK - n_real)
        code += [pad_c] * (PEER_TOPK - n_real)
        tops, code = _top16(vs, code, n_real)
        e = [jnp.exp(t - tops[0]) for t in tops]
        total = e[0]
        for x in e[1:]:
            total = total + x
        first = [jnp.floor(cd * (1.0 / N_KEYS)) for cd in code]
        second = [cd - f * float(N_KEYS) for cd, f in zip(code, first)]
        row = pl.multiple_of(h * PEER_TOPK, PEER_TOPK)
        gt_ref[pl.ds(row, PEER_TOPK), :] = _rows16([x / total for x in e])
        e1_ref[pl.ds(row, PEER_TOPK), :] = _rows16(first)
        e2_ref[pl.ds(row, PEER_TOPK), :] = _rows16(second)
        return carry

    lax.fori_loop(0, PEER_HEADS, pair_topk, 0)
    i1_ref[...] = e1_ref[...].T
    i2_ref[...] = e2_ref[...].T
    gate_ref[...] = gt_ref[...].T


def _route(q, subkeys, tb=TOKEN_BLOCK):
    t = q.shape[0]
    nsel = PEER_HEADS * PEER_TOPK
    sk = subkeys.reshape(2 * PEER_HEADS, N_KEYS, N_KEYS)
    out = jax.ShapeDtypeStruct((t, nsel), F32)
    spec = pl.BlockSpec((tb, nsel), lambda i: (i, 0))
    return pl.pallas_call(
        _route_kernel,
        grid=(t // tb,),
        in_specs=[pl.BlockSpec((tb, q.shape[1]), lambda i: (i, 0)),
                  pl.BlockSpec(sk.shape, lambda i: (0, 0, 0))],
        out_specs=[spec, spec, spec],
        out_shape=[out, out, out],
        scratch_shapes=[pltpu.VMEM((2 * PEER_HEADS, PEER_TOPK, tb), F32),
                        pltpu.VMEM((2 * PEER_HEADS, PEER_TOPK, tb), F32),
                        pltpu.VMEM((nsel, tb), F32),
                        pltpu.VMEM((nsel, tb), F32),
                        pltpu.VMEM((nsel, tb), F32)],
        compiler_params=_cparams(("parallel",)),
        name="peer_route",
    )(q, sk)


def _gelu_exact(x):
    return 0.5 * x * (1.0 + lax.erf(x * (1.0 / math.sqrt(2.0))))


def _store_gate_tile(i1_ref, i2_ref, gate_ref, t, gs_ref, key_iota):
    r1 = i1_ref[pl.ds(t, 1), :]
    r2 = i2_ref[pl.ds(t, 1), :]
    gr = gate_ref[pl.ds(t, 1), :]
    a_t = jnp.where(key_iota == r1, gr, 0.0).astype(BF16)
    b_t = jnp.where(key_iota == r2, 1.0, 0.0).astype(BF16)
    g = _dot_nt(a_t, b_t)
    groups = range(0, N_KEYS, 2 * G_PAIR)
    hi = jnp.concatenate([g[r:r + G_PAIR] for r in groups], axis=0)
    lo = jnp.concatenate([g[r + G_PAIR:r + 2 * G_PAIR] for r in groups], axis=0)
    bits = lambda x: lax.bitcast_convert_type(x.astype(BF16).astype(F32), jnp.uint32)
    row = pl.multiple_of(t * G_PITCH, SUBLANES)
    gs_ref[pl.ds(row, N_KEYS // 2), :] = bits(hi) | (bits(lo) >> 16)


def _peer_kernel(xn_ref, i1_ref, i2_ref, gate_ref, h_ref, u_ref, v_ref, *rest):
    norm_ref = rest[0] if len(rest) == 5 else None
    o_ref, gs_ref, w_ref, acc_ref = rest[-4:]
    c = pl.program_id(1)
    tb = xn_ref.shape[0]
    ec = u_ref.shape[0]
    a_per_chunk = ec // N_KEYS
    assert a_per_chunk % (2 * G_PAIR) == 0

    @pl.when(c == 0)
    def _build_gates():
        key_iota = lax.broadcasted_iota(jnp.int32, (N_KEYS, N_KEYS), 0).astype(F32)

        def one_token(t, carry):
            _store_gate_tile(i1_ref, i2_ref, gate_ref, t, gs_ref, key_iota)
            return carry

        lax.fori_loop(0, tb, one_token, 0, unroll=64)
        acc_ref[...] = jnp.zeros_like(acc_ref)

    hid = _dot_nt(xn_ref[...], u_ref[...])
    for p in range(a_per_chunk // 2):
        packed = gs_ref[pl.ds(c * (a_per_chunk // 2) + p, tb, stride=G_PITCH), :]
        al = (p // G_PAIR) * 2 * G_PAIR + p % G_PAIR
        for al_h, g_a in ((al, lax.bitcast_convert_type(packed & jnp.uint32(0xFFFF0000), F32)),
                          (al + G_PAIR, lax.bitcast_convert_type(packed << 16, F32))):
            act = _gelu_exact(hid[:, al_h * N_KEYS:(al_h + 1) * N_KEYS])
            w_ref[:, al_h * N_KEYS:(al_h + 1) * N_KEYS] = (g_a * act).astype(BF16)
    acc_ref[...] += jnp.dot(w_ref[...], v_ref[...], preferred_element_type=F32)

    @pl.when(c == pl.num_programs(1) - 1)
    def _finish():
        y = h_ref[...] + acc_ref[...]
        if norm_ref is not None:
            y = y * lax.rsqrt(jnp.mean(y * y, axis=-1, keepdims=True) + RMS_EPS) * norm_ref[...]
        o_ref[...] = y


def _peer_experts(xn, i1, i2, gate, h, u, v, layer, out_norm_g=None, tb=TOKEN_BLOCK, ec=EXPERT_CHUNK):
    t, d = h.shape
    n_exp = N_EXPERTS
    chunk0 = layer * (n_exp // ec)
    nsel = i1.shape[1]
    tok = lambda i, c: (i, 0)
    extra = [] if out_norm_g is None else [out_norm_g.reshape(1, d)]
    return pl.pallas_call(
        _peer_kernel,
        grid=(t // tb, n_exp // ec),
        in_specs=[pl.BlockSpec((tb, d), tok),
                  pl.BlockSpec((tb, nsel), tok),
                  pl.BlockSpec((tb, nsel), tok),
                  pl.BlockSpec((tb, nsel), tok),
                  pl.BlockSpec((tb, d), tok),
                  pl.BlockSpec((ec, d), lambda i, c: (chunk0 + c, 0)),
                  pl.BlockSpec((ec, d), lambda i, c: (chunk0 + c, 0))]
                 + [pl.BlockSpec((1, d), lambda i, c: (0, 0))] * len(extra),
        out_specs=pl.BlockSpec((tb, d), tok),
        out_shape=jax.ShapeDtypeStruct((t, d), F32),
        scratch_shapes=[pltpu.VMEM((tb * G_PITCH, N_KEYS), jnp.uint32),
                        pltpu.VMEM((tb, ec), BF16),
                        pltpu.VMEM((tb, d), F32)],
        compiler_params=_cparams(("parallel", "arbitrary")),
        name="peer_experts",
    )(xn, i1, i2, gate, h, u, v, *extra)


def _peer_ffn(h, g2, wq, subkeys, u, v, layer, out_norm_g=None):
    q, xn = _linear(h, wq.astype(BF16), g=g2, emit_xn=True, name="peer_query")
    i1, i2, gate = _route(q, subkeys)
    return _peer_experts(xn, i1, i2, gate, h, u, v, layer, out_norm_g)


def _band_bias(rows):
    r = np.arange(rows)[:, None]
    c = np.arange(ATT_WINDOW)[None, :]
    delta = ATT_BUF + r - c
    mult = np.zeros(delta.shape, np.float64)
    for window, dil in DILATED_GROUPS:
        mult += (delta >= 0) & (delta <= window) & (delta % dil == 0)
    return jnp.asarray(np.where(mult > 0, np.log(np.maximum(mult, 1.0)), NEG_INF), F32)


def _band_attend(q, kw, vw, bias):
    lane = lax.broadcasted_iota(jnp.int32, q.shape, 1)
    scores = [_dg(jnp.where((lane // HEAD_DIM) == half, q, jnp.zeros_like(q)), kw, _NT) for half in range(2)]
    outs = []
    for s in scores:
        s = s + bias
        m = jnp.max(s, axis=1, keepdims=True)
        p = jnp.exp(s - m)
        den = jnp.sum(p, axis=1, keepdims=True)
        outs.append(_dg(p.astype(BF16), vw, _NN) / den)
    return jnp.where((lane // HEAD_DIM) == 0, outs[0], outs[1])


def _attn_prompt_kernel(q_ref, k_ref, v_ref, bias_ref, o_ref, q_scr, k_scr, v_scr):
    s_len = q_ref.shape[0]
    zeros = jnp.zeros((ATT_BUF, LANES), BF16)
    k_scr[0:ATT_BUF, :] = zeros
    v_scr[0:ATT_BUF, :] = zeros
    k_scr[ATT_BUF:, :] = k_ref[...].astype(BF16)
    v_scr[ATT_BUF:, :] = v_ref[...].astype(BF16)
    q_scr[...] = (q_ref[...] * (HEAD_DIM ** -0.5)).astype(BF16)
    col = lax.broadcasted_iota(jnp.int32, (1, ATT_WINDOW), 1)

    def qblock(qi, carry):
        r0 = pl.multiple_of(qi * ATT_QBLOCK, ATT_QBLOCK)
        bias = jnp.where(col >= ATT_BUF - r0, bias_ref[...], NEG_INF)
        o_ref[pl.ds(r0, ATT_QBLOCK), :] = _band_attend(
            q_scr[pl.ds(r0, ATT_QBLOCK), :], k_scr[pl.ds(r0, ATT_WINDOW), :], v_scr[pl.ds(r0, ATT_WINDOW), :], bias)
        return carry

    lax.fori_loop(0, s_len // ATT_QBLOCK, qblock, 0)


_Q_BLOCK0 = POOL_WIDTH // LANES
_K_BLOCK0 = (POOL_WIDTH + ATT_WIDTH) // LANES
_V_BLOCK0 = (POOL_WIDTH + 2 * ATT_WIDTH) // LANES


def _attn_prompt(proj, n_seq, s_len):
    blk = lambda first: pl.BlockSpec((s_len, LANES), lambda b, hp: (b, first + hp))
    bias = _band_bias(ATT_QBLOCK)
    return pl.pallas_call(
        _attn_prompt_kernel,
        grid=(n_seq, ATT_WIDTH // LANES),
        in_specs=[blk(_Q_BLOCK0), blk(_K_BLOCK0), blk(_V_BLOCK0), pl.BlockSpec(bias.shape, lambda b, hp: (0, 0))],
        out_specs=pl.BlockSpec((s_len, LANES), lambda b, hp: (b, hp)),
        out_shape=jax.ShapeDtypeStruct((n_seq * s_len, ATT_WIDTH), F32),
        scratch_shapes=[pltpu.VMEM((s_len, LANES), BF16),
                        pltpu.VMEM((ATT_BUF + s_len, LANES), BF16),
                        pltpu.VMEM((ATT_BUF + s_len, LANES), BF16)],
        compiler_params=_cparams(("parallel", "parallel")),
        name="attn_prompt",
    )(proj, proj, proj, bias)


def _attn_sample_kernel(*refs):
    q_ref, k_ref, v_ref, ck_ref, cv_ref, bias_ref = refs[:6]
    o_ref, nk_ref, nv_ref, k_scr, v_scr = refs[-5:]
    t = q_ref.shape[0]
    n_past = ck_ref.shape[1]
    assert ATT_WINDOW - n_past == LANES

    def as_columns(x):
        return jnp.concatenate([x, jnp.zeros((LANES - t, LANES), F32)], axis=0).T

    new_k = as_columns(k_ref[...])
    new_v = as_columns(v_ref[...])
    k_scr[:, 0:n_past] = ck_ref[...].astype(BF16)
    v_scr[:, 0:n_past] = cv_ref[...].astype(BF16)
    k_scr[:, n_past:] = new_k.astype(BF16)
    v_scr[:, n_past:] = new_v.astype(BF16)
    q = (q_ref[...] * (HEAD_DIM ** -0.5)).astype(BF16)
    lane = lax.broadcasted_iota(jnp.int32, q.shape, 1)
    scores = [_dg(jnp.where((lane // HEAD_DIM) == half, q, jnp.zeros_like(q)), k_scr[...], _NN) for half in range(2)]
    outs = []
    for s in scores:
        s = s + bias_ref[...]
        m = jnp.max(s, axis=1, keepdims=True)
        p = jnp.exp(s - m)
        den = jnp.sum(p, axis=1, keepdims=True)
        outs.append(_dg(p.astype(BF16), v_scr[...], _NT) / den)
    o_ref[...] = jnp.where((lane // HEAD_DIM) == 0, outs[0], outs[1])

    col = lax.broadcasted_iota(jnp.int32, (LANES, LANES), 1)
    for old_ref, new_cols, out_ref in ((ck_ref, new_k, nk_ref), (cv_ref, new_v, nv_ref)):
        shifted = pltpu.roll(old_ref[...], n_past - t, 1)
        out_ref[:, 0:n_past - LANES] = shifted[:, 0:n_past - LANES]
        out_ref[:, n_past - LANES:] = jnp.where(col < LANES - t, shifted[:, n_past - LANES:],
                                                 pltpu.roll(new_cols, LANES - t, 1))


def _attn_sample(proj, row0, n_seq, t_len, cache_k, cache_v, layer, depth, stacks):
    assert row0 % t_len == 0 and t_len % 8 == 0 and t_len <= ATT_QBLOCK
    blk0 = row0 // t_len
    pairs = ATT_WIDTH // LANES
    blk = lambda first: pl.BlockSpec((t_len, LANES), lambda b, hp: (blk0 + b, first + hp))
    cblk = pl.BlockSpec((LANES, ATT_BUF), lambda b, hp: ((layer * n_seq + b) * pairs + hp, 0))
    bias = _band_bias(t_len)
    stack_shape = jax.ShapeDtypeStruct(cache_k.shape, F32)
    carried = [] if stacks is None else list(stacks)
    return pl.pallas_call(
        _attn_sample_kernel,
        grid=(n_seq, pairs),
        in_specs=[blk(_Q_BLOCK0), blk(_K_BLOCK0), blk(_V_BLOCK0), cblk, cblk,
                  pl.BlockSpec(bias.shape, lambda b, hp: (0, 0))]
                 + [pl.BlockSpec(memory_space=pl.ANY)] * len(carried),
        out_specs=[pl.BlockSpec((t_len, LANES), lambda b, hp: (b, hp)), cblk, cblk],
        out_shape=[jax.ShapeDtypeStruct((n_seq * t_len, ATT_WIDTH), F32), stack_shape, stack_shape],
        input_output_aliases={6 + j: 1 + j for j in range(len(carried))},
        scratch_shapes=[pltpu.VMEM((LANES, ATT_WINDOW), BF16), pltpu.VMEM((LANES, ATT_WINDOW), BF16)],
        compiler_params=_cparams(("parallel", "parallel")),
        name="attn_sample",
    )(proj, proj, proj, cache_k, cache_v, bias, *carried)


def _pool_kernel(x_ref, buf_ref, w_ref, scale_ref, o_ref, *, pos0):
    x = x_ref[...]
    n = x.shape[0]
    xf = jnp.concatenate([buf_ref[...], x], axis=0)
    s2 = xf + pltpu.roll(xf, 1, 0)
    s4 = s2 + pltpu.roll(s2, 2, 0)
    s8 = s4 + pltpu.roll(s4, 4, 0)
    s16 = s8 + pltpu.roll(s8, 8, 0)
    group = lax.broadcasted_iota(jnp.int32, (1, POOL_WIDTH), 1) // POOL_GROUP
    sums = jnp.where(group == 0, s2, jnp.where(group == 1, s4, jnp.where(group == 2, s8, s16)))[POOL_BUF + 1:]
    win = jnp.where(group == 0, 2.0, jnp.where(group == 1, 4.0, jnp.where(group == 2, 8.0, 16.0)))
    seen = (lax.broadcasted_iota(jnp.int32, (n, 1), 0) + (pos0 + 1)).astype(F32)
    diff = sums / jnp.minimum(win, seen) - x
    o_ref[...] = _mm3(diff, w_ref[...]) * scale_ref[...]


def _pool(proj, row0, n_seq, s_len, buf, w, scale, pos0):
    assert POOL_WINDOWS == (2, 4, 8, 16) and row0 % s_len == 0
    blk0 = row0 // s_len
    wbd = jnp.zeros((POOL_WIDTH, POOL_WIDTH), F32)
    for i in range(len(POOL_WINDOWS)):
        wbd = wbd.at[i * POOL_GROUP:(i + 1) * POOL_GROUP, i * POOL_GROUP:(i + 1) * POOL_GROUP].set(w[i])
    buf16 = jnp.pad(buf, ((0, 0), (1, 0), (0, 0))).reshape(n_seq * (POOL_BUF + 1), POOL_WIDTH)
    return pl.pallas_call(
        functools.partial(_pool_kernel, pos0=pos0),
        grid=(n_seq,),
        in_specs=[pl.BlockSpec((s_len, POOL_WIDTH), lambda b: (blk0 + b, 0)),
                  pl.BlockSpec((POOL_BUF + 1, POOL_WIDTH), lambda b: (b, 0)),
                  pl.BlockSpec((POOL_WIDTH, POOL_WIDTH), lambda b: (0, 0)),
                  pl.BlockSpec((1, POOL_WIDTH), lambda b: (0, 0))],
        out_specs=pl.BlockSpec((s_len, POOL_WIDTH), lambda b: (b, 0)),
        out_shape=jax.ShapeDtypeStruct((n_seq * s_len, POOL_WIDTH), F32),
        compiler_params=_cparams(("parallel",)),
        name="pool_mixer",
    )(proj, buf16, wbd, scale.reshape(1, POOL_WIDTH))


def _split(x):
    hi = x.astype(BF16)
    return hi, (x - hi.astype(F32)).astype(BF16)


def _dg(a, b, dims):
    return lax.dot_general(a, b, (dims, ((), ())), preferred_element_type=F32)


_NN = ((1,), (0,))
_NT = ((1,), (1,))
_TN = ((0,), (0,))


def _mm3(a, b, dims=_NN):
    a_hi, a_lo = _split(a)
    b_hi, b_lo = _split(b)
    return _dg(a_hi, b_hi, dims) + (_dg(a_lo, b_hi, dims) + _dg(a_hi, b_lo, dims))


def _split3(x):
    x0 = x.astype(BF16)
    r1 = x - x0.astype(F32)
    x1 = r1.astype(BF16)
    return x0, x1, (r1 - x1.astype(F32)).astype(BF16)


def _mm_exact_rhs(a, b_bf16):
    a0, a1, a2 = _split3(a)
    return _dg(a0, b_bf16, _NN) + (_dg(a1, b_bf16, _NN) + _dg(a2, b_bf16, _NN))


def _mm_exact_lhs(a_bf16, b):
    b0, b1, b2 = _split3(b)
    return _dg(a_bf16, b0, _NN) + (_dg(a_bf16, b1, _NN) + _dg(a_bf16, b2, _NN))


def _rwkv_chunk(pblks, prev_rows, s0, prm):
    (mu, w0, w_up, a0, a_up, g_up, k_k, k_a, r_k, ln_w, ln_b) = prm
    n_seq = len(pblks)
    c = pblks[0].shape[0]
    n = n_seq * c
    pblk = jnp.concatenate(pblks, axis=0) if n_seq > 1 else pblks[0]
    shifted = [jnp.concatenate([prev_rows[i], pblks[i][:-1]], axis=0) if c > 1 else prev_rows[i]
               for i in range(n_seq)]
    shifted = jnp.concatenate(shifted, axis=0) if n_seq > 1 else shifted[0]
    ps = pblk + (shifted - pblk) * mu
    c1, c2, c3 = RWKV_WIDTH, 2 * RWKV_WIDTH, 3 * RWKV_WIDTH
    c4 = c3 + DECAY_LORA
    c5 = c4 + ICLR_LORA
    r, k, v = ps[:, :c1], ps[:, c1:c2], ps[:, c2:c3]
    wd, ad, gd = ps[:, c3:c4], ps[:, c4:c5], ps[:, c5:]

    z = -(w0 + _mm3(jnp.tanh(wd), w_up))
    softplus = jnp.maximum(z, 0.0) + jnp.log(1.0 + jnp.exp(-jnp.abs(z)))
    lw = -jnp.exp(-softplus - 0.5)
    a = jax.nn.sigmoid(a0 + _mm3(ad, a_up))
    g = _mm3(jax.nn.sigmoid(gd), g_up)

    ch_r = lax.broadcasted_iota(jnp.int32, (RWKV_WIDTH, RWKV_WIDTH), 0) // HEAD_DIM
    ch_c = lax.broadcasted_iota(jnp.int32, (RWKV_WIDTH, RWKV_WIDTH), 1) // HEAD_DIM
    same_head = jnp.where(ch_r == ch_c, 1.0, 0.0).astype(BF16)

    kk = k * k_k
    kk = kk / jnp.maximum(jnp.sqrt(_mm_exact_rhs(kk * kk, same_head)), 1e-12)
    k2 = k * (1.0 + (a - 1.0) * k_a)
    bonus = _mm_exact_rhs(r * k2 * r_k, same_head) * v

    n_r = lax.broadcasted_iota(jnp.int32, (n, n), 0)
    n_c = lax.broadcasted_iota(jnp.int32, (n, n), 1)
    run = (n_r // c == n_c // c) & (n_c <= n_r)
    cs = _mm_exact_lhs(jnp.where(run, 1.0, 0.0).astype(BF16), lw)
    e_up = jnp.exp(cs)
    e_dn = jnp.exp(-cs)
    kap = kk * jnp.exp(cs - lw)
    bet = kk * a * e_dn
    kt = k2 * e_dn
    rt = r * e_up

    t_r = lax.broadcasted_iota(jnp.int32, (c, c), 0)
    t_c = lax.broadcasted_iota(jnp.int32, (c, c), 1)
    incl = t_c <= t_r
    strict = t_c < t_r
    eye = jnp.where(t_r == t_c, 1.0, 0.0)
    chains = [(i, h) for i in range(n_seq) for h in range(RWKV_HEADS)]
    cut = lambda x, i, h: x[i * c:(i + 1) * c, h * HEAD_DIM:(h + 1) * HEAD_DIM]
    kp_x = [cut(kap, i, h) for i, h in chains]
    bt_x = [cut(bet, i, h) for i, h in chains]
    kt_x = [cut(kt, i, h) for i, h in chains]
    rt_x = [cut(rt, i, h) for i, h in chains]
    v_x = [cut(v, i, h) for i, h in chains]
    s_x = [s0[i][h] for i, h in chains]
    every = range(len(chains))
    quad = [_mm3(jnp.concatenate([kp_x[x], rt_x[x]], axis=0), jnp.concatenate([bt_x[x], kt_x[x]], axis=0), _NT)
            for x in every]
    a_b = [jnp.where(strict, quad[x][:c, :c], 0.0) for x in every]
    a_k = [jnp.where(strict, quad[x][:c, c:], 0.0) for x in every]
    m_b = [jnp.where(incl, quad[x][c:, :c], 0.0) for x in every]
    m_k = [jnp.where(incl, quad[x][c:, c:], 0.0) for x in every]
    akv = [_mm3(a_k[x], v_x[x]) for x in every]
    mkv = [_mm3(m_k[x], v_x[x]) for x in every]
    rs = [_mm3(rt_x[x], s_x[x], _NT) for x in every]
    vk = [_mm3(v_x[x], kt_x[x], _TN) for x in every]
    inv = [eye - jnp.where(t_r // 2 == t_c // 2, a_b[x], 0.0) for x in every]
    b = 2
    while b < c:
        pair = (t_r // (2 * b) == t_c // (2 * b)) & ((t_r // b) % 2 == 1) & ((t_c // b) % 2 == 0)
        low = [_mm3(jnp.where(pair, a_b[x], 0.0), inv[x]) for x in every]
        inv = [inv[x] - _mm3(inv[x], low[x]) for x in every]
        b *= 2
    pq = [_mm3(inv[x], jnp.concatenate([kp_x[x], akv[x]], axis=1)) for x in every]
    p_x = [pq[x][:, :HEAD_DIM] for x in every]
    q_x = [pq[x][:, HEAD_DIM:] for x in every]
    u_x = [_mm3(p_x[x], s_x[x], _NT) + q_x[x] for x in every]
    o_x = [rs[x] + mkv[x] - _mm3(m_b[x], u_x[x]) for x in every]
    ub = [_mm3(u_x[x], bt_x[x], _TN) for x in every]
    s_new = [[None] * RWKV_HEADS for _ in range(n_seq)]
    for x, (i, h) in enumerate(chains):
        gam = e_up[(i + 1) * c - 1:(i + 1) * c, h * HEAD_DIM:(h + 1) * HEAD_DIM]
        s_new[i][h] = (s_x[x] + vk[x] - ub[x]) * gam
    o = [jnp.concatenate(o_x[i * RWKV_HEADS:(i + 1) * RWKV_HEADS], axis=1) for i in range(n_seq)]
    o = jnp.concatenate(o, axis=0) if n_seq > 1 else o[0]
    mean = _mm_exact_rhs(o, same_head) * (1.0 / HEAD_DIM)
    d = o - mean
    var = _mm_exact_rhs(d * d, same_head) * (1.0 / HEAD_DIM)
    o = (d * lax.rsqrt(var + GN_EPS) * ln_w + ln_b + bonus) * g
    return [o[i * c:(i + 1) * c] for i in range(n_seq)], s_new


def _rwkv_kernel(*refs, rows):
    p_refs = refs[:rows]
    (prev_ref, s0_ref, mu_ref, w0_ref, wup_ref, a0_ref, aup_ref, gup_ref, kk_ref, ka_ref, rk_ref, lnw_ref, lnb_ref,
     o_ref, sT_ref, s_scr, last_scr) = refs[rows:]
    ci = pl.program_id(1)

    @pl.when(ci == 0)
    def _init():
        s_scr[...] = s0_ref[...]
        last_scr[...] = prev_ref[...]

    prm = (mu_ref[...], w0_ref[...], wup_ref[...], a0_ref[...], aup_ref[...], gup_ref[...], kk_ref[...],
           ka_ref[...], rk_ref[...], lnw_ref[...], lnb_ref[...])
    pblks = [p_refs[i][...] for i in range(rows)]
    outs, s_new = _rwkv_chunk(pblks, [last_scr[i] for i in range(rows)],
                              [[s_scr[i, h] for h in range(RWKV_HEADS)] for i in range(rows)], prm)
    c = pblks[0].shape[0]
    for i in range(rows):
        o_ref[i] = outs[i]
        for h in range(RWKV_HEADS):
            s_scr[i, h] = s_new[i][h]
        last_scr[i] = pblks[i][c - 1:c, :]

    @pl.when(ci == pl.num_programs(1) - 1)
    def _fin():
        sT_ref[...] = s_scr[...]


def _rwkv_mixer(proj, row0, n_seq, s_len, prev, wkv0, lp, chunk, rows=RWKV_ROWS):
    assert s_len % chunk == 0 and n_seq % rows == 0 and row0 % chunk == 0
    assert proj.shape[1] == 2 * RWKV_PROJ
    n_chunks = s_len // chunk
    blk0 = row0 // chunk
    row = lambda x: x.reshape(1, -1).astype(F32)
    full = lambda a: pl.BlockSpec(a.shape, lambda b, c: (0,) * a.ndim)
    weights = [row(lp['rwkv_mu']), row(lp['rwkv_w0']), lp['rwkv_w_up'], row(lp['rwkv_a0']), lp['rwkv_a_up'],
               lp['rwkv_g_up'], row(lp['rwkv_k_k']), row(lp['rwkv_k_a']), row(lp['rwkv_r_k']),
               row(lp['rwkv_ln_w']), row(lp['rwkv_ln_b'])]
    p_spec = lambda i: pl.BlockSpec((chunk, RWKV_PROJ), lambda b, c: (blk0 + (b * rows + i) * n_chunks + c, 1))
    out, s_t = pl.pallas_call(
        functools.partial(_rwkv_kernel, rows=rows),
        grid=(n_seq // rows, n_chunks),
        in_specs=[p_spec(i) for i in range(rows)]
                 + [pl.BlockSpec((rows, 1, RWKV_PROJ), lambda b, c: (b, 0, 0)),
                    pl.BlockSpec((rows, RWKV_HEADS, HEAD_DIM, HEAD_DIM), lambda b, c: (b, 0, 0, 0))]
                 + [full(w) for w in weights],
        out_specs=[pl.BlockSpec((rows, chunk, RWKV_WIDTH), lambda b, c: (b, c, 0)),
                   pl.BlockSpec((rows, RWKV_HEADS, HEAD_DIM, HEAD_DIM), lambda b, c: (b, 0, 0, 0))],
        out_shape=[jax.ShapeDtypeStruct((n_seq, s_len, RWKV_WIDTH), F32),
                   jax.ShapeDtypeStruct((n_seq, RWKV_HEADS, HEAD_DIM, HEAD_DIM), F32)],
        scratch_shapes=[pltpu.VMEM((rows, RWKV_HEADS, HEAD_DIM, HEAD_DIM), F32),
                        pltpu.VMEM((rows, 1, RWKV_PROJ), F32)],
        compiler_params=_cparams(("parallel", "arbitrary")),
        name="rwkv_mixer",
    )(*([proj] * rows), prev.reshape(n_seq, 1, RWKV_PROJ), wkv0, *weights)
    return out.reshape(n_seq * s_len, RWKV_WIDTH), s_t


def kernel(x_prompt, x_sample, state_pool, cache_k, cache_v, state_shift, state_wkv, norm1_g, norm2_g, final_g, w_in, w_out, pool_w, pool_scale, rwkv_mu, rwkv_w0, rwkv_w_up, rwkv_a0, rwkv_a_up, rwkv_g_up, rwkv_k_k, rwkv_k_a, rwkv_r_k, rwkv_ln_w, rwkv_ln_b, peer_wq, peer_subkeys, peer_u, peer_v):
    bp, sp_len, d = x_prompt.shape
    bs, ss_len, _ = x_sample.shape
    tp = bp * sp_len
    ts = bs * ss_len
    depth = w_in.shape[0]
    keep = min(ATT_BUF, sp_len)
    h = jnp.concatenate([x_prompt.reshape(tp, d), x_sample.reshape(ts, d)], axis=0)
    pos = jnp.concatenate([jnp.tile(jnp.arange(sp_len, dtype=jnp.int32), bp),
                           jnp.tile(PAST_LEN + jnp.arange(ss_len, dtype=jnp.int32), bs)])
    rope_tabs = _rope_tables(pos)
    zero_buf = jnp.zeros((bp, POOL_BUF, POOL_WIDTH), F32)
    zero_shift = jnp.zeros((bp, RWKV_PROJ), F32)
    zero_wkv = jnp.zeros((bp, RWKV_HEADS, HEAD_DIM, HEAD_DIM), F32)
    k_lo, v_lo, r_lo = POOL_WIDTH + ATT_WIDTH, POOL_WIDTH + 2 * ATT_WIDTH, POOL_WIDTH + 3 * ATT_WIDTH
    heads = lambda x, n: x.reshape(n, -1, ATT_HEADS, HEAD_DIM)
    outs_p = [[] for _ in range(5)]
    outs_s = [[] for _ in range(5)]
    kv_stacks = None
    feature_major = lambda x: jnp.transpose(x, (0, 1, 3, 4, 2)).reshape(depth * bs * ATT_WIDTH, ATT_BUF)
    ck_all = feature_major(cache_k)
    cv_all = feature_major(cache_v)
    u_all = peer_u.astype(BF16).reshape(depth * N_EXPERTS, d)
    v_all = peer_v.astype(BF16).reshape(depth * N_EXPERTS, d)
    for l in range(depth):
        lp = {'rwkv_mu': rwkv_mu[l], 'rwkv_w0': rwkv_w0[l],
              'rwkv_w_up': rwkv_w_up[l], 'rwkv_a0': rwkv_a0[l], 'rwkv_a_up': rwkv_a_up[l],
              'rwkv_g_up': rwkv_g_up[l], 'rwkv_k_k': rwkv_k_k[l], 'rwkv_k_a': rwkv_k_a[l],
              'rwkv_r_k': rwkv_r_k[l], 'rwkv_ln_w': rwkv_ln_w[l], 'rwkv_ln_b': rwkv_ln_b[l]}
        proj = _linear(h, w_in[l].astype(BF16), g=norm1_g[l], rope=rope_tabs, name="in_proj")
        proj_s = proj[tp:].reshape(bs, ss_len, IN_WIDTH)

        pool_p = _pool(proj, 0, bp, sp_len, zero_buf, pool_w[l], pool_scale[l], 0)
        pool_s = _pool(proj, tp, bs, ss_len, state_pool[l], pool_w[l], pool_scale[l], PAST_LEN)
        att_p = _attn_prompt(proj, bp, sp_len)
        att_s, *kv_stacks = _attn_sample(proj, tp, bs, ss_len, ck_all, cv_all, l, depth, kv_stacks)
        rw_p, wkv_p = _rwkv_mixer(proj, 0, bp, sp_len, zero_shift, zero_wkv, lp, min(sp_len, RWKV_CHUNK))
        rw_s, wkv_s = _rwkv_mixer(proj, tp, bs, ss_len, state_shift[l], state_wkv[l], lp, min(ss_len, RWKV_CHUNK))

        tail = lambda n, lo, hi: jnp.stack([proj[(b + 1) * sp_len - n:(b + 1) * sp_len, lo:hi] for b in range(bp)])
        outs_p[0].append(tail(POOL_BUF, 0, POOL_WIDTH))
        outs_p[1].append(heads(tail(keep, k_lo, v_lo), bp))
        outs_p[2].append(heads(tail(keep, v_lo, r_lo), bp))
        outs_p[3].append(tail(1, r_lo, IN_WIDTH)[:, 0])
        outs_p[4].append(wkv_p)
        outs_s[0].append(jnp.concatenate([state_pool[l], proj_s[:, :, :POOL_WIDTH]], axis=1)[:, -POOL_BUF:])
        outs_s[3].append(proj_s[:, -1, r_lo:])
        outs_s[4].append(wkv_s)

        mixed = jnp.concatenate([jnp.concatenate([pool_p, att_p, rw_p], axis=1),
                                 jnp.concatenate([pool_s, att_s, rw_s], axis=1)], axis=0)
        h = _linear(mixed, w_out[l].astype(BF16), resid=h, name="out_proj")
        h = _peer_ffn(h, norm2_g[l], peer_wq[l], peer_subkeys[l], u_all, v_all, l,
                      final_g if l == depth - 1 else None)
    y = h
    y_prompt = y[:tp].reshape(bp, sp_len, d)
    y_sample = y[tp:].reshape(bs, ss_len, d)
    s_k, s_v = (jnp.transpose(x.reshape(depth, bs, ATT_HEADS, HEAD_DIM, ATT_BUF), (0, 1, 4, 2, 3)) for x in kv_stacks)
    return (y_prompt, y_sample, *[jnp.stack(a) for a in outs_p],
            jnp.stack(outs_s[0]), s_k, s_v, jnp.stack(outs_s[3]), jnp.stack(outs_s[4]))
```

```python
import functools
import math

import jax
import jax.numpy as jnp
import numpy as np
from jax import lax
from jax.experimental import pallas as pl
from jax.experimental.pallas import tpu as pltpu

F32 = jnp.float32
BF16 = jnp.bfloat16

D_MODEL = 1024
HEAD_DIM = 64
POOL_WINDOWS = (2, 4, 8, 16)
POOL_GROUP = 64
POOL_WIDTH = 256
POOL_BUF = 15
ATT_WIDTH = 384
ATT_HEADS = 6
DILATED_GROUPS = ((128, 1), (512, 4), (2048, 16))
ATT_BUF = 2048
ROPE_DIM = 16
ROPE_THETA = 500000.0
NEG_INF = -1e30
RWKV_WIDTH = 384
RWKV_HEADS = 6
DECAY_LORA = 64
ICLR_LORA = 64
GATE_LORA = 128
RWKV_PROJ = 1408
GN_EPS = HEAD_DIM * 1e-5
N_KEYS = 128
N_EXPERTS = N_KEYS * N_KEYS
PEER_HEADS = 8
PEER_TOPK = 16
RMS_EPS = 1e-6

TOKEN_BLOCK = 256
EXPERT_CHUNK = 4096
G_PAIR = 8
G_PITCH = 72
RWKV_CHUNK = 64
RWKV_ROWS = 2
LANES = 128
IN_WIDTH = POOL_WIDTH + 3 * ATT_WIDTH + RWKV_PROJ
ATT_QBLOCK = 128
ATT_WINDOW = ATT_BUF + ATT_QBLOCK
PAST_LEN = 8192
VMEM_LIMIT = 56 * 1024 * 1024


def _cparams(sem):
    return pltpu.CompilerParams(dimension_semantics=sem, vmem_limit_bytes=VMEM_LIMIT)


def _rope_tables(pos):
    half = ROPE_DIM // 2
    inv = ROPE_THETA ** (-jnp.arange(half, dtype=F32) * 2.0 / ROPE_DIM)
    ang = pos.astype(F32)[:, None] * inv[None, :]
    cos, sin = jnp.cos(ang), jnp.sin(ang)
    n = pos.shape[0]
    one = jnp.ones((n, HEAD_DIM - ROPE_DIM), F32)
    zero = jnp.zeros((n, HEAD_DIM - ROPE_DIM), F32)
    zh = jnp.zeros((n, half), F32)
    c = jnp.concatenate([cos, cos, one], axis=1)
    a = jnp.concatenate([-sin, zh, zero], axis=1)
    b = jnp.concatenate([zh, sin, zero], axis=1)
    tile = lambda x: jnp.concatenate([x, x], axis=1)
    return tile(c), tile(a), tile(b)


def _rope_lanes(x, c, a, b):
    half = ROPE_DIM // 2
    return x * c + pltpu.roll(x, LANES - half, 1) * a + pltpu.roll(x, half, 1) * b


def _linear_kernel(*refs, norm, resid, emit_xn, rope):
    it = iter(refs)
    x_ref = next(it)
    g_ref = next(it) if norm else None
    w_ref = next(it)
    r_ref = next(it) if resid else None
    rope_refs = [next(it) for _ in range(3)] if rope else None
    o_ref = next(it)
    xn_ref = next(it) if emit_xn else None

    x = x_ref[...].astype(F32)
    if norm:
        x = x * lax.rsqrt(jnp.mean(x * x, axis=-1, keepdims=True) + RMS_EPS) * g_ref[...]
    if emit_xn:
        xn_ref[...] = x.astype(xn_ref.dtype)
    acc = jnp.dot(x.astype(BF16), w_ref[...], preferred_element_type=F32)
    if resid:
        acc = acc + r_ref[...]
    if rope:
        c, a, b = (r[...] for r in rope_refs)
        lo, hi = POOL_WIDTH, POOL_WIDTH + 2 * ATT_WIDTH
        o_ref[:, :lo] = acc[:, :lo]
        for j in range(lo, hi, LANES):
            o_ref[:, j:j + LANES] = _rope_lanes(acc[:, j:j + LANES], c, a, b)
        o_ref[:, hi:] = acc[:, hi:]
    else:
        o_ref[...] = acc


def _linear(x, w, *, g=None, resid=None, emit_xn=False, rope=None, tb=TOKEN_BLOCK, name="linear"):
    t, k = x.shape
    n = w.shape[1]
    assert t % tb == 0
    norm = g is not None
    ins = [x]
    specs = [pl.BlockSpec((tb, k), lambda i: (i, 0))]
    if norm:
        ins.append(g.reshape(1, k).astype(F32))
        specs.append(pl.BlockSpec((1, k), lambda i: (0, 0)))
    ins.append(w)
    specs.append(pl.BlockSpec((k, n), lambda i: (0, 0)))
    if resid is not None:
        ins.append(resid)
        specs.append(pl.BlockSpec((tb, n), lambda i: (i, 0)))
    if rope is not None:
        ins.extend(rope)
        specs.extend([pl.BlockSpec((tb, LANES), lambda i: (i, 0))] * 3)
    out_shape = [jax.ShapeDtypeStruct((t, n), F32)]
    out_specs = [pl.BlockSpec((tb, n), lambda i: (i, 0))]
    if emit_xn:
        out_shape.append(jax.ShapeDtypeStruct((t, k), BF16))
        out_specs.append(pl.BlockSpec((tb, k), lambda i: (i, 0)))
    outs = pl.pallas_call(
        functools.partial(_linear_kernel, norm=norm, resid=resid is not None, emit_xn=emit_xn,
                          rope=rope is not None),
        grid=(t // tb,),
        in_specs=specs,
        out_specs=out_specs,
        out_shape=out_shape,
        compiler_params=_cparams(("parallel",)),
        name=name,
    )(*ins)
    return outs if emit_xn else outs[0]


def _dot_nt(a, b):
    return lax.dot_general(a, b, (((1,), (1,)), ((), ())), preferred_element_type=F32)


def _oddeven_merge(lo, hi, r):
    step = r * 2
    if step < hi - lo:
        yield from _oddeven_merge(lo, hi, step)
        yield from _oddeven_merge(lo + r, hi, step)
        yield from [(i, i + r) for i in range(lo + r, hi - r, step)]
    else:
        yield (lo, lo + r)


def _oddeven_sort(lo, hi):
    if hi - lo >= 1:
        mid = lo + (hi - lo) // 2
        yield from _oddeven_sort(lo, mid)
        yield from _oddeven_sort(mid + 1, hi)
        yield from _oddeven_merge(lo, hi, 1)


_SORT16 = tuple(_oddeven_sort(0, PEER_TOPK - 1))
_BITONIC16 = tuple((i, i + d) for d in (8, 4, 2, 1) for i in range(PEER_TOPK) if not i & d)
SUBLANES = 8


def _exchange(v, p, i, j):
    keep = v[i] >= v[j]
    v[i], v[j] = jnp.where(keep, v[i], v[j]), jnp.where(keep, v[j], v[i])
    p[i], p[j] = jnp.where(keep, p[i], p[j]), jnp.where(keep, p[j], p[i])


def _top16(v, p, n_real):
    v, p = list(v), list(p)
    for i, j in _SORT16:
        if j < n_real:
            _exchange(v, p, i, j)
    for shift in (4, 2, 1):
        vb = [pltpu.roll(x, shift, 0) for x in v]
        pb = [pltpu.roll(x, shift, 0) for x in p]
        for i in range(PEER_TOPK):
            keep = v[i] >= vb[PEER_TOPK - 1 - i]
            v[i] = jnp.where(keep, v[i], vb[PEER_TOPK - 1 - i])
            p[i] = jnp.where(keep, p[i], pb[PEER_TOPK - 1 - i])
        for i, j in _BITONIC16:
            _exchange(v, p, i, j)
    return v, p


def _rows16(xs):
    sub = lax.broadcasted_iota(jnp.int32, xs[0].shape, 0)
    halves = []
    for base in (0, SUBLANES):
        acc = xs[base]
        for k in range(1, SUBLANES):
            acc = jnp.where(sub == k, xs[base + k], acc)
        halves.append(acc)
    return jnp.concatenate(halves, axis=0)


def _route_kernel(q_ref, sk_ref, i1_ref, i2_ref, gate_ref, tv_ref, ti_ref, e1_ref, e2_ref, gt_ref):
    tb = q_ref.shape[0]
    neg = jnp.float32(-jnp.inf)
    sub_iota = lax.broadcasted_iota(jnp.int32, (SUBLANES, tb), 0).astype(F32)

    def half_topk(hc, carry):
        off = pl.multiple_of(hc * N_KEYS, N_KEYS)
        qh = q_ref[:, pl.ds(off, N_KEYS)]
        sk = sk_ref[hc]
        q_hi = qh.astype(BF16)
        q_lo = (qh - q_hi.astype(F32)).astype(BF16)
        s_hi = sk.astype(BF16)
        s_lo = (sk - s_hi.astype(F32)).astype(BF16)
        sc = _dot_nt(s_hi, q_hi) + _dot_nt(s_lo, q_hi) + _dot_nt(s_hi, q_lo)
        n_slab = N_KEYS // SUBLANES
        vals = [sc[j * SUBLANES:(j + 1) * SUBLANES] for j in range(n_slab)]
        idxs = [sub_iota + float(j * SUBLANES) for j in range(n_slab)]
        vals, idxs = _top16(vals, idxs, n_slab)
        tv_ref[hc] = _rows16(vals)
        ti_ref[hc] = _rows16(idxs)
        return carry

    lax.fori_loop(0, 2 * PEER_HEADS, half_topk, 0, unroll=2)

    row8 = lax.broadcasted_iota(jnp.int32, (SUBLANES, tb), 0)

    def pair_topk(h, carry):
        a = tv_ref[2 * h]
        b = tv_ref[2 * h + 1]
        ia = ti_ref[2 * h] * float(N_KEYS)
        ib = ti_ref[2 * h + 1]
        vs = [a[0:1] + b[0:8], a[0:1] + b[8:16]]
        code = [ia[0:1] + ib[0:8], ia[0:1] + ib[8:16]]
        for i in range(1, 8):
            lim = PEER_TOPK // (i + 1)
            v = a[i:i + 1] + b[0:8]
            vs.append(jnp.where(row8 < lim, v, neg) if lim < 8 else v)
            code.append(ia[i:i + 1] + ib[0:8])
        vs.append(a[8:16] + b[0:1])
        code.append(ia[8:16] + ib[0:1])
        n_real = len(vs)
        pad_v = jnp.full((SUBLANES, tb), neg, F32)
        pad_c = jnp.zeros((SUBLANES, tb), F32)
        vs += [pad_v] * (PEER_TOPK - n_real)
        code += [pad_c] * (PEER_TOPK - n_real)
        tops, code = _top16(vs, code, n_real)
        e = [jnp.exp(t - tops[0]) for t in tops]
        total = e[0]
        for x in e[1:]:
            total = total + x
        first = [jnp.floor(cd * (1.0 / N_KEYS)) for cd in code]
        second = [cd - f * float(N_KEYS) for cd, f in zip(code, first)]
        row = pl.multiple_of(h * PEER_TOPK, PEER_TOPK)
        gt_ref[pl.ds(row, PEER_TOPK), :] = _rows16([x / total for x in e])
        e1_ref[pl.ds(row, PEER_TOPK), :] = _rows16(first)
        e2_ref[pl.ds(row, PEER_TOPK), :] = _rows16(second)
        return carry

    lax.fori_loop(0, PEER_HEADS, pair_topk, 0)
    i1_ref[...] = e1_ref[...].T
    i2_ref[...] = e2_ref[...].T
    gate_ref[...] = gt_ref[...].T


def _route(q, subkeys, tb=TOKEN_BLOCK):
    t = q.shape[0]
    nsel = PEER_HEADS * PEER_TOPK
    sk = subkeys.reshape(2 * PEER_HEADS, N_KEYS, N_KEYS)
    out = jax.ShapeDtypeStruct((t, nsel), F32)
    spec = pl.BlockSpec((tb, nsel), lambda i: (i, 0))
    return pl.pallas_call(
        _route_kernel,
        grid=(t // tb,),
        in_specs=[pl.BlockSpec((tb, q.shape[1]), lambda i: (i, 0)),
                  pl.BlockSpec(sk.shape, lambda i: (0, 0, 0))],
        out_specs=[spec, spec, spec],
        out_shape=[out, out, out],
        scratch_shapes=[pltpu.VMEM((2 * PEER_HEADS, PEER_TOPK, tb), F32),
                        pltpu.VMEM((2 * PEER_HEADS, PEER_TOPK, tb), F32),
                        pltpu.VMEM((nsel, tb), F32),
                        pltpu.VMEM((nsel, tb), F32),
                        pltpu.VMEM((nsel, tb), F32)],
        compiler_params=_cparams(("parallel",)),
        name="peer_route",
    )(q, sk)


def _gelu_exact(x):
    return 0.5 * x * (1.0 + lax.erf(x * (1.0 / math.sqrt(2.0))))


def _store_gate_tile(i1_ref, i2_ref, gate_ref, t, gs_ref, key_iota):
    r1 = i1_ref[pl.ds(t, 1), :]
    r2 = i2_ref[pl.ds(t, 1), :]
    gr = gate_ref[pl.ds(t, 1), :]
    a_t = jnp.where(key_iota == r1, gr, 0.0).astype(BF16)
    b_t = jnp.where(key_iota == r2, 1.0, 0.0).astype(BF16)
    g = _dot_nt(a_t, b_t)
    groups = range(0, N_KEYS, 2 * G_PAIR)
    hi = jnp.concatenate([g[r:r + G_PAIR] for r in groups], axis=0)
    lo = jnp.concatenate([g[r + G_PAIR:r + 2 * G_PAIR] for r in groups], axis=0)
    bits = lambda x: lax.bitcast_convert_type(x.astype(BF16).astype(F32), jnp.uint32)
    row = pl.multiple_of(t * G_PITCH, SUBLANES)
    gs_ref[pl.ds(row, N_KEYS // 2), :] = bits(hi) | (bits(lo) >> 16)


def _peer_kernel(xn_ref, i1_ref, i2_ref, gate_ref, h_ref, u_ref, v_ref, *rest):
    norm_ref = rest[0] if len(rest) == 5 else None
    o_ref, gs_ref, w_ref, acc_ref = rest[-4:]
    c = pl.program_id(1)
    tb = xn_ref.shape[0]
    ec = u_ref.shape[0]
    a_per_chunk = ec // N_KEYS
    assert a_per_chunk % (2 * G_PAIR) == 0

    @pl.when(c == 0)
    def _build_gates():
        key_iota = lax.broadcasted_iota(jnp.int32, (N_KEYS, N_KEYS), 0).astype(F32)

        def one_token(t, carry):
            _store_gate_tile(i1_ref, i2_ref, gate_ref, t, gs_ref, key_iota)
            return carry

        lax.fori_loop(0, tb, one_token, 0, unroll=64)
        acc_ref[...] = jnp.zeros_like(acc_ref)

    hid = _dot_nt(xn_ref[...], u_ref[...])
    for p in range(a_per_chunk // 2):
        packed = gs_ref[pl.ds(c * (a_per_chunk // 2) + p, tb, stride=G_PITCH), :]
        al = (p // G_PAIR) * 2 * G_PAIR + p % G_PAIR
        for al_h, g_a in ((al, lax.bitcast_convert_type(packed & jnp.uint32(0xFFFF0000), F32)),
                          (al + G_PAIR, lax.bitcast_convert_type(packed << 16, F32))):
            act = _gelu_exact(hid[:, al_h * N_KEYS:(al_h + 1) * N_KEYS])
            w_ref[:, al_h * N_KEYS:(al_h + 1) * N_KEYS] = (g_a * act).astype(BF16)
    acc_ref[...] += jnp.dot(w_ref[...], v_ref[...], preferred_element_type=F32)

    @pl.when(c == pl.num_programs(1) - 1)
    def _finish():
        y = h_ref[...] + acc_ref[...]
        if norm_ref is not None:
            y = y * lax.rsqrt(jnp.mean(y * y, axis=-1, keepdims=True) + RMS_EPS) * norm_ref[...]
        o_ref[...] = y


def _peer_experts(xn, i1, i2, gate, h, u, v, layer, out_norm_g=None, tb=TOKEN_BLOCK, ec=EXPERT_CHUNK):
    t, d = h.shape
    n_exp = N_EXPERTS
    chunk0 = layer * (n_exp // ec)
    nsel = i1.shape[1]
    tok = lambda i, c: (i, 0)
    extra = [] if out_norm_g is None else [out_norm_g.reshape(1, d)]
    return pl.pallas_call(
        _peer_kernel,
        grid=(t // tb, n_exp // ec),
        in_specs=[pl.BlockSpec((tb, d), tok),
                  pl.BlockSpec((tb, nsel), tok),
                  pl.BlockSpec((tb, nsel), tok),
                  pl.BlockSpec((tb, nsel), tok),
                  pl.BlockSpec((tb, d), tok),
                  pl.BlockSpec((ec, d), lambda i, c: (chunk0 + c, 0)),
                  pl.BlockSpec((ec, d), lambda i, c: (chunk0 + c, 0))]
                 + [pl.BlockSpec((1, d), lambda i, c: (0, 0))] * len(extra),
        out_specs=pl.BlockSpec((tb, d), tok),
        out_shape=jax.ShapeDtypeStruct((t, d), F32),
        scratch_shapes=[pltpu.VMEM((tb * G_PITCH, N_KEYS), jnp.uint32),
                        pltpu.VMEM((tb, ec), BF16),
                        pltpu.VMEM((tb, d), F32)],
        compiler_params=_cparams(("parallel", "arbitrary")),
        name="peer_experts",
    )(xn, i1, i2, gate, h, u, v, *extra)


def _peer_ffn(h, g2, wq, subkeys, u, v, layer, out_norm_g=None):
    q, xn = _linear(h, wq.astype(BF16), g=g2, emit_xn=True, name="peer_query")
    i1, i2, gate = _route(q, subkeys)
    return _peer_experts(xn, i1, i2, gate, h, u, v, layer, out_norm_g)


def _band_bias(rows):
    r = np.arange(rows)[:, None]
    c = np.arange(ATT_WINDOW)[None, :]
    delta = ATT_BUF + r - c
    mult = np.zeros(delta.shape, np.float64)
    for window, dil in DILATED_GROUPS:
        mult += (delta >= 0) & (delta <= window) & (delta % dil == 0)
    return jnp.asarray(np.where(mult > 0, np.log(np.maximum(mult, 1.0)), NEG_INF), F32)


def _band_attend(q, kw, vw, bias):
    lane = lax.broadcasted_iota(jnp.int32, q.shape, 1)
    scores = [_dg(jnp.where((lane // HEAD_DIM) == half, q, jnp.zeros_like(q)), kw, _NT) for half in range(2)]
    outs = []
    for s in scores:
        s = s + bias
        m = jnp.max(s, axis=1, keepdims=True)
        p = jnp.exp(s - m)
        den = jnp.sum(p, axis=1, keepdims=True)
        outs.append(_dg(p.astype(BF16), vw, _NN) / den)
    return jnp.where((lane // HEAD_DIM) == 0, outs[0], outs[1])


def _attn_prompt_kernel(q_ref, k_ref, v_ref, bias_ref, o_ref, q_scr, k_scr, v_scr):
    s_len = q_ref.shape[0]
    zeros = jnp.zeros((ATT_BUF, LANES), BF16)
    k_scr[0:ATT_BUF, :] = zeros
    v_scr[0:ATT_BUF, :] = zeros
    k_scr[ATT_BUF:, :] = k_ref[...].astype(BF16)
    v_scr[ATT_BUF:, :] = v_ref[...].astype(BF16)
    q_scr[...] = (q_ref[...] * (HEAD_DIM ** -0.5)).astype(BF16)
    col = lax.broadcasted_iota(jnp.int32, (1, ATT_WINDOW), 1)

    def qblock(qi, carry):
        r0 = pl.multiple_of(qi * ATT_QBLOCK, ATT_QBLOCK)
        bias = jnp.where(col >= ATT_BUF - r0, bias_ref[...], NEG_INF)
        o_ref[pl.ds(r0, ATT_QBLOCK), :] = _band_attend(
            q_scr[pl.ds(r0, ATT_QBLOCK), :], k_scr[pl.ds(r0, ATT_WINDOW), :], v_scr[pl.ds(r0, ATT_WINDOW), :], bias)
        return carry

    lax.fori_loop(0, s_len // ATT_QBLOCK, qblock, 0)


_Q_BLOCK0 = POOL_WIDTH // LANES
_K_BLOCK0 = (POOL_WIDTH + ATT_WIDTH) // LANES
_V_BLOCK0 = (POOL_WIDTH + 2 * ATT_WIDTH) // LANES


def _attn_prompt(proj, n_seq, s_len):
    blk = lambda first: pl.BlockSpec((s_len, LANES), lambda b, hp: (b, first + hp))
    bias = _band_bias(ATT_QBLOCK)
    return pl.pallas_call(
        _attn_prompt_kernel,
        grid=(n_seq, ATT_WIDTH // LANES),
        in_specs=[blk(_Q_BLOCK0), blk(_K_BLOCK0), blk(_V_BLOCK0), pl.BlockSpec(bias.shape, lambda b, hp: (0, 0))],
        out_specs=pl.BlockSpec((s_len, LANES), lambda b, hp: (b, hp)),
        out_shape=jax.ShapeDtypeStruct((n_seq * s_len, ATT_WIDTH), F32),
        scratch_shapes=[pltpu.VMEM((s_len, LANES), BF16),
                        pltpu.VMEM((ATT_BUF + s_len, LANES), BF16),
                        pltpu.VMEM((ATT_BUF + s_len, LANES), BF16)],
        compiler_params=_cparams(("parallel", "parallel")),
        name="attn_prompt",
    )(proj, proj, proj, bias)


def _attn_sample_kernel(*refs):
    q_ref, k_ref, v_ref, ck_ref, cv_ref, bias_ref = refs[:6]
    o_ref, nk_ref, nv_ref, k_scr, v_scr = refs[-5:]
    t = q_ref.shape[0]
    n_past = ck_ref.shape[1]
    assert ATT_WINDOW - n_past == LANES

    def as_columns(x):
        return jnp.concatenate([x, jnp.zeros((LANES - t, LANES), F32)], axis=0).T

    new_k = as_columns(k_ref[...])
    new_v = as_columns(v_ref[...])
    k_scr[:, 0:n_past] = ck_ref[...].astype(BF16)
    v_scr[:, 0:n_past] = cv_ref[...].astype(BF16)
    k_scr[:, n_past:] = new_k.astype(BF16)
    v_scr[:, n_past:] = new_v.astype(BF16)
    q = (q_ref[...] * (HEAD_DIM ** -0.5)).astype(BF16)
    lane = lax.broadcasted_iota(jnp.int32, q.shape, 1)
    scores = [_dg(jnp.where((lane // HEAD_DIM) == half, q, jnp.zeros_like(q)), k_scr[...], _NN) for half in range(2)]
    outs = []
    for s in scores:
        s = s + bias_ref[...]
        m = jnp.max(s, axis=1, keepdims=True)
        p = jnp.exp(s - m)
        den = jnp.sum(p, axis=1, keepdims=True)
        outs.append(_dg(p.astype(BF16), v_scr[...], _NT) / den)
    o_ref[...] = jnp.where((lane // HEAD_DIM) == 0, outs[0], outs[1])

    col = lax.broadcasted_iota(jnp.int32, (LANES, LANES), 1)
    for old_ref, new_cols, out_ref in ((ck_ref, new_k, nk_ref), (cv_ref, new_v, nv_ref)):
        shifted = pltpu.roll(old_ref[...], n_past - t, 1)
        out_ref[:, 0:n_past - LANES] = shifted[:, 0:n_past - LANES]
        out_ref[:, n_past - LANES:] = jnp.where(col < LANES - t, shifted[:, n_past - LANES:],
                                                 pltpu.roll(new_cols, LANES - t, 1))


def _attn_sample(proj, row0, n_seq, t_len, cache_k, cache_v, layer, depth, stacks):
    assert row0 % t_len == 0 and t_len % 8 == 0 and t_len <= ATT_QBLOCK
    blk0 = row0 // t_len
    pairs = ATT_WIDTH // LANES
    blk = lambda first: pl.BlockSpec((t_len, LANES), lambda b, hp: (blk0 + b, first + hp))
    cblk = pl.BlockSpec((LANES, ATT_BUF), lambda b, hp: ((layer * n_seq + b) * pairs + hp, 0))
    bias = _band_bias(t_len)
    stack_shape = jax.ShapeDtypeStruct(cache_k.shape, F32)
    carried = [] if stacks is None else list(stacks)
    return pl.pallas_call(
        _attn_sample_kernel,
        grid=(n_seq, pairs),
        in_specs=[blk(_Q_BLOCK0), blk(_K_BLOCK0), blk(_V_BLOCK0), cblk, cblk,
                  pl.BlockSpec(bias.shape, lambda b, hp: (0, 0))]
                 + [pl.BlockSpec(memory_space=pl.ANY)] * len(carried),
        out_specs=[pl.BlockSpec((t_len, LANES), lambda b, hp: (b, hp)), cblk, cblk],
        out_shape=[jax.ShapeDtypeStruct((n_seq * t_len, ATT_WIDTH), F32), stack_shape, stack_shape],
        input_output_aliases={6 + j: 1 + j for j in range(len(carried))},
        scratch_shapes=[pltpu.VMEM((LANES, ATT_WINDOW), BF16), pltpu.VMEM((LANES, ATT_WINDOW), BF16)],
        compiler_params=_cparams(("parallel", "parallel")),
        name="attn_sample",
    )(proj, proj, proj, cache_k, cache_v, bias, *carried)


def _pool_kernel(x_ref, buf_ref, w_ref, scale_ref, o_ref, *, pos0):
    x = x_ref[...]
    n = x.shape[0]
    xf = jnp.concatenate([buf_ref[...], x], axis=0)
    s2 = xf + pltpu.roll(xf, 1, 0)
    s4 = s2 + pltpu.roll(s2, 2, 0)
    s8 = s4 + pltpu.roll(s4, 4, 0)
    s16 = s8 + pltpu.roll(s8, 8, 0)
    group = lax.broadcasted_iota(jnp.int32, (1, POOL_WIDTH), 1) // POOL_GROUP
    sums = jnp.where(group == 0, s2, jnp.where(group == 1, s4, jnp.where(group == 2, s8, s16)))[POOL_BUF + 1:]
    win = jnp.where(group == 0, 2.0, jnp.where(group == 1, 4.0, jnp.where(group == 2, 8.0, 16.0)))
    seen = (lax.broadcasted_iota(jnp.int32, (n, 1), 0) + (pos0 + 1)).astype(F32)
    diff = sums / jnp.minimum(win, seen) - x
    o_ref[...] = _mm3(diff, w_ref[...]) * scale_ref[...]


def _pool(proj, row0, n_seq, s_len, buf, w, scale, pos0):
    assert POOL_WINDOWS == (2, 4, 8, 16) and row0 % s_len == 0
    blk0 = row0 // s_len
    wbd = jnp.zeros((POOL_WIDTH, POOL_WIDTH), F32)
    for i in range(len(POOL_WINDOWS)):
        wbd = wbd.at[i * POOL_GROUP:(i + 1) * POOL_GROUP, i * POOL_GROUP:(i + 1) * POOL_GROUP].set(w[i])
    buf16 = jnp.pad(buf, ((0, 0), (1, 0), (0, 0))).reshape(n_seq * (POOL_BUF + 1), POOL_WIDTH)
    return pl.pallas_call(
        functools.partial(_pool_kernel, pos0=pos0),
        grid=(n_seq,),
        in_specs=[pl.BlockSpec((s_len, POOL_WIDTH), lambda b: (blk0 + b, 0)),
                  pl.BlockSpec((POOL_BUF + 1, POOL_WIDTH), lambda b: (b, 0)),
                  pl.BlockSpec((POOL_WIDTH, POOL_WIDTH), lambda b: (0, 0)),
                  pl.BlockSpec((1, POOL_WIDTH), lambda b: (0, 0))],
        out_specs=pl.BlockSpec((s_len, POOL_WIDTH), lambda b: (b, 0)),
        out_shape=jax.ShapeDtypeStruct((n_seq * s_len, POOL_WIDTH), F32),
        compiler_params=_cparams(("parallel",)),
        name="pool_mixer",
    )(proj, buf16, wbd, scale.reshape(1, POOL_WIDTH))


def _split(x):
    hi = x.astype(BF16)
    return hi, (x - hi.astype(F32)).astype(BF16)


def _dg(a, b, dims):
    return lax.dot_general(a, b, (dims, ((), ())), preferred_element_type=F32)


_NN = ((1,), (0,))
_NT = ((1,), (1,))
_TN = ((0,), (0,))


def _mm3(a, b, dims=_NN):
    a_hi, a_lo = _split(a)
    b_hi, b_lo = _split(b)
    return _dg(a_hi, b_hi, dims) + (_dg(a_lo, b_hi, dims) + _dg(a_hi, b_lo, dims))


def _split3(x):
    x0 = x.astype(BF16)
    r1 = x - x0.astype(F32)
    x1 = r1.astype(BF16)
    return x0, x1, (r1 - x1.astype(F32)).astype(BF16)


def _mm_exact_rhs(a, b_bf16):
    a0, a1, a2 = _split3(a)
    return _dg(a0, b_bf16, _NN) + (_dg(a1, b_bf16, _NN) + _dg(a2, b_bf16, _NN))


def _mm_exact_lhs(a_bf16, b):
    b0, b1, b2 = _split3(b)
    return _dg(a_bf16, b0, _NN) + (_dg(a_bf16, b1, _NN) + _dg(a_bf16, b2, _NN))


def _rwkv_chunk(pblks, prev_rows, s0, prm):
    (mu, w0, w_up, a0, a_up, g_up, k_k, k_a, r_k, ln_w, ln_b) = prm
    n_seq = len(pblks)
    c = pblks[0].shape[0]
    n = n_seq * c
    pblk = jnp.concatenate(pblks, axis=0) if n_seq > 1 else pblks[0]
    shifted = [jnp.concatenate([prev_rows[i], pblks[i][:-1]], axis=0) if c > 1 else prev_rows[i]
               for i in range(n_seq)]
    shifted = jnp.concatenate(shifted, axis=0) if n_seq > 1 else shifted[0]
    ps = pblk + (shifted - pblk) * mu
    c1, c2, c3 = RWKV_WIDTH, 2 * RWKV_WIDTH, 3 * RWKV_WIDTH
    c4 = c3 + DECAY_LORA
    c5 = c4 + ICLR_LORA
    r, k, v = ps[:, :c1], ps[:, c1:c2], ps[:, c2:c3]
    wd, ad, gd = ps[:, c3:c4], ps[:, c4:c5], ps[:, c5:]

    z = -(w0 + _mm3(jnp.tanh(wd), w_up))
    softplus = jnp.maximum(z, 0.0) + jnp.log(1.0 + jnp.exp(-jnp.abs(z)))
    lw = -jnp.exp(-softplus - 0.5)
    a = jax.nn.sigmoid(a0 + _mm3(ad, a_up))
    g = _mm3(jax.nn.sigmoid(gd), g_up)

    ch_r = lax.broadcasted_iota(jnp.int32, (RWKV_WIDTH, RWKV_WIDTH), 0) // HEAD_DIM
    ch_c = lax.broadcasted_iota(jnp.int32, (RWKV_WIDTH, RWKV_WIDTH), 1) // HEAD_DIM
    same_head = jnp.where(ch_r == ch_c, 1.0, 0.0).astype(BF16)

    kk = k * k_k
    kk = kk / jnp.maximum(jnp.sqrt(_mm_exact_rhs(kk * kk, same_head)), 1e-12)
    k2 = k * (1.0 + (a - 1.0) * k_a)
    bonus = _mm_exact_rhs(r * k2 * r_k, same_head) * v

    n_r = lax.broadcasted_iota(jnp.int32, (n, n), 0)
    n_c = lax.broadcasted_iota(jnp.int32, (n, n), 1)
    run = (n_r // c == n_c // c) & (n_c <= n_r)
    cs = _mm_exact_lhs(jnp.where(run, 1.0, 0.0).astype(BF16), lw)
    e_up = jnp.exp(cs)
    e_dn = jnp.exp(-cs)
    kap = kk * jnp.exp(cs - lw)
    bet = kk * a * e_dn
    kt = k2 * e_dn
    rt = r * e_up

    t_r = lax.broadcasted_iota(jnp.int32, (c, c), 0)
    t_c = lax.broadcasted_iota(jnp.int32, (c, c), 1)
    incl = t_c <= t_r
    strict = t_c < t_r
    eye = jnp.where(t_r == t_c, 1.0, 0.0)
    chains = [(i, h) for i in range(n_seq) for h in range(RWKV_HEADS)]
    cut = lambda x, i, h: x[i * c:(i + 1) * c, h * HEAD_DIM:(h + 1) * HEAD_DIM]
    kp_x = [cut(kap, i, h) for i, h in chains]
    bt_x = [cut(bet, i, h) for i, h in chains]
    kt_x = [cut(kt, i, h) for i, h in chains]
    rt_x = [cut(rt, i, h) for i, h in chains]
    v_x = [cut(v, i, h) for i, h in chains]
    s_x = [s0[i][h] for i, h in chains]
    every = range(len(chains))
    quad = [_mm3(jnp.concatenate([kp_x[x], rt_x[x]], axis=0), jnp.concatenate([bt_x[x], kt_x[x]], axis=0), _NT)
            for x in every]
    a_b = [jnp.where(strict, quad[x][:c, :c], 0.0) for x in every]
    a_k = [jnp.where(strict, quad[x][:c, c:], 0.0) for x in every]
    m_b = [jnp.where(incl, quad[x][c:, :c], 0.0) for x in every]
    m_k = [jnp.where(incl, quad[x][c:, c:], 0.0) for x in every]
    akv = [_mm3(a_k[x], v_x[x]) for x in every]
    mkv = [_mm3(m_k[x], v_x[x]) for x in every]
    rs = [_mm3(rt_x[x], s_x[x], _NT) for x in every]
    vk = [_mm3(v_x[x], kt_x[x], _TN) for x in every]
    inv = [eye - jnp.where(t_r // 2 == t_c // 2, a_b[x], 0.0) for x in every]
    b = 2
    while b < c:
        pair = (t_r // (2 * b) == t_c // (2 * b)) & ((t_r // b) % 2 == 1) & ((t_c // b) % 2 == 0)
        low = [_mm3(jnp.where(pair, a_b[x], 0.0), inv[x]) for x in every]
        inv = [inv[x] - _mm3(inv[x], low[x]) for x in every]
        b *= 2
    pq = [_mm3(inv[x], jnp.concatenate([kp_x[x], akv[x]], axis=1)) for x in every]
    p_x = [pq[x][:, :HEAD_DIM] for x in every]
    q_x = [pq[x][:, HEAD_DIM:] for x in every]
    u_x = [_mm3(p_x[x], s_x[x], _NT) + q_x[x] for x in every]
    o_x = [rs[x] + mkv[x] - _mm3(m_b[x], u_x[x]) for x in every]
    ub = [_mm3(u_x[x], bt_x[x], _TN) for x in every]
    s_new = [[None] * RWKV_HEADS for _ in range(n_seq)]
    for x, (i, h) in enumerate(chains):
        gam = e_up[(i + 1) * c - 1:(i + 1) * c, h * HEAD_DIM:(h + 1) * HEAD_DIM]
        s_new[i][h] = (s_x[x] + vk[x] - ub[x]) * gam
    o = [jnp.concatenate(o_x[i * RWKV_HEADS:(i + 1) * RWKV_HEADS], axis=1) for i in range(n_seq)]
    o = jnp.concatenate(o, axis=0) if n_seq > 1 else o[0]
    mean = _mm_exact_rhs(o, same_head) * (1.0 / HEAD_DIM)
    d = o - mean
    var = _mm_exact_rhs(d * d, same_head) * (1.0 / HEAD_DIM)
    o = (d * lax.rsqrt(var + GN_EPS) * ln_w + ln_b + bonus) * g
    return [o[i * c:(i + 1) * c] for i in range(n_seq)], s_new


def _rwkv_kernel(*refs, rows):
    p_refs = refs[:rows]
    (prev_ref, s0_ref, mu_ref, w0_ref, wup_ref, a0_ref, aup_ref, gup_ref, kk_ref, ka_ref, rk_ref, lnw_ref, lnb_ref,
     o_ref, sT_ref, s_scr, last_scr) = refs[rows:]
    ci = pl.program_id(1)

    @pl.when(ci == 0)
    def _init():
        s_scr[...] = s0_ref[...]
        last_scr[...] = prev_ref[...]

    prm = (mu_ref[...], w0_ref[...], wup_ref[...], a0_ref[...], aup_ref[...], gup_ref[...], kk_ref[...],
           ka_ref[...], rk_ref[...], lnw_ref[...], lnb_ref[...])
    pblks = [p_refs[i][...] for i in range(rows)]
    outs, s_new = _rwkv_chunk(pblks, [last_scr[i] for i in range(rows)],
                              [[s_scr[i, h] for h in range(RWKV_HEADS)] for i in range(rows)], prm)
    c = pblks[0].shape[0]
    for i in range(rows):
        o_ref[i] = outs[i]
        for h in range(RWKV_HEADS):
            s_scr[i, h] = s_new[i][h]
        last_scr[i] = pblks[i][c - 1:c, :]

    @pl.when(ci == pl.num_programs(1) - 1)
    def _fin():
        sT_ref[...] = s_scr[...]


def _rwkv_mixer(proj, row0, n_seq, s_len, prev, wkv0, lp, chunk, rows=RWKV_ROWS):
    assert s_len % chunk == 0 and n_seq % rows == 0 and row0 % chunk == 0
    assert proj.shape[1] == 2 * RWKV_PROJ
    n_chunks = s_len // chunk
    blk0 = row0 // chunk
    row = lambda x: x.reshape(1, -1).astype(F32)
    full = lambda a: pl.BlockSpec(a.shape, lambda b, c: (0,) * a.ndim)
    weights = [row(lp['rwkv_mu']), row(lp['rwkv_w0']), lp['rwkv_w_up'], row(lp['rwkv_a0']), lp['rwkv_a_up'],
               lp['rwkv_g_up'], row(lp['rwkv_k_k']), row(lp['rwkv_k_a']), row(lp['rwkv_r_k']),
               row(lp['rwkv_ln_w']), row(lp['rwkv_ln_b'])]
    p_spec = lambda i: pl.BlockSpec((chunk, RWKV_PROJ), lambda b, c: (blk0 + (b * rows + i) * n_chunks + c, 1))
    out, s_t = pl.pallas_call(
        functools.partial(_rwkv_kernel, rows=rows),
        grid=(n_seq // rows, n_chunks),
        in_specs=[p_spec(i) for i in range(rows)]
                 + [pl.BlockSpec((rows, 1, RWKV_PROJ), lambda b, c: (b, 0, 0)),
                    pl.BlockSpec((rows, RWKV_HEADS, HEAD_DIM, HEAD_DIM), lambda b, c: (b, 0, 0, 0))]
                 + [full(w) for w in weights],
        out_specs=[pl.BlockSpec((rows, chunk, RWKV_WIDTH), lambda b, c: (b, c, 0)),
                   pl.BlockSpec((rows, RWKV_HEADS, HEAD_DIM, HEAD_DIM), lambda b, c: (b, 0, 0, 0))],
        out_shape=[jax.ShapeDtypeStruct((n_seq, s_len, RWKV_WIDTH), F32),
                   jax.ShapeDtypeStruct((n_seq, RWKV_HEADS, HEAD_DIM, HEAD_DIM), F32)],
        scratch_shapes=[pltpu.VMEM((rows, RWKV_HEADS, HEAD_DIM, HEAD_DIM), F32),
                        pltpu.VMEM((rows, 1, RWKV_PROJ), F32)],
        compiler_params=_cparams(("parallel", "arbitrary")),
        name="rwkv_mixer",
    )(*([proj] * rows), prev.reshape(n_seq, 1, RWKV_PROJ), wkv0, *weights)
    return out.reshape(n_seq * s_len, RWKV_WIDTH), s_t


def kernel(x_prompt, x_sample, state_pool, cache_k, cache_v, state_shift, state_wkv, norm1_g, norm2_g, final_g, w_in, w_out, pool_w, pool_scale, rwkv_mu, rwkv_w0, rwkv_w_up, rwkv_a0, rwkv_a_up, rwkv_g_up, rwkv_k_k, rwkv_k_a, rwkv_r_k, rwkv_ln_w, rwkv_ln_b, peer_wq, peer_subkeys, peer_u, peer_v):
    bp, sp_len, d = x_prompt.shape
    bs, ss_len, _ = x_sample.shape
    tp = bp * sp_len
    ts = bs * ss_len
    depth = w_in.shape[0]
    keep = min(ATT_BUF, sp_len)
    h = jnp.concatenate([x_prompt.reshape(tp, d), x_sample.reshape(ts, d)], axis=0)
    pos = jnp.concatenate([jnp.tile(jnp.arange(sp_len, dtype=jnp.int32), bp),
                           jnp.tile(PAST_LEN + jnp.arange(ss_len, dtype=jnp.int32), bs)])
    rope_tabs = _rope_tables(pos)
    zero_buf = jnp.zeros((bp, POOL_BUF, POOL_WIDTH), F32)
    zero_shift = jnp.zeros((bp, RWKV_PROJ), F32)
    zero_wkv = jnp.zeros((bp, RWKV_HEADS, HEAD_DIM, HEAD_DIM), F32)
    k_lo, v_lo, r_lo = POOL_WIDTH + ATT_WIDTH, POOL_WIDTH + 2 * ATT_WIDTH, POOL_WIDTH + 3 * ATT_WIDTH
    heads = lambda x, n: x.reshape(n, -1, ATT_HEADS, HEAD_DIM)
    outs_p = [[] for _ in range(5)]
    outs_s = [[] for _ in range(5)]
    kv_stacks = None
    feature_major = lambda x: jnp.transpose(x, (0, 1, 3, 4, 2)).reshape(depth * bs * ATT_WIDTH, ATT_BUF)
    ck_all = feature_major(cache_k)
    cv_all = feature_major(cache_v)
    u_all = peer_u.astype(BF16).reshape(depth * N_EXPERTS, d)
    v_all = peer_v.astype(BF16).reshape(depth * N_EXPERTS, d)
    for l in range(depth):
        lp = {'rwkv_mu': rwkv_mu[l], 'rwkv_w0': rwkv_w0[l],
              'rwkv_w_up': rwkv_w_up[l], 'rwkv_a0': rwkv_a0[l], 'rwkv_a_up': rwkv_a_up[l],
              'rwkv_g_up': rwkv_g_up[l], 'rwkv_k_k': rwkv_k_k[l], 'rwkv_k_a': rwkv_k_a[l],
              'rwkv_r_k': rwkv_r_k[l], 'rwkv_ln_w': rwkv_ln_w[l], 'rwkv_ln_b': rwkv_ln_b[l]}
        proj = _linear(h, w_in[l].astype(BF16), g=norm1_g[l], rope=rope_tabs, name="in_proj")
        proj_s = proj[tp:].reshape(bs, ss_len, IN_WIDTH)

        pool_p = _pool(proj, 0, bp, sp_len, zero_buf, pool_w[l], pool_scale[l], 0)
        pool_s = _pool(proj, tp, bs, ss_len, state_pool[l], pool_w[l], pool_scale[l], PAST_LEN)
        att_p = _attn_prompt(proj, bp, sp_len)
        att_s, *kv_stacks = _attn_sample(proj, tp, bs, ss_len, ck_all, cv_all, l, depth, kv_stacks)
        rw_p, wkv_p = _rwkv_mixer(proj, 0, bp, sp_len, zero_shift, zero_wkv, lp, min(sp_len, RWKV_CHUNK))
        rw_s, wkv_s = _rwkv_mixer(proj, tp, bs, ss_len, state_shift[l], state_wkv[l], lp, min(ss_len, RWKV_CHUNK))

        tail = lambda n, lo, hi: jnp.stack([proj[(b + 1) * sp_len - n:(b + 1) * sp_len, lo:hi] for b in range(bp)])
        outs_p[0].append(tail(POOL_BUF, 0, POOL_WIDTH))
        outs_p[1].append(heads(tail(keep, k_lo, v_lo), bp))
        outs_p[2].append(heads(tail(keep, v_lo, r_lo), bp))
        outs_p[3].append(tail(1, r_lo, IN_WIDTH)[:, 0])
        outs_p[4].append(wkv_p)
        outs_s[0].append(jnp.concatenate([state_pool[l], proj_s[:, :, :POOL_WIDTH]], axis=1)[:, -POOL_BUF:])
        outs_s[3].append(proj_s[:, -1, r_lo:])
        outs_s[4].append(wkv_s)

        mixed = jnp.concatenate([jnp.concatenate([pool_p, att_p, rw_p], axis=1),
                                 jnp.concatenate([pool_s, att_s, rw_s], axis=1)], axis=0)
        h = _linear(mixed, w_out[l].astype(BF16), resid=h, name="out_proj")
        h = _peer_ffn(h, norm2_g[l], peer_wq[l], peer_subkeys[l], u_all, v_all, l,
                      final_g if l == depth - 1 else None)
    y = h
    y_prompt = y[:tp].reshape(bp, sp_len, d)
    y_sample = y[tp:].reshape(bs, ss_len, d)
    s_k, s_v = (jnp.transpose(x.reshape(depth, bs, ATT_HEADS, HEAD_DIM, ATT_BUF), (0, 1, 4, 2, 3)) for x in kv_stacks)
    return (y_prompt, y_sample, *[jnp.stack(a) for a in outs_p],
            jnp.stack(outs_s[0]), s_k, s_v, jnp.stack(outs_s[3]), jnp.stack(outs_s[4]))
```
